```python
import math
import jax, jax.numpy as jnp
from jax import lax
import numpy as np

D_MODEL = 2048
BATCH = 2
SEQ = 8192
DEPTH = 2

MIX_W = D_MODEL // 2
N_BRANCH = 3
RET_HEADS = 8
RET_DK = MIX_W // RET_HEADS
RET_DV = MIX_W // RET_HEADS
RET_CHUNK = 128
RET_ROPE_BASE = 10000.0
CONV_CH = MIX_W
CONV_GROUPS = 8
CONV_K = 3
ATT_HEADS = 8
ATT_DH = MIX_W // ATT_HEADS
IDX_HEADS = 8
IDX_DH = 64
TOPK_MAX = 256
ATT_BLOCK = 128
REL_BUCKETS = 32
REL_MAX_DIST = 128
D_FF = 5632
N_EXPERTS = 8
TOP_K = 2
D_FF_EXPERT = 7168
EPS = 1e-6

N_DENSE = (DEPTH + 1) // 2
N_MOE = DEPTH // 2
IN_SPLITS = (RET_HEADS * RET_DK, RET_HEADS * RET_DK, RET_HEADS * RET_DV, RET_HEADS * RET_DV,
             CONV_CH, CONV_CH, CONV_CH,
             ATT_HEADS * ATT_DH, ATT_HEADS * ATT_DH, ATT_HEADS * ATT_DH,
             IDX_HEADS * IDX_DH, IDX_DH, IDX_HEADS,
             N_BRANCH * D_MODEL)
IN_COLS = sum(IN_SPLITS)

kernel_name = "hybrid_retention_shortconv_dsa_moe"


def rmsnorm(x, g):
    xf = x.astype(jnp.float32)
    y = xf * lax.rsqrt(jnp.mean(xf * xf, axis=-1, keepdims=True) + EPS)
    return (y * g.astype(jnp.float32)).astype(x.dtype)


def rotary(x, positions):
    d = x.shape[-1]
    inv_freq = RET_ROPE_BASE ** (-jnp.arange(0, d, 2, dtype=jnp.float32) / d)
    ang = positions.astype(jnp.float32)[..., None] * inv_freq
    cos = jnp.cos(ang)[:, :, None, :]
    sin = jnp.sin(ang)[:, :, None, :]
    xf = x.astype(jnp.float32)
    x1, x2 = xf[..., : d // 2], xf[..., d // 2:]
    return jnp.concatenate([x1 * cos - x2 * sin, x1 * sin + x2 * cos], axis=-1).astype(x.dtype)


def retention(q, k, v):
    B, S, H, dk = q.shape
    dv = v.shape[-1]
    C = RET_CHUNK
    N = S // C
    log_g = jnp.log1p(-jnp.exp2(-5.0 - jnp.arange(H, dtype=jnp.float32)))
    pos = jnp.arange(C, dtype=jnp.float32)
    diff = pos[:, None] - pos[None, :]
    intra_decay = jnp.where(diff >= 0, jnp.exp(jnp.maximum(diff, 0.0)[None] * log_g[:, None, None]),
                            0.0).astype(q.dtype)
    k_decay = jnp.exp((C - 1 - pos)[:, None] * log_g[None, :]).astype(q.dtype)
    q_decay = jnp.exp((pos + 1)[:, None] * log_g[None, :]).astype(q.dtype)
    chunk_decay = jnp.exp(C * log_g).astype(q.dtype)
    qc = q.reshape(B, N, C, H, dk)
    kc = (k * dk ** -0.5).reshape(B, N, C, H, dk)
    vc = v.reshape(B, N, C, H, dv)
    scores = jnp.einsum('bnihd,bnjhd->bnhij', qc, kc) * intra_decay
    intra = jnp.einsum('bnhij,bnjhe->bnihe', scores, vc)
    kv = jnp.einsum('bnjhd,jh,bnjhe->nbhde', kc, k_decay, vc)

    def step(state, kv_n):
        return state * chunk_decay[:, None, None] + kv_n, state

    _, prev = lax.scan(step, jnp.zeros((B, H, dk, dv), kv.dtype), kv)
    cross = jnp.einsum('bnihd,nbhde->bnihe', qc, prev) * q_decay[:, :, None]
    return (intra + cross).reshape(B, S, H, dv)


def head_groupnorm(y, w, b):
    B, S, H, dv = y.shape
    yf = y.astype(jnp.float32)
    mu = jnp.mean(yf, axis=-1, keepdims=True)
    var = jnp.mean(jnp.square(yf - mu), axis=-1, keepdims=True)
    out = ((yf - mu) * lax.rsqrt(var + EPS)).reshape(B, S, H * dv)
    return (out * w.astype(jnp.float32) + b.astype(jnp.float32)).astype(y.dtype)


def short_gated_conv(b_gate, c_gate, u_in, conv_w):
    u = c_gate * u_in
    conv = lax.conv_general_dilated(u, conv_w[:, None, :].astype(u.dtype), window_strides=(1,),
                                    padding=[(CONV_K - 1, 0)],
                                    dimension_numbers=('NWC', 'WIO', 'NWC'),
                                    feature_group_count=CONV_CH)
    return b_gate * conv


def head_rms(x, g):
    xf = x.astype(jnp.float32)
    y = xf * lax.rsqrt(jnp.mean(xf * xf, axis=-1, keepdims=True) + EPS)
    return (y * g.astype(jnp.float32)).astype(x.dtype)


def t5_bucket(rel):
    n = jnp.maximum(rel, 0)
    max_exact = REL_BUCKETS // 2
    nf = jnp.maximum(n, 1).astype(jnp.float32)
    large = max_exact + (jnp.log(nf / max_exact) / math.log(REL_MAX_DIST / max_exact)
                         * (REL_BUCKETS - max_exact)).astype(jnp.int32)
    large = jnp.minimum(large, REL_BUCKETS - 1)
    return jnp.where(n < max_exact, n, large)


def dsa_attention(q, k, v, qi, ki, wi, positions, rel_bias):
    B, S, H, dh = q.shape
    k_sel = min(TOPK_MAX, S // 4)
    nb = S // ATT_BLOCK

    def blocks(a):
        return a.reshape((B, nb, ATT_BLOCK) + a.shape[2:]).swapaxes(0, 1)

    starts = jnp.arange(nb, dtype=jnp.int32) * ATT_BLOCK
    key_idx = jnp.arange(S, dtype=jnp.int32)
    gather = jax.vmap(lambda table, idx: table[idx])

    def one_block(args):
        q_b, qi_b, w_b, pos_b, start = args
        t_idx = start + jnp.arange(ATT_BLOCK, dtype=jnp.int32)
        dots = jnp.einsum('bthd,bsd->bths', qi_b, ki)
        score = jnp.einsum('bth,bths->bts', w_b, jax.nn.relu(dots)).astype(jnp.float32)
        causal = key_idx[None, :] <= t_idx[:, None]
        score = jnp.where(causal[None], score, -jnp.inf)
        _, idx = lax.top_k(score, k_sel)
        k_g = gather(k, idx)
        v_g = gather(v, idx)
        p_g = gather(positions, idx)
        bias = rel_bias[t5_bucket(pos_b[:, :, None] - p_g)]
        logits = (jnp.einsum('bthd,btkhd->bhtk', q_b, k_g).astype(jnp.float32) * dh ** -0.5
                  + bias.astype(jnp.float32).transpose(0, 3, 1, 2))
        valid = idx <= t_idx[None, :, None]
        logits = jnp.where(valid[:, None], logits, -jnp.inf)
        p = jax.nn.softmax(logits, axis=-1).astype(v.dtype)
        return jnp.einsum('bhtk,btkhd->bthd', p, v_g)

    out = lax.map(one_block, (blocks(q), blocks(qi), blocks(wi), blocks(positions), starts))
    return out.swapaxes(0, 1).reshape(B, S, H * dh)


def mixer(h, positions, rel_bias, w_in, w_br, w_o, ret_gn_w, ret_gn_b, conv_w, q_norm, k_norm):
    B, S, _ = h.shape
    points = [int(p) for p in np.cumsum(IN_SPLITS)[:-1]]
    (rq, rk, rv, rg, cb, cc, cu, aq, ak, av, iq, ik, iw, gates) = jnp.split(h @ w_in, points, axis=-1)
    rq = rotary(rq.reshape(B, S, RET_HEADS, RET_DK), positions)
    rk = rotary(rk.reshape(B, S, RET_HEADS, RET_DK), positions)
    ret = retention(rq, rk, rv.reshape(B, S, RET_HEADS, RET_DV))
    y_ret = jax.nn.silu(rg) * head_groupnorm(ret, ret_gn_w, ret_gn_b)
    y_conv = short_gated_conv(cb, cc, cu, conv_w)
    q = head_rms(aq.reshape(B, S, ATT_HEADS, ATT_DH), q_norm)
    k = head_rms(ak.reshape(B, S, ATT_HEADS, ATT_DH), k_norm)
    v = av.reshape(B, S, ATT_HEADS, ATT_DH)
    qi = iq.reshape(B, S, IDX_HEADS, IDX_DH) * IDX_DH ** -0.5
    wi = iw * IDX_HEADS ** -0.5
    y_att = dsa_attention(q, k, v, qi, ik, wi, positions, rel_bias)
    y = jnp.stack([y_ret.astype(h.dtype), y_conv.astype(h.dtype), y_att.astype(h.dtype)], axis=0)
    branch = jnp.einsum('nbsc,ncd->nbsd', y, w_br)
    g = jax.nn.sigmoid(gates.reshape(B, S, N_BRANCH, D_MODEL))
    merged = jnp.einsum('bsnd,nbsd->bsd', g, branch)
    return merged @ w_o


def swiglu(h, w1, w3, w2):
    return (jax.nn.silu(h @ w1) * (h @ w3)) @ w2


def moe(h, router, w1, w3, w2):
    logits = (h @ router).astype(jnp.float32)
    top_vals, top_idx = lax.top_k(logits, TOP_K)
    top_w = jax.nn.softmax(top_vals, axis=-1)
    combine = jnp.einsum('bske,bsk->bse', jax.nn.one_hot(top_idx, N_EXPERTS, dtype=jnp.float32), top_w)
    out = jnp.zeros_like(h)
    for e in range(N_EXPERTS):
        out = out + combine[..., e:e + 1].astype(h.dtype) * swiglu(h, w1[e], w3[e], w2[e])
    return out


def setup_inputs(seed: int = 0) -> dict:
    key = jax.random.key(seed)
    ks = jax.random.split(key, 20)
    f32 = jnp.float32
    nrm = lambda k, shape, s: jax.random.normal(k, shape, f32) * s
    return {
        "x": nrm(ks[0], (BATCH, SEQ, D_MODEL), 1.0),
        "positions": jnp.broadcast_to(jnp.arange(SEQ, dtype=jnp.int32), (BATCH, SEQ)),
        "rel_bias": nrm(ks[1], (REL_BUCKETS, ATT_HEADS), 0.5),
        "norm_mix": 1.0 + nrm(ks[2], (DEPTH, D_MODEL), 0.02),
        "norm_ffn": 1.0 + nrm(ks[3], (DEPTH, D_MODEL), 0.02),
        "w_in": nrm(ks[4], (DEPTH, D_MODEL, IN_COLS), D_MODEL ** -0.5),
        "w_br": nrm(ks[5], (DEPTH, N_BRANCH, MIX_W, D_MODEL), MIX_W ** -0.5),
        "w_o": nrm(ks[6], (DEPTH, D_MODEL, D_MODEL), D_MODEL ** -0.5),
        "ret_gn_w": 1.0 + nrm(ks[7], (DEPTH, MIX_W), 0.02),
        "ret_gn_b": nrm(ks[8], (DEPTH, MIX_W), 0.02),
        "conv_w": nrm(ks[9], (DEPTH, CONV_K, CONV_CH), CONV_K ** -0.5),
        "q_norm": 1.0 + nrm(ks[10], (DEPTH, ATT_DH), 0.02),
        "k_norm": 1.0 + nrm(ks[11], (DEPTH, ATT_DH), 0.02),
        "ffn_w1": nrm(ks[12], (N_DENSE, D_MODEL, D_FF), D_MODEL ** -0.5),
        "ffn_w3": nrm(ks[13], (N_DENSE, D_MODEL, D_FF), D_MODEL ** -0.5),
        "ffn_w2": nrm(ks[14], (N_DENSE, D_FF, D_MODEL), D_FF ** -0.5),
        "moe_router": nrm(ks[15], (N_MOE, D_MODEL, N_EXPERTS), D_MODEL ** -0.5),
        "moe_w1": nrm(ks[16], (N_MOE, N_EXPERTS, D_MODEL, D_FF_EXPERT), D_MODEL ** -0.5),
        "moe_w3": nrm(ks[17], (N_MOE, N_EXPERTS, D_MODEL, D_FF_EXPERT), D_MODEL ** -0.5),
        "moe_w2": nrm(ks[18], (N_MOE, N_EXPERTS, D_FF_EXPERT, D_MODEL), D_FF_EXPERT ** -0.5),
    }


def reference(x, positions, rel_bias, norm_mix, norm_ffn, w_in, w_br, w_o, ret_gn_w, ret_gn_b,
              conv_w, q_norm, k_norm, ffn_w1, ffn_w3, ffn_w2, moe_router, moe_w1, moe_w3, moe_w2):
    for l in range(DEPTH):
        h = rmsnorm(x, norm_mix[l])
        x = x + mixer(h, positions, rel_bias, w_in[l], w_br[l], w_o[l], ret_gn_w[l], ret_gn_b[l],
                      conv_w[l], q_norm[l], k_norm[l])
        h = rmsnorm(x, norm_ffn[l])
        if l % 2 == 0:
            d = l // 2
            x = x + swiglu(h, ffn_w1[d], ffn_w3[d], ffn_w2[d])
        else:
            m = l // 2
            x = x + moe(h, moe_router[m], moe_w1[m], moe_w3[m], moe_w2[m])
    return x
```

```python
import functools
import math

import numpy as np
import jax
import jax.numpy as jnp
from jax import lax
from jax.experimental import pallas as pl
from jax.experimental.pallas import tpu as pltpu

F32 = jnp.float32
BF16 = jnp.bfloat16
I32 = jnp.int32

EPS = 1e-6
N_BRANCH = 3
RET_HEADS = 8
RET_CHUNK = 128
RET_ROPE_BASE = 10000.0
CONV_K = 3
ATT_HEADS = 8
IDX_HEADS = 8
IDX_DH = 64
TOPK_MAX = 256
REL_BUCKETS = 32
REL_MAX_DIST = 128
TOP_K = 2

LANES = 128
SUBLANES = 8
VMEM_LIMIT_BYTES = 56 * 1024 * 1024

NEG_MASK = -1e30
INT_MIN = -2 ** 31


def _cparams(sem):
    return pltpu.CompilerParams(dimension_semantics=sem, vmem_limit_bytes=VMEM_LIMIT_BYTES)


def _sigmoid(x):
    return 1.0 / (1.0 + jnp.exp(-x))


def _tile(n, pref):
    t = min(n, pref)
    assert n % t == 0, (n, pref)
    return t


def _rmsnorm_kernel(x_ref, g_ref, o_ref):
    x = x_ref[...]
    ms = jnp.mean(x * x, axis=-1, keepdims=True)
    o_ref[...] = (x * lax.rsqrt(ms + EPS) * g_ref[...]).astype(o_ref.dtype)


def _rmsnorm(x2, g):
    t, d = x2.shape
    tm = _tile(t, 512)
    return pl.pallas_call(
        _rmsnorm_kernel,
        out_shape=jax.ShapeDtypeStruct((t, d), BF16),
        grid=(t // tm,),
        in_specs=[pl.BlockSpec((tm, d), lambda i: (i, 0)),
                  pl.BlockSpec((1, d), lambda i: (0, 0))],
        out_specs=pl.BlockSpec((tm, d), lambda i: (i, 0)),
        compiler_params=_cparams(("parallel",)),
        name="rmsnorm",
    )(x2, g.reshape(1, d))


def _mm_kernel(a_ref, b_ref, o_ref):
    o_ref[...] = jnp.dot(a_ref[...], b_ref[...], preferred_element_type=F32).astype(o_ref.dtype)


def _mm_res_kernel(a_ref, b_ref, r_ref, o_ref):
    o_ref[...] = (r_ref[...] + jnp.dot(a_ref[...], b_ref[...], preferred_element_type=F32)).astype(o_ref.dtype)


def _matmul(a, b, out_dtype, tm, tn, res=None, name="matmul"):
    m, k = a.shape
    k2, n = b.shape
    assert k == k2
    tm = _tile(m, tm)
    tn = _tile(n, tn)
    in_specs = [pl.BlockSpec((tm, k), lambda i, j: (i, 0)),
                pl.BlockSpec((k, tn), lambda i, j: (0, j))]
    args = [a, b]
    kern = _mm_kernel
    if res is not None:
        in_specs.append(pl.BlockSpec((tm, tn), lambda i, j: (i, j)))
        args.append(res)
        kern = _mm_res_kernel
    return pl.pallas_call(
        kern,
        out_shape=jax.ShapeDtypeStruct((m, n), out_dtype),
        grid=(m // tm, n // tn),
        in_specs=in_specs,
        out_specs=pl.BlockSpec((tm, tn), lambda i, j: (i, j)),
        compiler_params=_cparams(("parallel", "parallel")),
        name=name,
    )(*args)


def _ret_kernel(pos_ref, invf_ref, q_ref, k_ref, v_ref, g_ref, dec_ref, kdec_ref, qdec_ref,
                gnw_ref, gnb_ref, o_ref, state_ref, *, heads, dh, chunk_decay):
    @pl.when(pl.program_id(1) == 0)
    def _():
        state_ref[...] = jnp.zeros_like(state_ref)

    c = q_ref.shape[1]
    ang = pos_ref[0].astype(F32) * invf_ref[...]
    cos = jnp.cos(ang)
    sin = jnp.sin(ang)
    lane = lax.broadcasted_iota(I32, (c, dh), 1)
    sin_signed = jnp.where(lane < dh // 2, -sin, sin)
    scale = dh ** -0.5
    for h in range(heads):
        sl = slice(h * dh, (h + 1) * dh)
        q = q_ref[0, :, sl].astype(F32)
        k = k_ref[0, :, sl].astype(F32)
        v = v_ref[0, :, sl]
        qr = (q * cos + pltpu.roll(q, dh // 2, 1) * sin_signed).astype(BF16)
        kr = (k * cos + pltpu.roll(k, dh // 2, 1) * sin_signed) * scale
        scores = lax.dot_general(qr, kr.astype(BF16), (((1,), (1,)), ((), ())),
                                 preferred_element_type=F32) * dec_ref[h]
        intra = jnp.dot(scores.astype(BF16), v, preferred_element_type=F32)
        prev = state_ref[h]
        cross = jnp.dot(qr, prev.astype(BF16), preferred_element_type=F32) * qdec_ref[h]
        kd = (kr * kdec_ref[h]).astype(BF16)
        kv = lax.dot_general(kd, v, (((0,), (0,)), ((), ())), preferred_element_type=F32)
        state_ref[h] = prev * chunk_decay[h] + kv
        ret = intra + cross
        mu = jnp.mean(ret, axis=-1, keepdims=True)
        cen = ret - mu
        var = jnp.mean(cen * cen, axis=-1, keepdims=True)
        gn = cen * lax.rsqrt(var + EPS) * gnw_ref[:, sl] + gnb_ref[:, sl]
        g = g_ref[0, :, sl].astype(F32)
        o_ref[0, :, sl] = (g * _sigmoid(g) * gn).astype(o_ref.dtype)


def _retention_branch(proj3, pos3, gn_w, gn_b, mix_w):
    b, s, _ = proj3.shape
    heads, c = RET_HEADS, RET_CHUNK
    dh = mix_w // heads
    assert dh == LANES and s % c == 0
    log_g = np.log1p(-np.exp2(-5.0 - np.arange(heads, dtype=np.float64)))
    pos = np.arange(c, dtype=np.float64)
    diff = pos[:, None] - pos[None, :]
    intra_decay = np.where(diff >= 0, np.exp(np.maximum(diff, 0.0)[None] * log_g[:, None, None]), 0.0)
    k_decay = np.exp((c - 1 - pos)[None, :] * log_g[:, None])
    q_decay = np.exp((pos + 1)[None, :] * log_g[:, None])
    chunk_decay = tuple(float(x) for x in np.exp(c * log_g))
    dec = jnp.asarray(intra_decay, F32)
    kdec = jnp.asarray(np.broadcast_to(k_decay[:, :, None], (heads, c, dh)), F32)
    qdec = jnp.asarray(np.broadcast_to(q_decay[:, :, None], (heads, c, dh)), F32)
    inv_freq = RET_ROPE_BASE ** (-jnp.arange(0, dh, 2, dtype=F32) / dh)
    invf = jnp.concatenate([inv_freq, inv_freq]).reshape(1, dh)

    def col(cb):
        return pl.BlockSpec((1, c, mix_w), lambda bi, n, cb=cb: (bi, n, cb))

    const3 = pl.BlockSpec((heads, c, dh), lambda bi, n: (0, 0, 0))
    return pl.pallas_call(
        functools.partial(_ret_kernel, heads=heads, dh=dh, chunk_decay=chunk_decay),
        out_shape=jax.ShapeDtypeStruct((b, s, mix_w), BF16),
        grid=(b, s // c),
        in_specs=[pl.BlockSpec((1, c, 1), lambda bi, n: (bi, n, 0)),
                  pl.BlockSpec((1, dh), lambda bi, n: (0, 0)),
                  col(0), col(1), col(2), col(3),
                  const3, const3, const3,
                  pl.BlockSpec((1, mix_w), lambda bi, n: (0, 0)),
                  pl.BlockSpec((1, mix_w), lambda bi, n: (0, 0))],
        out_specs=pl.BlockSpec((1, c, mix_w), lambda bi, n: (bi, n, 0)),
        scratch_shapes=[pltpu.VMEM((heads, dh, dh), F32)],
        compiler_params=_cparams(("arbitrary", "arbitrary")),
        name="retention",
    )(pos3, invf, proj3, proj3, proj3, proj3, dec, kdec, qdec,
      gn_w.reshape(1, mix_w), gn_b.reshape(1, mix_w))


def _conv_kernel(b_ref, c_ref, u_ref, w_ref, o_ref, ubuf_ref):
    ts = o_ref.shape[1]
    halo = SUBLANES

    @pl.when(pl.program_id(1) == 0)
    def _():
        ubuf_ref[0:halo, :] = jnp.zeros((halo, ubuf_ref.shape[1]), F32)

    u = c_ref[0].astype(F32) * u_ref[0].astype(F32)
    ubuf_ref[halo:halo + ts, :] = u
    u1 = ubuf_ref[halo - 1:halo - 1 + ts, :]
    u2 = ubuf_ref[halo - 2:halo - 2 + ts, :]
    conv = w_ref[0:1, :] * u2 + w_ref[1:2, :] * u1 + w_ref[2:3, :] * u
    o_ref[0] = (b_ref[0].astype(F32) * conv).astype(o_ref.dtype)
    ubuf_ref[0:halo, :] = ubuf_ref[ts:ts + halo, :]


def _conv_branch(proj3, conv_w, mix_w):
    b, s, _ = proj3.shape
    ts = _tile(s, 512)

    def col(cb):
        return pl.BlockSpec((1, ts, mix_w), lambda bi, n, cb=cb: (bi, n, cb))

    return pl.pallas_call(
        _conv_kernel,
        out_shape=jax.ShapeDtypeStruct((b, s, mix_w), BF16),
        grid=(b, s // ts),
        in_specs=[col(4), col(5), col(6),
                  pl.BlockSpec((CONV_K, mix_w), lambda bi, n: (0, 0))],
        out_specs=pl.BlockSpec((1, ts, mix_w), lambda bi, n: (bi, n, 0)),
        scratch_shapes=[pltpu.VMEM((ts + SUBLANES, mix_w), F32)],
        compiler_params=_cparams(("arbitrary", "arbitrary")),
        name="short_conv",
    )(proj3, proj3, proj3, conv_w)


def _qknorm_kernel(q_ref, k_ref, qn_ref, kn_ref, qo_ref, ko_ref, *, heads, dh, q_scale):
    for h in range(heads):
        sl = slice(h * dh, (h + 1) * dh)
        q = q_ref[0, :, sl].astype(F32)
        k = k_ref[0, :, sl].astype(F32)
        qy = q * lax.rsqrt(jnp.mean(q * q, axis=-1, keepdims=True) + EPS) * qn_ref[...]
        ky = k * lax.rsqrt(jnp.mean(k * k, axis=-1, keepdims=True) + EPS) * kn_ref[...]
        qo_ref[0, :, sl] = (qy * q_scale).astype(qo_ref.dtype)
        ko_ref[0, :, sl] = ky.astype(ko_ref.dtype)


def _qk_norm(proj3, q_norm, k_norm, mix_w):
    b, s, _ = proj3.shape
    heads = ATT_HEADS
    dh = mix_w // heads
    ts = _tile(s, 512)
    out = jax.ShapeDtypeStruct((b, s, mix_w), BF16)
    blk = pl.BlockSpec((1, ts, mix_w), lambda bi, n: (bi, n, 0))
    return pl.pallas_call(
        functools.partial(_qknorm_kernel, heads=heads, dh=dh, q_scale=dh ** -0.5),
        out_shape=(out, out),
        grid=(b, s // ts),
        in_specs=[pl.BlockSpec((1, ts, mix_w), lambda bi, n: (bi, n, 7)),
                  pl.BlockSpec((1, ts, mix_w), lambda bi, n: (bi, n, 8)),
                  pl.BlockSpec((1, dh), lambda bi, n: (0, 0)),
                  pl.BlockSpec((1, dh), lambda bi, n: (0, 0))],
        out_specs=(blk, blk),
        compiler_params=_cparams(("parallel", "parallel")),
        name="qk_norm",
    )(proj3, proj3, q_norm.reshape(1, dh), k_norm.reshape(1, dh))


def _select_kernel(iq_ref, sm_ref, kit_ref, o_ref, qall_ref, keys_ref, j_ref, *,
                   tq, ck, ksel, heads, d_idx, idx_bits):
    qb = pl.program_id(1)
    row0 = qb * tq
    nch = (row0 + tq + ck - 1) // ck
    for h in range(heads):
        qall_ref[h * tq:(h + 1) * tq, :] = iq_ref[0, :, h * d_idx:(h + 1) * d_idx]
    w = sm_ref[0, :, d_idx:d_idx + heads].astype(F32) * (heads ** -0.5 * d_idx ** -0.5)
    t_col = row0 + lax.broadcasted_iota(I32, (tq, 1), 0)
    lane_ck = lax.broadcasted_iota(I32, (1, ck), 1)

    def score_chunk(c, carry):
        off = pl.multiple_of(c * ck, ck)
        kt = kit_ref[0, :, pl.ds(off, ck)]
        d = jnp.dot(qall_ref[...], kt, preferred_element_type=F32)
        sc = w[:, 0:1] * jnp.maximum(d[0:tq], 0.0)
        for h in range(1, heads):
            sc = sc + w[:, h:h + 1] * jnp.maximum(d[h * tq:(h + 1) * tq], 0.0)
        sc = sc + 0.0
        bits = pltpu.bitcast(sc, I32)
        key = bits ^ ((bits >> 31) & 0x7FFFFFFF)
        key = jnp.where(off + lane_ck <= t_col, key, INT_MIN)
        keys_ref[:, pl.ds(off, ck)] = key
        return carry

    lax.fori_loop(0, nch, score_chunk, 0)

    def count(pred):
        def body(c, acc):
            off = pl.multiple_of(c * ck, ck)
            m = jnp.where(pred(keys_ref[:, pl.ds(off, ck)], off + lane_ck), 1, 0)
            for q in range(ck // LANES):
                acc = acc + m[:, q * LANES:(q + 1) * LANES]
            return acc
        acc = lax.fori_loop(0, nch, body, jnp.zeros((tq, LANES), I32))
        return jnp.sum(acc.astype(F32), axis=1, keepdims=True)

    kf = float(ksel)
    thr = jnp.where(count(lambda k, col: k >= 0) >= kf, 0, INT_MIN)

    def bit_step(i, thr):
        cand = thr | jnp.left_shift(jnp.int32(1), 30 - i)
        return jnp.where(count(lambda k, col: k >= cand) >= kf, cand, thr)

    thr = lax.fori_loop(0, 31, bit_step, thr)
    cnt_gt = count(lambda k, col: k > thr)
    cnt_ge = count(lambda k, col: k >= thr)
    need = kf - cnt_gt
    excess = jnp.where(thr == INT_MIN, 0.0, cnt_ge - kf)
    j_ref[...] = jnp.full(j_ref.shape, 2 ** idx_bits, I32)

    @pl.when(jnp.max(excess) > 0.0)
    def _():
        def idx_step(i, jj):
            cand = jj | jnp.left_shift(jnp.int32(1), idx_bits - 1 - i)
            c_lt = count(lambda k, col: (k == thr) & (col < cand))
            return jnp.where(c_lt < need, cand, jj)
        jj = lax.fori_loop(0, idx_bits, idx_step, jnp.zeros((tq, 1), I32))
        j_ref[...] = jnp.broadcast_to(jj, j_ref.shape)

    jmax = j_ref[:, 0:1]
    o_ref[...] = jnp.full(o_ref.shape, NEG_MASK, o_ref.dtype)

    def write_chunk(c, carry):
        off = pl.multiple_of(c * ck, ck)
        k = keys_ref[:, pl.ds(off, ck)]
        col = off + lane_ck
        sel = ((k > thr) | ((k == thr) & (col <= jmax))) & (col <= t_col)
        o_ref[0, :, pl.ds(off, ck)] = jnp.where(sel, 0.0, NEG_MASK).astype(o_ref.dtype)
        return carry

    lax.fori_loop(0, nch, write_chunk, 0)


def _dsa_select(proj3, kit, s, ksel, iq_block):
    b = proj3.shape[0]
    tq = _tile(s, 128)
    ck = _tile(s, 512)
    heads, d_idx = IDX_HEADS, IDX_DH
    iq_w = heads * d_idx
    idx_bits = max(1, int(math.ceil(math.log2(s))))
    return pl.pallas_call(
        functools.partial(_select_kernel, tq=tq, ck=ck, ksel=ksel, heads=heads, d_idx=d_idx,
                          idx_bits=idx_bits),
        out_shape=jax.ShapeDtypeStruct((b, s, s), BF16),
        grid=(b, s // tq),
        in_specs=[pl.BlockSpec((1, tq, iq_w), lambda bi, n: (bi, n, iq_block)),
                  pl.BlockSpec((1, tq, iq_w), lambda bi, n: (bi, n, iq_block + 1)),
                  pl.BlockSpec((1, d_idx, s), lambda bi, n: (bi, 0, 0))],
        out_specs=pl.BlockSpec((1, tq, s), lambda bi, n: (bi, n, 0)),
        scratch_shapes=[pltpu.VMEM((heads * tq, d_idx), BF16),
                        pltpu.VMEM((tq, s), I32),
                        pltpu.VMEM((tq, LANES), I32)],
        compiler_params=_cparams(("parallel", "parallel")),
        name="dsa_select",
    )(proj3, proj3, kit)


def _t5_bucket(n):
    max_exact = REL_BUCKETS // 2
    nf = jnp.maximum(n, 1).astype(F32)
    large = max_exact + (jnp.log(nf / max_exact) / math.log(REL_MAX_DIST / max_exact)
                         * (REL_BUCKETS - max_exact)).astype(I32)
    large = jnp.minimum(large, REL_BUCKETS - 1)
    return jnp.where(n < max_exact, n, large)


def _attn_kernel(rb_ref, q_ref, k_ref, v_ref, mask_ref, pq_ref, pk_ref, o_ref,
                 acc_ref, m_ref, l_ref, bias_ref, *, tq, tk, rb_rows, heads, dh):
    qb = pl.program_id(1)
    kb = pl.program_id(2)
    kmax = ((qb + 1) * tq - 1) // tk
    far_bucket = REL_BUCKETS - 1

    @pl.when(kb == 0)
    def _():
        acc_ref[...] = jnp.zeros_like(acc_ref)
        m_ref[...] = jnp.full(m_ref.shape, -jnp.inf, F32)
        l_ref[...] = jnp.zeros_like(l_ref)

    def attend(bias_of):
        for h in range(heads):
            sl = slice(h * dh, (h + 1) * dh)
            kh = k_ref[0, :, sl]
            vh = v_ref[0, :, sl]

            def rows(r, carry):
                r0 = pl.multiple_of(r * rb_rows, rb_rows)
                qh = q_ref[0, pl.ds(r0, rb_rows), sl]
                s = lax.dot_general(qh, kh, (((1,), (1,)), ((), ())),
                                    preferred_element_type=F32) + bias_of(h, r0)
                m_old = m_ref[h, pl.ds(r0, rb_rows), :]
                m_new = jnp.maximum(m_old, jnp.max(s, axis=1, keepdims=True))
                alpha = jnp.exp(m_old - m_new)
                p = jnp.exp(s - m_new)
                l_ref[h, pl.ds(r0, rb_rows), :] = alpha * l_ref[h, pl.ds(r0, rb_rows), :] + \
                    jnp.sum(p, axis=1, keepdims=True)
                m_ref[h, pl.ds(r0, rb_rows), :] = m_new
                acc_ref[pl.ds(r0, rb_rows), sl] = alpha * acc_ref[pl.ds(r0, rb_rows), sl] + \
                    jnp.dot(p.astype(BF16), vh, preferred_element_type=F32)
                return carry

            lax.fori_loop(0, tq // rb_rows, rows, 0)

    @pl.when(kb <= kmax)
    def _():
        pq = pq_ref[0]
        pk = pk_ref[0]
        all_far = (jnp.min(pq) - jnp.max(pk)) >= REL_MAX_DIST

        @pl.when(all_far)
        def _():
            attend(lambda h, r0: mask_ref[0, pl.ds(r0, rb_rows), :].astype(F32) + rb_ref[far_bucket, h])

        @pl.when(jnp.logical_not(all_far))
        def _():
            n_lane = lax.broadcasted_iota(I32, (1, REL_MAX_DIST), 1)
            bucket = _t5_bucket(n_lane)
            dist = jnp.clip(pq - pk, 0, REL_MAX_DIST - 1)
            maskf = mask_ref[0].astype(F32)
            for h in range(heads):
                tab = jnp.zeros((1, REL_MAX_DIST), F32)
                for j in range(REL_BUCKETS):
                    tab = jnp.where(bucket == j, rb_ref[j, h], tab)
                tab = jnp.broadcast_to(tab, (tq, REL_MAX_DIST))
                for c in range(tk // LANES):
                    cs = slice(c * LANES, (c + 1) * LANES)
                    bias_ref[h, :, cs] = maskf[:, cs] + jnp.take_along_axis(tab, dist[:, cs], axis=1)
            attend(lambda h, r0: bias_ref[h, pl.ds(r0, rb_rows), :])

    @pl.when(kb == kmax)
    def _():
        for h in range(heads):
            sl = slice(h * dh, (h + 1) * dh)
            o_ref[0, :, sl] = (acc_ref[:, sl] / l_ref[h]).astype(o_ref.dtype)


def _dsa_attention(qn, kn, proj3, mask, pos_q, pos_k, rel_bias, mix_w):
    b, s, _ = qn.shape
    heads = ATT_HEADS
    dh = mix_w // heads
    tq = _tile(s, 512)
    tk = tq
    rb_rows = _tile(tq, 128)
    assert REL_MAX_DIST == LANES

    def kidx(qb, kb):
        return jnp.minimum(kb, ((qb + 1) * tq - 1) // tk)

    return pl.pallas_call(
        functools.partial(_attn_kernel, tq=tq, tk=tk, rb_rows=rb_rows, heads=heads, dh=dh),
        out_shape=jax.ShapeDtypeStruct((b, s, mix_w), BF16),
        grid=(b, s // tq, s // tk),
        in_specs=[pl.BlockSpec(memory_space=pltpu.SMEM),
                  pl.BlockSpec((1, tq, mix_w), lambda bi, qb, kb: (bi, qb, 0)),
                  pl.BlockSpec((1, tk, mix_w), lambda bi, qb, kb: (bi, kidx(qb, kb), 0)),
                  pl.BlockSpec((1, tk, mix_w), lambda bi, qb, kb: (bi, kidx(qb, kb), 9)),
                  pl.BlockSpec((1, tq, tk), lambda bi, qb, kb: (bi, qb, kidx(qb, kb))),
                  pl.BlockSpec((1, tq, 1), lambda bi, qb, kb: (bi, qb, 0)),
                  pl.BlockSpec((1, 1, tk), lambda bi, qb, kb: (bi, 0, kidx(qb, kb)))],
        out_specs=pl.BlockSpec((1, tq, mix_w), lambda bi, qb, kb: (bi, qb, 0)),
        scratch_shapes=[pltpu.VMEM((tq, mix_w), F32),
                        pltpu.VMEM((heads, tq, 1), F32),
                        pltpu.VMEM((heads, tq, 1), F32),
                        pltpu.VMEM((heads, tq, tk), F32)],
        compiler_params=_cparams(("parallel", "parallel", "arbitrary")),
        name="dsa_attention",
    )(rel_bias, qn, kn, proj3, mask, pos_q, pos_k)


def _merge_kernel(y0_ref, y1_ref, y2_ref, w_ref, g0_ref, g1_ref, g2_ref, o_ref):
    acc = None
    for n, (y_ref, g_ref) in enumerate(((y0_ref, g0_ref), (y1_ref, g1_ref), (y2_ref, g2_ref))):
        br = jnp.dot(y_ref[...], w_ref[n], preferred_element_type=F32)
        term = _sigmoid(g_ref[...].astype(F32)) * br
        acc = term if acc is None else acc + term
    o_ref[...] = acc.astype(o_ref.dtype)


def _branch_merge(ys, w_br, proj2, d_model, mix_w, gate_col0):
    t = proj2.shape[0]
    tm = _tile(t, 1024)
    tn = _tile(d_model, 512)
    yspec = pl.BlockSpec((tm, mix_w), lambda i, j: (i, 0))

    def gspec(n):
        base = (gate_col0 + n * d_model) // tn
        return pl.BlockSpec((tm, tn), lambda i, j, base=base: (i, base + j))

    return pl.pallas_call(
        _merge_kernel,
        out_shape=jax.ShapeDtypeStruct((t, d_model), BF16),
        grid=(t // tm, d_model // tn),
        in_specs=[yspec, yspec, yspec,
                  pl.BlockSpec((N_BRANCH, mix_w, tn), lambda i, j: (0, 0, j)),
                  gspec(0), gspec(1), gspec(2)],
        out_specs=pl.BlockSpec((tm, tn), lambda i, j: (i, j)),
        compiler_params=_cparams(("parallel", "parallel")),
        name="branch_merge",
    )(ys[0], ys[1], ys[2], w_br, proj2, proj2, proj2)


def _glu_kernel(a_ref, w1_ref, w3_ref, o_ref):
    a = a_ref[...]
    h1 = jnp.dot(a, w1_ref[...], preferred_element_type=F32)
    h3 = jnp.dot(a, w3_ref[...], preferred_element_type=F32)
    o_ref[...] = (h1 * _sigmoid(h1) * h3).astype(o_ref.dtype)


def _glu(a, w1, w3):
    t, d = a.shape
    ff = w1.shape[1]
    tm = _tile(t, 1024)
    tn = _tile(ff, 512)
    return pl.pallas_call(
        _glu_kernel,
        out_shape=jax.ShapeDtypeStruct((t, ff), BF16),
        grid=(t // tm, ff // tn),
        in_specs=[pl.BlockSpec((tm, d), lambda i, j: (i, 0)),
                  pl.BlockSpec((d, tn), lambda i, j: (0, j)),
                  pl.BlockSpec((d, tn), lambda i, j: (0, j))],
        out_specs=pl.BlockSpec((tm, tn), lambda i, j: (i, j)),
        compiler_params=_cparams(("parallel", "parallel")),
        name="swiglu_up",
    )(a, w1, w3)


def _router_kernel(x_ref, g_ref, r_ref, comb_ref, sel_ref, *, n_exp):
    x = x_ref[...]
    h = (x * lax.rsqrt(jnp.mean(x * x, axis=-1, keepdims=True) + EPS) * g_ref[...]).astype(BF16)
    logits = jnp.dot(h, r_ref[...], preferred_element_type=F32)
    lane = lax.broadcasted_iota(I32, logits.shape, 1)
    lg = jnp.where(lane < n_exp, logits, -jnp.inf)
    v1 = jnp.max(lg, axis=1, keepdims=True)
    i1 = jnp.min(jnp.where(lg == v1, lane, LANES), axis=1, keepdims=True)
    lg2 = jnp.where(lane == i1, -jnp.inf, lg)
    v2 = jnp.max(lg2, axis=1, keepdims=True)
    i2 = jnp.min(jnp.where(lg2 == v2, lane, LANES), axis=1, keepdims=True)
    e = jnp.exp(v2 - v1)
    w1 = 1.0 / (1.0 + e)
    w2 = e / (1.0 + e)
    comb_ref[...] = jnp.where(lane == i1, w1, jnp.where(lane == i2, w2, 0.0))
    sel_ref[...] = jnp.where((lane == i1) | (lane == i2), 1, 0)


def _router(x2, g, router_w):
    t, d = x2.shape
    n_exp = router_w.shape[1]
    tm = _tile(t, 512)
    rpad = jnp.zeros((d, LANES), BF16).at[:, :n_exp].set(router_w.astype(BF16))
    comb, sel = pl.pallas_call(
        functools.partial(_router_kernel, n_exp=n_exp),
        out_shape=(jax.ShapeDtypeStruct((t, LANES), F32), jax.ShapeDtypeStruct((t, LANES), I32)),
        grid=(t // tm,),
        in_specs=[pl.BlockSpec((tm, d), lambda i: (i, 0)),
                  pl.BlockSpec((1, d), lambda i: (0, 0)),
                  pl.BlockSpec((d, LANES), lambda i: (0, 0))],
        out_specs=(pl.BlockSpec((tm, LANES), lambda i: (i, 0)),
                   pl.BlockSpec((tm, LANES), lambda i: (i, 0))),
        compiler_params=_cparams(("parallel",)),
        name="moe_router",
    )(x2, g.reshape(1, d), rpad)
    return comb[:, :n_exp], sel[:, :n_exp]


def _dispatch_kernel(src_ref, x_hbm, g_ref, o_ref, buf_ref, sem, *, rows):
    def row_copy(r):
        return pltpu.make_async_copy(x_hbm.at[pl.ds(src_ref[0, 0, r], 1)], buf_ref.at[pl.ds(r, 1)], sem)

    def start(r, carry):
        row_copy(r).start()
        return carry

    def wait(r, carry):
        row_copy(r).wait()
        return carry

    lax.fori_loop(0, rows, start, 0)
    lax.fori_loop(0, rows, wait, 0)
    x = buf_ref[...]
    o_ref[...] = (x * lax.rsqrt(jnp.mean(x * x, axis=-1, keepdims=True) + EPS) * g_ref[...]).astype(o_ref.dtype)


def _dispatch(x2, g, src_tok, rows):
    t, d = x2.shape
    p = src_tok.shape[0]
    assert p % rows == 0
    src3 = src_tok.reshape(p // rows, 1, rows)
    return pl.pallas_call(
        functools.partial(_dispatch_kernel, rows=rows),
        out_shape=jax.ShapeDtypeStruct((p, d), BF16),
        grid=(p // rows,),
        in_specs=[pl.BlockSpec((1, 1, rows), lambda i: (i, 0, 0), memory_space=pltpu.SMEM),
                  pl.BlockSpec(memory_space=pl.ANY),
                  pl.BlockSpec((1, d), lambda i: (0, 0))],
        out_specs=pl.BlockSpec((rows, d), lambda i: (i, 0)),
        scratch_shapes=[pltpu.VMEM((rows, d), F32), pltpu.SemaphoreType.DMA(())],
        compiler_params=_cparams(("arbitrary",)),
        name="moe_dispatch",
    )(src3, x2, g.reshape(1, d))


def _moe_up_kernel(te_ref, tv_ref, a_ref, w1_ref, w3_ref, o_ref):
    i = pl.program_id(0)

    @pl.when(tv_ref[i] == 1)
    def _():
        a = a_ref[...]
        h1 = jnp.dot(a, w1_ref[0], preferred_element_type=F32)
        h3 = jnp.dot(a, w3_ref[0], preferred_element_type=F32)
        o_ref[...] = (h1 * _sigmoid(h1) * h3).astype(o_ref.dtype)

    @pl.when(tv_ref[i] == 0)
    def _():
        o_ref[...] = jnp.zeros_like(o_ref)


def _moe_up(hs, w1, w3, tile_expert, tile_valid, tm):
    p, d = hs.shape
    ff = w1.shape[2]
    tn = _tile(ff, 512)
    nj = ff // tn

    def wmap(i, j, te, tv):
        return (te[i], 0, jnp.where(tv[i] == 1, j, nj - 1))

    return pl.pallas_call(
        _moe_up_kernel,
        out_shape=jax.ShapeDtypeStruct((p, ff), BF16),
        grid_spec=pltpu.PrefetchScalarGridSpec(
            num_scalar_prefetch=2,
            grid=(p // tm, nj),
            in_specs=[pl.BlockSpec((tm, d), lambda i, j, te, tv: (i, 0)),
                      pl.BlockSpec((1, d, tn), wmap),
                      pl.BlockSpec((1, d, tn), wmap)],
            out_specs=pl.BlockSpec((tm, tn), lambda i, j, te, tv: (i, j))),
        compiler_params=_cparams(("arbitrary", "arbitrary")),
        name="moe_up",
    )(tile_expert, tile_valid, hs, w1, w3)


def _moe_down_kernel(te_ref, tv_ref, a_ref, w_ref, o_ref):
    i = pl.program_id(0)

    @pl.when(tv_ref[i] == 1)
    def _():
        o_ref[...] = jnp.dot(a_ref[...], w_ref[0], preferred_element_type=F32)

    @pl.when(tv_ref[i] == 0)
    def _():
        o_ref[...] = jnp.zeros_like(o_ref)


def _moe_down(us, w2, tile_expert, tile_valid, tm):
    p, ff = us.shape
    d = w2.shape[2]
    tn = _tile(d, 512)
    nj = d // tn

    def wmap(i, j, te, tv):
        return (te[i], 0, jnp.where(tv[i] == 1, j, nj - 1))

    return pl.pallas_call(
        _moe_down_kernel,
        out_shape=jax.ShapeDtypeStruct((p, d), F32),
        grid_spec=pltpu.PrefetchScalarGridSpec(
            num_scalar_prefetch=2,
            grid=(p // tm, nj),
            in_specs=[pl.BlockSpec((tm, ff), lambda i, j, te, tv: (i, 0)),
                      pl.BlockSpec((1, ff, tn), wmap)],
            out_specs=pl.BlockSpec((tm, tn), lambda i, j, te, tv: (i, j))),
        compiler_params=_cparams(("arbitrary", "arbitrary")),
        name="moe_down",
    )(tile_expert, tile_valid, us, w2)


def _combine_kernel(p0_ref, p1_ref, x_ref, w0_ref, w1_ref, ys_hbm, o_ref, a_ref, b_ref, sem, *, rows):
    def copies(r):
        return (pltpu.make_async_copy(ys_hbm.at[pl.ds(p0_ref[0, 0, r], 1)], a_ref.at[pl.ds(r, 1)], sem.at[0]),
                pltpu.make_async_copy(ys_hbm.at[pl.ds(p1_ref[0, 0, r], 1)], b_ref.at[pl.ds(r, 1)], sem.at[1]))

    def start(r, carry):
        ca, cb = copies(r)
        ca.start()
        cb.start()
        return carry

    def wait(r, carry):
        ca, cb = copies(r)
        ca.wait()
        cb.wait()
        return carry

    lax.fori_loop(0, rows, start, 0)
    lax.fori_loop(0, rows, wait, 0)
    o_ref[...] = x_ref[...] + (w0_ref[...] * a_ref[...] + w1_ref[...] * b_ref[...])


def _combine(x2, ys, pos0, pos1, w0, w1, rows):
    t, d = x2.shape
    assert t % rows == 0
    nblk = t // rows
    sm = pl.BlockSpec((1, 1, rows), lambda i: (i, 0, 0), memory_space=pltpu.SMEM)
    return pl.pallas_call(
        functools.partial(_combine_kernel, rows=rows),
        out_shape=jax.ShapeDtypeStruct((t, d), F32),
        grid=(nblk,),
        in_specs=[sm, sm,
                  pl.BlockSpec((rows, d), lambda i: (i, 0)),
                  pl.BlockSpec((rows, 1), lambda i: (i, 0)),
                  pl.BlockSpec((rows, 1), lambda i: (i, 0)),
                  pl.BlockSpec(memory_space=pl.ANY)],
        out_specs=pl.BlockSpec((rows, d), lambda i: (i, 0)),
        scratch_shapes=[pltpu.VMEM((rows, d), F32), pltpu.VMEM((rows, d), F32),
                        pltpu.SemaphoreType.DMA((2,))],
        compiler_params=_cparams(("arbitrary",)),
        name="moe_combine",
    )(pos0.reshape(nblk, 1, rows), pos1.reshape(nblk, 1, rows), x2, w0, w1, ys)


def _moe(x2, g, router_w, w1, w3, w2):
    t, d = x2.shape
    n_exp = router_w.shape[1]
    tm = _tile(t, 512)
    comb, sel = _router(x2, g, router_w)
    rank = jnp.cumsum(sel, axis=0) - sel
    counts = jnp.sum(sel, axis=0)
    gsize = ((counts + tm - 1) // tm) * tm
    gend = jnp.cumsum(gsize)
    goff = gend - gsize
    dest = goff[None, :] + rank
    e0 = jnp.argmax(sel, axis=1)
    e1 = (n_exp - 1) - jnp.argmax(sel[:, ::-1], axis=1)
    pos0 = jnp.take_along_axis(dest, e0[:, None], axis=1)[:, 0].astype(I32)
    pos1 = jnp.take_along_axis(dest, e1[:, None], axis=1)[:, 0].astype(I32)
    wt0 = jnp.take_along_axis(comb, e0[:, None], axis=1)
    wt1 = jnp.take_along_axis(comb, e1[:, None], axis=1)
    p = t * TOP_K + n_exp * tm
    tok = jnp.arange(t, dtype=I32)
    src_tok = jnp.zeros((p,), I32).at[jnp.concatenate([pos0, pos1])].set(jnp.concatenate([tok, tok]))
    n_tiles = p // tm
    tstart = jnp.arange(n_tiles, dtype=I32) * tm
    tile_valid = (tstart < gend[-1]).astype(I32)
    te_raw = jnp.minimum(jnp.sum((tstart[:, None] >= gend[None, :]).astype(I32), axis=1), n_exp - 1)
    te_last = te_raw[jnp.maximum(gend[-1] // tm - 1, 0)]
    tile_expert = jnp.where(tile_valid == 1, te_raw, te_last).astype(I32)

    hs = _dispatch(x2, g, src_tok, rows=_tile(p, 256))
    us = _moe_up(hs, w1, w3, tile_expert, tile_valid, tm)
    ys = _moe_down(us, w2, tile_expert, tile_valid, tm)
    return _combine(x2, ys, pos0, pos1, wt0, wt1, rows=_tile(t, 256))


def _pack_w_in(w_in_l, d_model, mix_w):
    iq_w = IDX_HEADS * IDX_DH
    n_main = 10 * mix_w
    small = IDX_DH + IDX_HEADS
    gates0 = n_main + iq_w + small
    pad = iq_w - small
    w = w_in_l.astype(BF16)
    return jnp.concatenate([w[:, :n_main], w[:, gates0:gates0 + N_BRANCH * d_model],
                            w[:, n_main:n_main + iq_w], w[:, n_main + iq_w:gates0],
                            jnp.zeros((d_model, pad), BF16)], axis=1)


def _mixer(x2, b, s, positions, rel_bias, norm_g, w_in_l, w_br_l, w_o_l, gn_w, gn_b, conv_w, q_norm, k_norm):
    t, d_model = x2.shape
    mix_w = d_model // 2
    iq_w = IDX_HEADS * IDX_DH
    gate_col0 = 10 * mix_w
    iq_col0 = gate_col0 + N_BRANCH * d_model
    assert iq_col0 % iq_w == 0

    h = _rmsnorm(x2, norm_g)
    proj2 = _matmul(h, _pack_w_in(w_in_l, d_model, mix_w), BF16, 1024, 512, name="in_proj")
    proj3 = proj2.reshape(b, s, proj2.shape[1])
    pos_q = positions.reshape(b, s, 1)
    pos_k = positions.reshape(b, 1, s)

    y_ret = _retention_branch(proj3, pos_q, gn_w, gn_b, mix_w)
    y_conv = _conv_branch(proj3, conv_w, mix_w)

    ik0 = iq_col0 + iq_w
    kit = jnp.swapaxes(proj3[:, :, ik0:ik0 + IDX_DH], 1, 2)
    mask = _dsa_select(proj3, kit, s, min(TOPK_MAX, s // 4), iq_col0 // iq_w)
    qn, kn = _qk_norm(proj3, q_norm, k_norm, mix_w)
    y_att = _dsa_attention(qn, kn, proj3, mask, pos_q, pos_k, rel_bias, mix_w)

    ys = [y.reshape(t, mix_w) for y in (y_ret, y_conv, y_att)]
    merged = _branch_merge(ys, w_br_l.astype(BF16), proj2, d_model, mix_w, gate_col0)
    return _matmul(merged, w_o_l.astype(BF16), F32, 1024, 512, res=x2, name="out_proj")


def kernel(x, positions, rel_bias, norm_mix, norm_ffn, w_in, w_br, w_o, ret_gn_w, ret_gn_b, conv_w,
           q_norm, k_norm, ffn_w1, ffn_w3, ffn_w2, moe_router, moe_w1, moe_w3, moe_w2):
    b, s, d_model = x.shape
    depth = w_in.shape[0]
    x2 = x.reshape(b * s, d_model)
    for l in range(depth):
        x2 = _mixer(x2, b, s, positions, rel_bias, norm_mix[l], w_in[l], w_br[l], w_o[l],
                    ret_gn_w[l], ret_gn_b[l], conv_w[l], q_norm[l], k_norm[l])
        if l % 2 == 0:
            i = l // 2
            h = _rmsnorm(x2, norm_ffn[l])
            u = _glu(h, ffn_w1[i].astype(BF16), ffn_w3[i].astype(BF16))
            x2 = _matmul(u, ffn_w2[i].astype(BF16), F32, 512, 512, res=x2, name="ffn_down")
        else:
            i = l // 2
            x2 = _moe(x2, norm_ffn[l], moe_router[i], moe_w1[i].astype(BF16), moe_w3[i].astype(BF16),
                      moe_w2[i].astype(BF16))
    return x2.reshape(b, s, d_model)
```

```python
import functools
import math

import numpy as np
import jax
import jax.numpy as jnp
from jax import lax
from jax.experimental import pallas as pl
from jax.experimental.pallas import tpu as pltpu

F32 = jnp.float32
BF16 = jnp.bfloat16
I32 = jnp.int32
I16 = jnp.int16

EPS = 1e-6
N_BRANCH = 3
RET_HEADS = 8
RET_CHUNK = 128
RET_ROPE_BASE = 10000.0
CONV_K = 3
ATT_HEADS = 8
IDX_HEADS = 8
IDX_DH = 64
TOPK_MAX = 256
REL_BUCKETS = 32
REL_MAX_DIST = 128
TOP_K = 2

LANES = 128
SUBLANES = 8
VMEM_LIMIT_BYTES = 56 * 1024 * 1024
ROW_DMA_UNROLL = 8

NEG_MASK = -1e30
LOG2E = math.log2(math.e)
INT_MIN = -2 ** 31
I16_MIN = -2 ** 15


def _cparams(sem):
    return pltpu.CompilerParams(dimension_semantics=sem, vmem_limit_bytes=VMEM_LIMIT_BYTES)


def _sigmoid(x):
    return 1.0 / (1.0 + jnp.exp(-x))


def _tile(n, pref):
    t = min(n, pref)
    assert n % t == 0, (n, pref)
    return t


def _rmsnorm_kernel(x_ref, g_ref, o_ref):
    x = x_ref[...]
    ms = jnp.mean(x * x, axis=-1, keepdims=True)
    o_ref[...] = (x * lax.rsqrt(ms + EPS) * g_ref[...]).astype(o_ref.dtype)


def _rmsnorm(x2, g):
    t, d = x2.shape
    tm = _tile(t, 512)
    return pl.pallas_call(
        _rmsnorm_kernel,
        out_shape=jax.ShapeDtypeStruct((t, d), BF16),
        grid=(t // tm,),
        in_specs=[pl.BlockSpec((tm, d), lambda i: (i, 0)),
                  pl.BlockSpec((1, d), lambda i: (0, 0))],
        out_specs=pl.BlockSpec((tm, d), lambda i: (i, 0)),
        compiler_params=_cparams(("parallel",)),
        name="rmsnorm",
    )(x2, g.reshape(1, d))


def _mm_kernel(a_ref, b_ref, o_ref):
    o_ref[...] = jnp.dot(a_ref[...], b_ref[...], preferred_element_type=F32).astype(o_ref.dtype)


def _mm_res_kernel(a_ref, b_ref, r_ref, o_ref):
    o_ref[...] = (r_ref[...] + jnp.dot(a_ref[...], b_ref[...], preferred_element_type=F32)).astype(o_ref.dtype)


def _matmul(a, b, out_dtype, tm, tn, res=None, name="matmul"):
    m, k = a.shape
    k2, n = b.shape
    assert k == k2
    tm = _tile(m, tm)
    tn = _tile(n, tn)
    in_specs = [pl.BlockSpec((tm, k), lambda i, j: (i, 0)),
                pl.BlockSpec((k, tn), lambda i, j: (0, j))]
    args = [a, b]
    kern = _mm_kernel
    if res is not None:
        in_specs.append(pl.BlockSpec((tm, tn), lambda i, j: (i, j)))
        args.append(res)
        kern = _mm_res_kernel
    return pl.pallas_call(
        kern,
        out_shape=jax.ShapeDtypeStruct((m, n), out_dtype),
        grid=(m // tm, n // tn),
        in_specs=in_specs,
        out_specs=pl.BlockSpec((tm, tn), lambda i, j: (i, j)),
        compiler_params=_cparams(("parallel", "parallel")),
        name=name,
    )(*args)


def _ret_kernel(pos_ref, invf_ref, q_ref, k_ref, v_ref, g_ref, dec_ref, kdec_ref, qdec_ref,
                gnw_ref, gnb_ref, o_ref, state_ref, *, heads, dh, chunk_decay):
    @pl.when(pl.program_id(1) == 0)
    def _():
        state_ref[...] = jnp.zeros_like(state_ref)

    c = q_ref.shape[1]
    ang = pos_ref[0].astype(F32) * invf_ref[...]
    cos = jnp.cos(ang)
    sin = jnp.sin(ang)
    lane = lax.broadcasted_iota(I32, (c, dh), 1)
    sin_signed = jnp.where(lane < dh // 2, -sin, sin)
    scale = dh ** -0.5
    for h in range(heads):
        sl = slice(h * dh, (h + 1) * dh)
        q = q_ref[0, :, sl].astype(F32)
        k = k_ref[0, :, sl].astype(F32)
        v = v_ref[0, :, sl]
        qr = (q * cos + pltpu.roll(q, dh // 2, 1) * sin_signed).astype(BF16)
        kr = (k * cos + pltpu.roll(k, dh // 2, 1) * sin_signed) * scale
        scores = lax.dot_general(qr, kr.astype(BF16), (((1,), (1,)), ((), ())),
                                 preferred_element_type=F32) * dec_ref[h]
        intra = jnp.dot(scores.astype(BF16), v, preferred_element_type=F32)
        prev = state_ref[h]
        cross = jnp.dot(qr, prev.astype(BF16), preferred_element_type=F32) * qdec_ref[h]
        kd = (kr * kdec_ref[h]).astype(BF16)
        kv = lax.dot_general(kd, v, (((0,), (0,)), ((), ())), preferred_element_type=F32)
        state_ref[h] = prev * chunk_decay[h] + kv
        ret = intra + cross
        mu = jnp.mean(ret, axis=-1, keepdims=True)
        cen = ret - mu
        var = jnp.mean(cen * cen, axis=-1, keepdims=True)
        gn = cen * lax.rsqrt(var + EPS) * gnw_ref[:, sl] + gnb_ref[:, sl]
        g = g_ref[0, :, sl].astype(F32)
        o_ref[0, :, sl] = (g * _sigmoid(g) * gn).astype(o_ref.dtype)


def _retention_branch(proj3, pos3, gn_w, gn_b, mix_w):
    b, s, _ = proj3.shape
    heads, c = RET_HEADS, RET_CHUNK
    dh = mix_w // heads
    assert dh == LANES and s % c == 0
    log_g = np.log1p(-np.exp2(-5.0 - np.arange(heads, dtype=np.float64)))
    pos = np.arange(c, dtype=np.float64)
    diff = pos[:, None] - pos[None, :]
    intra_decay = np.where(diff >= 0, np.exp(np.maximum(diff, 0.0)[None] * log_g[:, None, None]), 0.0)
    k_decay = np.exp((c - 1 - pos)[None, :] * log_g[:, None])
    q_decay = np.exp((pos + 1)[None, :] * log_g[:, None])
    chunk_decay = tuple(float(x) for x in np.exp(c * log_g))
    dec = jnp.asarray(intra_decay, F32)
    kdec = jnp.asarray(np.broadcast_to(k_decay[:, :, None], (heads, c, dh)), F32)
    qdec = jnp.asarray(np.broadcast_to(q_decay[:, :, None], (heads, c, dh)), F32)
    inv_freq = RET_ROPE_BASE ** (-jnp.arange(0, dh, 2, dtype=F32) / dh)
    invf = jnp.concatenate([inv_freq, inv_freq]).reshape(1, dh)

    def col(cb):
        return pl.BlockSpec((1, c, mix_w), lambda bi, n, cb=cb: (bi, n, cb))

    const3 = pl.BlockSpec((heads, c, dh), lambda bi, n: (0, 0, 0))
    return pl.pallas_call(
        functools.partial(_ret_kernel, heads=heads, dh=dh, chunk_decay=chunk_decay),
        out_shape=jax.ShapeDtypeStruct((b, s, mix_w), BF16),
        grid=(b, s // c),
        in_specs=[pl.BlockSpec((1, c, 1), lambda bi, n: (bi, n, 0)),
                  pl.BlockSpec((1, dh), lambda bi, n: (0, 0)),
                  col(0), col(1), col(2), col(3),
                  const3, const3, const3,
                  pl.BlockSpec((1, mix_w), lambda bi, n: (0, 0)),
                  pl.BlockSpec((1, mix_w), lambda bi, n: (0, 0))],
        out_specs=pl.BlockSpec((1, c, mix_w), lambda bi, n: (bi, n, 0)),
        scratch_shapes=[pltpu.VMEM((heads, dh, dh), F32)],
        compiler_params=_cparams(("arbitrary", "arbitrary")),
        name="retention",
    )(pos3, invf, proj3, proj3, proj3, proj3, dec, kdec, qdec,
      gn_w.reshape(1, mix_w), gn_b.reshape(1, mix_w))


def _conv_kernel(b_ref, c_ref, u_ref, w_ref, o_ref, ubuf_ref):
    ts = o_ref.shape[1]
    halo = SUBLANES

    @pl.when(pl.program_id(1) == 0)
    def _():
        ubuf_ref[0:halo, :] = jnp.zeros((halo, ubuf_ref.shape[1]), F32)

    u = c_ref[0].astype(F32) * u_ref[0].astype(F32)
    ubuf_ref[halo:halo + ts, :] = u
    u1 = ubuf_ref[halo - 1:halo - 1 + ts, :]
    u2 = ubuf_ref[halo - 2:halo - 2 + ts, :]
    conv = w_ref[0:1, :] * u2 + w_ref[1:2, :] * u1 + w_ref[2:3, :] * u
    o_ref[0] = (b_ref[0].astype(F32) * conv).astype(o_ref.dtype)
    ubuf_ref[0:halo, :] = ubuf_ref[ts:ts + halo, :]


def _conv_branch(proj3, conv_w, mix_w):
    b, s, _ = proj3.shape
    ts = _tile(s, 512)

    def col(cb):
        return pl.BlockSpec((1, ts, mix_w), lambda bi, n, cb=cb: (bi, n, cb))

    return pl.pallas_call(
        _conv_kernel,
        out_shape=jax.ShapeDtypeStruct((b, s, mix_w), BF16),
        grid=(b, s // ts),
        in_specs=[col(4), col(5), col(6),
                  pl.BlockSpec((CONV_K, mix_w), lambda bi, n: (0, 0))],
        out_specs=pl.BlockSpec((1, ts, mix_w), lambda bi, n: (bi, n, 0)),
        scratch_shapes=[pltpu.VMEM((ts + SUBLANES, mix_w), F32)],
        compiler_params=_cparams(("arbitrary", "arbitrary")),
        name="short_conv",
    )(proj3, proj3, proj3, conv_w)


def _qknorm_kernel(q_ref, k_ref, qn_ref, kn_ref, qo_ref, ko_ref, *, heads, dh, q_scale):
    for h in range(heads):
        sl = slice(h * dh, (h + 1) * dh)
        q = q_ref[0, :, sl].astype(F32)
        k = k_ref[0, :, sl].astype(F32)
        qy = q * lax.rsqrt(jnp.mean(q * q, axis=-1, keepdims=True) + EPS) * qn_ref[...]
        ky = k * lax.rsqrt(jnp.mean(k * k, axis=-1, keepdims=True) + EPS) * kn_ref[...]
        qo_ref[0, :, sl] = (qy * q_scale).astype(qo_ref.dtype)
        ko_ref[0, :, sl] = ky.astype(ko_ref.dtype)


def _qk_norm(proj3, q_norm, k_norm, mix_w):
    b, s, _ = proj3.shape
    heads = ATT_HEADS
    dh = mix_w // heads
    ts = _tile(s, 512)
    out = jax.ShapeDtypeStruct((b, s, mix_w), BF16)
    blk = pl.BlockSpec((1, ts, mix_w), lambda bi, n: (bi, n, 0))
    return pl.pallas_call(
        functools.partial(_qknorm_kernel, heads=heads, dh=dh, q_scale=dh ** -0.5 * LOG2E),
        out_shape=(out, out),
        grid=(b, s // ts),
        in_specs=[pl.BlockSpec((1, ts, mix_w), lambda bi, n: (bi, n, 7)),
                  pl.BlockSpec((1, ts, mix_w), lambda bi, n: (bi, n, 8)),
                  pl.BlockSpec((1, dh), lambda bi, n: (0, 0)),
                  pl.BlockSpec((1, dh), lambda bi, n: (0, 0))],
        out_specs=(blk, blk),
        compiler_params=_cparams(("parallel", "parallel")),
        name="qk_norm",
    )(proj3, proj3, q_norm.reshape(1, dh), k_norm.reshape(1, dh))


def _select_kernel(iq_ref, sm_ref, kit_ref, o_ref, qall_ref, khi_ref, klo_ref, j_ref, *,
                   tq, ck, ksel, heads, d_idx, idx_bits):
    qb = pl.program_id(1)
    row0 = qb * tq
    nch = (row0 + tq + ck - 1) // ck
    for h in range(heads):
        qall_ref[h * tq:(h + 1) * tq, :] = iq_ref[0, :, h * d_idx:(h + 1) * d_idx]
    w = sm_ref[0, :, d_idx:d_idx + heads].astype(F32) * (heads ** -0.5 * d_idx ** -0.5)
    t_col = row0 + lax.broadcasted_iota(I32, (tq, 1), 0)
    lane_ck = lax.broadcasted_iota(I32, (1, ck), 1)
    t16 = t_col.astype(I16)

    def score_chunk(c, carry):
        off = pl.multiple_of(c * ck, ck)
        kt = kit_ref[0, :, pl.ds(off, ck)]
        sc = jnp.zeros((tq, ck), F32)
        for h in range(heads):
            d = jnp.dot(qall_ref[h * tq:(h + 1) * tq, :], kt, preferred_element_type=F32)
            sc = sc + w[:, h:h + 1] * jnp.maximum(d, 0.0)
        bits = pltpu.bitcast(sc, I32)
        key = bits ^ ((bits >> 31) & 0x7FFFFFFF)
        key = jnp.where(off + lane_ck <= t_col, key, INT_MIN)
        khi_ref[:, pl.ds(off, ck)] = (key >> 16).astype(I16)
        klo_ref[:, pl.ds(off, ck)] = ((key & 0xFFFF) + I16_MIN).astype(I16)
        return carry

    lax.fori_loop(0, nch, score_chunk, 0)

    def count(pred):
        def body(c, acc):
            off = pl.multiple_of(c * ck, ck)
            sl = pl.ds(off, ck)
            m = jnp.where(pred(khi_ref[:, sl], klo_ref[:, sl], (off + lane_ck).astype(I16)),
                          jnp.int16(1), jnp.int16(0))
            for q in range(ck // LANES):
                acc = acc + m[:, q * LANES:(q + 1) * LANES]
            return acc
        acc = lax.fori_loop(0, nch, body, jnp.zeros((tq, LANES), I16))
        return jnp.sum(acc.astype(F32), axis=1, keepdims=True)

    def kth_largest(values_of, kf):
        thr = jnp.where(count(lambda hi, lo, col: values_of(hi, lo) >= jnp.int16(0)) >= kf, 0, I16_MIN)

        def bit_step(i, thr):
            cand = thr | jnp.left_shift(jnp.int32(1), 14 - i)
            c16 = cand.astype(I16)
            return jnp.where(count(lambda hi, lo, col: values_of(hi, lo) >= c16) >= kf, cand, thr)

        return lax.fori_loop(0, 15, bit_step, thr)

    kf = float(ksel)
    thi32 = kth_largest(lambda hi, lo: hi, kf)
    thi = thi32.astype(I16)
    cnt_gt_hi = count(lambda hi, lo, col: hi > thi)
    k_lo = kf - cnt_gt_hi

    def mark_chunk(c, carry):
        sl = pl.ds(pl.multiple_of(c * ck, ck), ck)
        klo_ref[:, sl] = jnp.where(khi_ref[:, sl] == thi, klo_ref[:, sl], jnp.int16(I16_MIN))
        return carry

    lax.fori_loop(0, nch, mark_chunk, 0)
    tlo32 = kth_largest(lambda hi, lo: lo, k_lo)
    tlo = tlo32.astype(I16)
    cnt_gt = cnt_gt_hi + count(lambda hi, lo, col: (hi == thi) & (lo > tlo))
    cnt_eq = count(lambda hi, lo, col: (hi == thi) & (lo == tlo))
    need = kf - cnt_gt
    masked_thr = (thi32 == I16_MIN) & (tlo32 == I16_MIN)
    excess = jnp.where(masked_thr, 0.0, cnt_eq - need)
    j_ref[...] = jnp.full(j_ref.shape, 2 ** idx_bits, I32)

    @pl.when(jnp.max(excess) > 0.0)
    def _():
        def idx_step(i, jj):
            cand = jj | jnp.left_shift(jnp.int32(1), idx_bits - 1 - i)
            c16 = cand.astype(I16)
            c_lt = count(lambda hi, lo, col: (hi == thi) & (lo == tlo) & (col < c16))
            return jnp.where(c_lt < need, cand, jj)
        jj = lax.fori_loop(0, idx_bits, idx_step, jnp.zeros((tq, 1), I32))
        j_ref[...] = jnp.broadcast_to(jj, j_ref.shape)

    jmax = j_ref[:, 0:1].astype(I16)
    o_ref[...] = jnp.full(o_ref.shape, NEG_MASK, o_ref.dtype)

    def write_chunk(c, carry):
        off = pl.multiple_of(c * ck, ck)
        sl = pl.ds(off, ck)
        hi = khi_ref[:, sl]
        lo = klo_ref[:, sl]
        col = (off + lane_ck).astype(I16)
        keep = jnp.asarray(0.0, o_ref.dtype)
        drop = jnp.asarray(NEG_MASK, o_ref.dtype)
        on_lo = jnp.where(lo == tlo, jnp.where(col <= jmax, keep, drop), jnp.where(lo > tlo, keep, drop))
        on_hi = jnp.where(hi == thi, on_lo, jnp.where(hi > thi, keep, drop))
        o_ref[0, :, sl] = jnp.where(col <= t16, on_hi, drop)
        return carry

    lax.fori_loop(0, nch, write_chunk, 0)


def _dsa_select(proj3, kit, s, ksel, iq_block):
    b = proj3.shape[0]
    tq = _tile(s, 256)
    ck = _tile(s, 512)
    heads, d_idx = IDX_HEADS, IDX_DH
    iq_w = heads * d_idx
    idx_bits = max(1, int(math.ceil(math.log2(s))))
    assert 2 ** idx_bits < -I16_MIN and s // LANES < -I16_MIN
    return pl.pallas_call(
        functools.partial(_select_kernel, tq=tq, ck=ck, ksel=ksel, heads=heads, d_idx=d_idx,
                          idx_bits=idx_bits),
        out_shape=jax.ShapeDtypeStruct((b, s, s), BF16),
        grid=(b, s // tq),
        in_specs=[pl.BlockSpec((1, tq, iq_w), lambda bi, n: (bi, n, iq_block)),
                  pl.BlockSpec((1, tq, iq_w), lambda bi, n: (bi, n, iq_block + 1)),
                  pl.BlockSpec((1, d_idx, s), lambda bi, n: (bi, 0, 0))],
        out_specs=pl.BlockSpec((1, tq, s), lambda bi, n: (bi, n, 0)),
        scratch_shapes=[pltpu.VMEM((heads * tq, d_idx), BF16),
                        pltpu.VMEM((tq, s), I16),
                        pltpu.VMEM((tq, s), I16),
                        pltpu.VMEM((tq, LANES), I32)],
        compiler_params=_cparams(("parallel", "parallel")),
        name="dsa_select",
    )(proj3, proj3, kit)


def _t5_bucket(n):
    max_exact = REL_BUCKETS // 2
    nf = jnp.maximum(n, 1).astype(F32)
    large = max_exact + (jnp.log(nf / max_exact) / math.log(REL_MAX_DIST / max_exact)
                         * (REL_BUCKETS - max_exact)).astype(I32)
    large = jnp.minimum(large, REL_BUCKETS - 1)
    return jnp.where(n < max_exact, n, large)


def _attn_kernel(rb_ref, q_ref, k_ref, v_ref, mask_ref, pq_ref, pk_ref, o_ref,
                 acc_ref, m_ref, l_ref, bias_ref, s_ref, p_ref, *, tq, tk, heads, dh):
    qb = pl.program_id(1)
    kb = pl.program_id(2)
    kmax = ((qb + 1) * tq - 1) // tk
    nlt = tk // LANES
    gran = LANES

    @pl.when(kb == 0)
    def _():
        acc_ref[...] = jnp.zeros_like(acc_ref)
        m_ref[...] = jnp.full(m_ref.shape, -jnp.inf, F32)
        l_ref[...] = jnp.zeros_like(l_ref)

    def attend(bias_of):
        ones = jnp.ones((tk, LANES), BF16)
        for h in range(heads):
            sl = slice(h * dh, (h + 1) * dh)
            kh = k_ref[0, :, sl]
            v_aug = jnp.concatenate([v_ref[0, :, sl], ones], axis=1)
            add, const = bias_of(h)
            buf = h % 2
            s = lax.dot_general(q_ref[0, :, sl], kh, (((1,), (1,)), ((), ())),
                                preferred_element_type=F32) + add
            s_ref[buf] = s
            tile_max = s[:, 0:LANES]
            for c in range(1, nlt):
                tile_max = jnp.maximum(tile_max, s[:, c * LANES:(c + 1) * LANES])
            m_cur = jnp.broadcast_to(jnp.max(tile_max, axis=1, keepdims=True), (tq, LANES)) + const
            m_old = m_ref[h]
            m_new = jnp.maximum(m_old, m_cur)
            alpha = jnp.exp2(m_old - m_new)
            shift = m_new - const
            for c in range(nlt):
                cs = slice(c * LANES, (c + 1) * LANES)
                p_ref[buf, :, cs] = jnp.exp2(s_ref[buf, :, cs] - shift).astype(BF16)
            pv = jnp.dot(p_ref[buf], v_aug, preferred_element_type=F32)
            m_ref[h] = m_new
            l_ref[h] = alpha * l_ref[h] + pv[:, dh:]
            acc_ref[:, sl] = alpha * acc_ref[:, sl] + pv[:, :dh]

    @pl.when(kb <= kmax)
    def _():
        pq = pq_ref[0]
        pk = pk_ref[0]
        all_far = (jnp.min(pq) - jnp.max(pk)) >= REL_MAX_DIST

        @pl.when(all_far)
        def _():
            attend(lambda h: (mask_ref[0].astype(F32), rb_ref[REL_BUCKETS - 1, h] * LOG2E))

        @pl.when(jnp.logical_not(all_far))
        def _():
            n_lane = lax.broadcasted_iota(I32, (1, REL_MAX_DIST), 1)
            bucket = _t5_bucket(n_lane)
            tabs = []
            for h in range(heads):
                tab = jnp.zeros((1, REL_MAX_DIST), F32)
                for j in range(REL_BUCKETS):
                    tab = jnp.where(bucket == j, rb_ref[j, h] * LOG2E, tab)
                tabs.append(jnp.broadcast_to(tab, (gran, REL_MAX_DIST)))
            k_lo = [jnp.min(pk[:, cj * gran:(cj + 1) * gran]) for cj in range(tk // gran)]
            k_hi = [jnp.max(pk[:, cj * gran:(cj + 1) * gran]) for cj in range(tk // gran)]
            for ri in range(tq // gran):
                rs = slice(ri * gran, (ri + 1) * gran)
                pq_g = pq[rs]
                q_lo = jnp.min(pq_g)
                q_hi = jnp.max(pq_g)
                for cj in range(tk // gran):
                    cs = slice(cj * gran, (cj + 1) * gran)
                    pk_g = pk[:, cs]
                    lo = q_lo - k_hi[cj]
                    hi = q_hi - k_lo[cj]
                    is_far = lo >= REL_MAX_DIST
                    is_zero = hi <= 0
                    maskf = mask_ref[0, rs, cs].astype(F32)

                    @pl.when(is_far | is_zero)
                    def _():
                        for h in range(heads):
                            c_h = jnp.where(is_far, rb_ref[REL_BUCKETS - 1, h], rb_ref[0, h]) * LOG2E
                            bias_ref[h, rs, cs] = maskf + c_h

                    @pl.when(jnp.logical_not(is_far | is_zero))
                    def _():
                        dist = jnp.clip(pq_g - pk_g, 0, REL_MAX_DIST - 1)
                        for h in range(heads):
                            bias_ref[h, rs, cs] = maskf + jnp.take_along_axis(tabs[h], dist, axis=1)
            attend(lambda h: (bias_ref[h], 0.0))

    @pl.when(kb == kmax)
    def _():
        for h in range(heads):
            sl = slice(h * dh, (h + 1) * dh)
            o_ref[0, :, sl] = (acc_ref[:, sl] / l_ref[h]).astype(o_ref.dtype)


def _dsa_attention(qn, kn, proj3, mask, pos_q, pos_k, rel_bias, mix_w):
    b, s, _ = qn.shape
    heads = ATT_HEADS
    dh = mix_w // heads
    tq = _tile(s, 512)
    tk = tq
    assert REL_MAX_DIST == LANES

    def kidx(qb, kb):
        return jnp.minimum(kb, ((qb + 1) * tq - 1) // tk)

    return pl.pallas_call(
        functools.partial(_attn_kernel, tq=tq, tk=tk, heads=heads, dh=dh),
        out_shape=jax.ShapeDtypeStruct((b, s, mix_w), BF16),
        grid=(b, s // tq, s // tk),
        in_specs=[pl.BlockSpec(memory_space=pltpu.SMEM),
                  pl.BlockSpec((1, tq, mix_w), lambda bi, qb, kb: (bi, qb, 0)),
                  pl.BlockSpec((1, tk, mix_w), lambda bi, qb, kb: (bi, kidx(qb, kb), 0)),
                  pl.BlockSpec((1, tk, mix_w), lambda bi, qb, kb: (bi, kidx(qb, kb), 9)),
                  pl.BlockSpec((1, tq, tk), lambda bi, qb, kb: (bi, qb, kidx(qb, kb))),
                  pl.BlockSpec((1, tq, 1), lambda bi, qb, kb: (bi, qb, 0)),
                  pl.BlockSpec((1, 1, tk), lambda bi, qb, kb: (bi, 0, kidx(qb, kb)))],
        out_specs=pl.BlockSpec((1, tq, mix_w), lambda bi, qb, kb: (bi, qb, 0)),
        scratch_shapes=[pltpu.VMEM((tq, mix_w), F32),
                        pltpu.VMEM((heads, tq, LANES), F32),
                        pltpu.VMEM((heads, tq, LANES), F32),
                        pltpu.VMEM((heads, tq, tk), F32),
                        pltpu.VMEM((2, tq, tk), F32),
                        pltpu.VMEM((2, tq, tk), BF16)],
        compiler_params=_cparams(("parallel", "parallel", "arbitrary")),
        name="dsa_attention",
    )(rel_bias, qn, kn, proj3, mask, pos_q, pos_k)


def _merge_kernel(y0_ref, y1_ref, y2_ref, w_ref, g0_ref, g1_ref, g2_ref, o_ref):
    acc = None
    for n, (y_ref, g_ref) in enumerate(((y0_ref, g0_ref), (y1_ref, g1_ref), (y2_ref, g2_ref))):
        br = jnp.dot(y_ref[...], w_ref[n], preferred_element_type=F32)
        term = _sigmoid(g_ref[...].astype(F32)) * br
        acc = term if acc is None else acc + term
    o_ref[...] = acc.astype(o_ref.dtype)


def _branch_merge(ys, w_br, proj2, d_model, mix_w, gate_col0):
    t = proj2.shape[0]
    tm = _tile(t, 1024)
    tn = _tile(d_model, 512)
    yspec = pl.BlockSpec((tm, mix_w), lambda i, j: (i, 0))

    def gspec(n):
        base = (gate_col0 + n * d_model) // tn
        return pl.BlockSpec((tm, tn), lambda i, j, base=base: (i, base + j))

    return pl.pallas_call(
        _merge_kernel,
        out_shape=jax.ShapeDtypeStruct((t, d_model), BF16),
        grid=(t // tm, d_model // tn),
        in_specs=[yspec, yspec, yspec,
                  pl.BlockSpec((N_BRANCH, mix_w, tn), lambda i, j: (0, 0, j)),
                  gspec(0), gspec(1), gspec(2)],
        out_specs=pl.BlockSpec((tm, tn), lambda i, j: (i, j)),
        compiler_params=_cparams(("parallel", "parallel")),
        name="branch_merge",
    )(ys[0], ys[1], ys[2], w_br, proj2, proj2, proj2)


def _glu_kernel(a_ref, w1_ref, w3_ref, o_ref):
    a = a_ref[...]
    h1 = jnp.dot(a, w1_ref[...], preferred_element_type=F32)
    h3 = jnp.dot(a, w3_ref[...], preferred_element_type=F32)
    o_ref[...] = (h1 * _sigmoid(h1) * h3).astype(o_ref.dtype)


def _glu(a, w1, w3):
    t, d = a.shape
    ff = w1.shape[1]
    tm = _tile(t, 1024)
    tn = _tile(ff, 512)
    return pl.pallas_call(
        _glu_kernel,
        out_shape=jax.ShapeDtypeStruct((t, ff), BF16),
        grid=(t // tm, ff // tn),
        in_specs=[pl.BlockSpec((tm, d), lambda i, j: (i, 0)),
                  pl.BlockSpec((d, tn), lambda i, j: (0, j)),
                  pl.BlockSpec((d, tn), lambda i, j: (0, j))],
        out_specs=pl.BlockSpec((tm, tn), lambda i, j: (i, j)),
        compiler_params=_cparams(("parallel", "parallel")),
        name="swiglu_up",
    )(a, w1, w3)


def _router_kernel(x_ref, g_ref, r_ref, comb_ref, sel_ref, *, n_exp):
    x = x_ref[...]
    h = (x * lax.rsqrt(jnp.mean(x * x, axis=-1, keepdims=True) + EPS) * g_ref[...]).astype(BF16)
    logits = jnp.dot(h, r_ref[...], preferred_element_type=F32)
    lane = lax.broadcasted_iota(I32, logits.shape, 1)
    lg = jnp.where(lane < n_exp, logits, -jnp.inf)
    v1 = jnp.max(lg, axis=1, keepdims=True)
    i1 = jnp.min(jnp.where(lg == v1, lane, LANES), axis=1, keepdims=True)
    lg2 = jnp.where(lane == i1, -jnp.inf, lg)
    v2 = jnp.max(lg2, axis=1, keepdims=True)
    i2 = jnp.min(jnp.where(lg2 == v2, lane, LANES), axis=1, keepdims=True)
    e = jnp.exp(v2 - v1)
    w1 = 1.0 / (1.0 + e)
    w2 = e / (1.0 + e)
    comb_ref[...] = jnp.where(lane == i1, w1, jnp.where(lane == i2, w2, 0.0))
    sel_ref[...] = jnp.where((lane == i1) | (lane == i2), 1, 0)


def _router(x2, g, router_w):
    t, d = x2.shape
    n_exp = router_w.shape[1]
    tm = _tile(t, 512)
    rpad = jnp.zeros((d, LANES), BF16).at[:, :n_exp].set(router_w.astype(BF16))
    comb, sel = pl.pallas_call(
        functools.partial(_router_kernel, n_exp=n_exp),
        out_shape=(jax.ShapeDtypeStruct((t, LANES), F32), jax.ShapeDtypeStruct((t, LANES), I32)),
        grid=(t // tm,),
        in_specs=[pl.BlockSpec((tm, d), lambda i: (i, 0)),
                  pl.BlockSpec((1, d), lambda i: (0, 0)),
                  pl.BlockSpec((d, LANES), lambda i: (0, 0))],
        out_specs=(pl.BlockSpec((tm, LANES), lambda i: (i, 0)),
                   pl.BlockSpec((tm, LANES), lambda i: (i, 0))),
        compiler_params=_cparams(("parallel",)),
        name="moe_router",
    )(x2, g.reshape(1, d), rpad)
    return comb[:, :n_exp], sel[:, :n_exp]


def _row_gather(idx_ref, src_hbm, buf_ref, sem, slot, rows, start):
    def body(r8, carry):
        for u in range(ROW_DMA_UNROLL):
            r = r8 * ROW_DMA_UNROLL + u
            cp = pltpu.make_async_copy(src_hbm.at[pl.ds(idx_ref[0, 0, r], 1)],
                                       buf_ref.at[slot, pl.ds(r, 1)], sem.at[slot])
            if start:
                cp.start()
            else:
                cp.wait()
        return carry

    lax.fori_loop(0, rows // ROW_DMA_UNROLL, body, 0)


def _dispatch_kernel(cur_ref, nxt_ref, x_hbm, g_ref, o_ref, buf_ref, sem, *, rows, nsteps):
    i = pl.program_id(0)
    slot = i % 2

    @pl.when(i == 0)
    def _():
        _row_gather(cur_ref, x_hbm, buf_ref, sem, 0, rows, start=True)

    @pl.when(i + 1 < nsteps)
    def _():
        _row_gather(nxt_ref, x_hbm, buf_ref, sem, 1 - slot, rows, start=True)

    _row_gather(cur_ref, x_hbm, buf_ref, sem, slot, rows, start=False)
    x = buf_ref[slot]
    o_ref[...] = (x * lax.rsqrt(jnp.mean(x * x, axis=-1, keepdims=True) + EPS) * g_ref[...]).astype(o_ref.dtype)


def _dispatch(x2, g, src_tok, rows):
    t, d = x2.shape
    p = src_tok.shape[0]
    assert p % rows == 0 and rows % ROW_DMA_UNROLL == 0
    nsteps = p // rows
    src3 = src_tok.reshape(nsteps, 1, rows)
    return pl.pallas_call(
        functools.partial(_dispatch_kernel, rows=rows, nsteps=nsteps),
        out_shape=jax.ShapeDtypeStruct((p, d), BF16),
        grid=(nsteps,),
        in_specs=[pl.BlockSpec((1, 1, rows), lambda i: (i, 0, 0), memory_space=pltpu.SMEM),
                  pl.BlockSpec((1, 1, rows), lambda i: (jnp.minimum(i + 1, nsteps - 1), 0, 0),
                               memory_space=pltpu.SMEM),
                  pl.BlockSpec(memory_space=pl.ANY),
                  pl.BlockSpec((1, d), lambda i: (0, 0))],
        out_specs=pl.BlockSpec((rows, d), lambda i: (i, 0)),
        scratch_shapes=[pltpu.VMEM((2, rows, d), F32), pltpu.SemaphoreType.DMA((2,))],
        compiler_params=_cparams(("arbitrary",)),
        name="moe_dispatch",
    )(src3, src3, x2, g.reshape(1, d))


def _moe_up_kernel(te_ref, tv_ref, a_ref, w1_ref, w3_ref, o_ref):
    i = pl.program_id(0)

    @pl.when(tv_ref[i] == 1)
    def _():
        a = a_ref[...]
        h1 = jnp.dot(a, w1_ref[0], preferred_element_type=F32)
        h3 = jnp.dot(a, w3_ref[0], preferred_element_type=F32)
        o_ref[...] = (h1 * _sigmoid(h1) * h3).astype(o_ref.dtype)

    @pl.when(tv_ref[i] == 0)
    def _():
        o_ref[...] = jnp.zeros_like(o_ref)


def _moe_up(hs, w1, w3, tile_expert, tile_valid, tm):
    p, d = hs.shape
    ff = w1.shape[2]
    tn = _tile(ff, 512)
    nj = ff // tn

    def wmap(i, j, te, tv):
        return (te[i], 0, jnp.where(tv[i] == 1, j, nj - 1))

    return pl.pallas_call(
        _moe_up_kernel,
        out_shape=jax.ShapeDtypeStruct((p, ff), BF16),
        grid_spec=pltpu.PrefetchScalarGridSpec(
            num_scalar_prefetch=2,
            grid=(p // tm, nj),
            in_specs=[pl.BlockSpec((tm, d), lambda i, j, te, tv: (i, 0)),
                      pl.BlockSpec((1, d, tn), wmap),
                      pl.BlockSpec((1, d, tn), wmap)],
            out_specs=pl.BlockSpec((tm, tn), lambda i, j, te, tv: (i, j))),
        compiler_params=_cparams(("arbitrary", "arbitrary")),
        name="moe_up",
    )(tile_expert, tile_valid, hs, w1, w3)


def _moe_down_kernel(te_ref, tv_ref, a_ref, w_ref, o_ref):
    i = pl.program_id(0)

    @pl.when(tv_ref[i] == 1)
    def _():
        o_ref[...] = jnp.dot(a_ref[...], w_ref[0], preferred_element_type=F32)

    @pl.when(tv_ref[i] == 0)
    def _():
        o_ref[...] = jnp.zeros_like(o_ref)


def _moe_down(us, w2, tile_expert, tile_valid, tm):
    p, ff = us.shape
    d = w2.shape[2]
    tn = _tile(d, 512)
    nj = d // tn

    def wmap(i, j, te, tv):
        return (te[i], 0, jnp.where(tv[i] == 1, j, nj - 1))

    return pl.pallas_call(
        _moe_down_kernel,
        out_shape=jax.ShapeDtypeStruct((p, d), F32),
        grid_spec=pltpu.PrefetchScalarGridSpec(
            num_scalar_prefetch=2,
            grid=(p // tm, nj),
            in_specs=[pl.BlockSpec((tm, ff), lambda i, j, te, tv: (i, 0)),
                      pl.BlockSpec((1, ff, tn), wmap)],
            out_specs=pl.BlockSpec((tm, tn), lambda i, j, te, tv: (i, j))),
        compiler_params=_cparams(("arbitrary", "arbitrary")),
        name="moe_down",
    )(tile_expert, tile_valid, us, w2)


def _combine_kernel(p0_ref, p1_ref, n0_ref, n1_ref, x_ref, w0_ref, w1_ref, ys_hbm, o_ref,
                    a_ref, b_ref, sem_a, sem_b, *, rows, nsteps):
    i = pl.program_id(0)
    slot = i % 2

    @pl.when(i == 0)
    def _():
        _row_gather(p0_ref, ys_hbm, a_ref, sem_a, 0, rows, start=True)
        _row_gather(p1_ref, ys_hbm, b_ref, sem_b, 0, rows, start=True)

    @pl.when(i + 1 < nsteps)
    def _():
        _row_gather(n0_ref, ys_hbm, a_ref, sem_a, 1 - slot, rows, start=True)
        _row_gather(n1_ref, ys_hbm, b_ref, sem_b, 1 - slot, rows, start=True)

    _row_gather(p0_ref, ys_hbm, a_ref, sem_a, slot, rows, start=False)
    _row_gather(p1_ref, ys_hbm, b_ref, sem_b, slot, rows, start=False)
    o_ref[...] = x_ref[...] + (w0_ref[...] * a_ref[slot] + w1_ref[...] * b_ref[slot])


def _combine(x2, ys, pos0, pos1, w0, w1, rows):
    t, d = x2.shape
    assert t % rows == 0 and rows % ROW_DMA_UNROLL == 0
    nsteps = t // rows
    cur = pl.BlockSpec((1, 1, rows), lambda i: (i, 0, 0), memory_space=pltpu.SMEM)
    nxt = pl.BlockSpec((1, 1, rows), lambda i: (jnp.minimum(i + 1, nsteps - 1), 0, 0),
                       memory_space=pltpu.SMEM)
    p0 = pos0.reshape(nsteps, 1, rows)
    p1 = pos1.reshape(nsteps, 1, rows)
    return pl.pallas_call(
        functools.partial(_combine_kernel, rows=rows, nsteps=nsteps),
        out_shape=jax.ShapeDtypeStruct((t, d), F32),
        grid=(nsteps,),
        in_specs=[cur, cur, nxt, nxt,
                  pl.BlockSpec((rows, d), lambda i: (i, 0)),
                  pl.BlockSpec((rows, 1), lambda i: (i, 0)),
                  pl.BlockSpec((rows, 1), lambda i: (i, 0)),
                  pl.BlockSpec(memory_space=pl.ANY)],
        out_specs=pl.BlockSpec((rows, d), lambda i: (i, 0)),
        scratch_shapes=[pltpu.VMEM((2, rows, d), F32), pltpu.VMEM((2, rows, d), F32),
                        pltpu.SemaphoreType.DMA((2,)), pltpu.SemaphoreType.DMA((2,))],
        compiler_params=_cparams(("arbitrary",)),
        name="moe_combine",
    )(p0, p1, p0, p1, x2, w0, w1, ys)


def _moe(x2, g, router_w, w1, w3, w2):
    t, d = x2.shape
    n_exp = router_w.shape[1]
    tm = _tile(t, 512)
    comb, sel = _router(x2, g, router_w)
    rank = jnp.cumsum(sel, axis=0) - sel
    counts = jnp.sum(sel, axis=0)
    gsize = ((counts + tm - 1) // tm) * tm
    gend = jnp.cumsum(gsize)
    goff = gend - gsize
    dest = goff[None, :] + rank
    e0 = jnp.argmax(sel, axis=1)
    e1 = (n_exp - 1) - jnp.argmax(sel[:, ::-1], axis=1)
    pos0 = jnp.take_along_axis(dest, e0[:, None], axis=1)[:, 0].astype(I32)
    pos1 = jnp.take_along_axis(dest, e1[:, None], axis=1)[:, 0].astype(I32)
    wt0 = jnp.take_along_axis(comb, e0[:, None], axis=1)
    wt1 = jnp.take_along_axis(comb, e1[:, None], axis=1)
    p = t * TOP_K + n_exp * tm
    tok = jnp.arange(t, dtype=I32)
    src_tok = jnp.zeros((p,), I32).at[jnp.concatenate([pos0, pos1])].set(jnp.concatenate([tok, tok]))
    n_tiles = p // tm
    tstart = jnp.arange(n_tiles, dtype=I32) * tm
    tile_valid = (tstart < gend[-1]).astype(I32)
    te_raw = jnp.minimum(jnp.sum((tstart[:, None] >= gend[None, :]).astype(I32), axis=1), n_exp - 1)
    te_last = te_raw[jnp.maximum(gend[-1] // tm - 1, 0)]
    tile_expert = jnp.where(tile_valid == 1, te_raw, te_last).astype(I32)

    hs = _dispatch(x2, g, src_tok, rows=_tile(p, 256))
    us = _moe_up(hs, w1, w3, tile_expert, tile_valid, tm)
    ys = _moe_down(us, w2, tile_expert, tile_valid, tm)
    return _combine(x2, ys, pos0, pos1, wt0, wt1, rows=_tile(t, 256))


def _pack_w_in(w_in_l, d_model, mix_w):
    iq_w = IDX_HEADS * IDX_DH
    n_main = 10 * mix_w
    small = IDX_DH + IDX_HEADS
    gates0 = n_main + iq_w + small
    pad = iq_w - small
    w = w_in_l.astype(BF16)
    return jnp.concatenate([w[:, :n_main], w[:, gates0:gates0 + N_BRANCH * d_model],
                            w[:, n_main:n_main + iq_w], w[:, n_main + iq_w:gates0],
                            jnp.zeros((d_model, pad), BF16)], axis=1)


def _mixer(x2, b, s, positions, rel_bias, norm_g, w_in_l, w_br_l, w_o_l, gn_w, gn_b, conv_w, q_norm, k_norm):
    t, d_model = x2.shape
    mix_w = d_model // 2
    iq_w = IDX_HEADS * IDX_DH
    gate_col0 = 10 * mix_w
    iq_col0 = gate_col0 + N_BRANCH * d_model
    assert iq_col0 % iq_w == 0

    h = _rmsnorm(x2, norm_g)
    proj2 = _matmul(h, _pack_w_in(w_in_l, d_model, mix_w), BF16, 1024, 512, name="in_proj")
    proj3 = proj2.reshape(b, s, proj2.shape[1])
    pos_q = positions.reshape(b, s, 1)
    pos_k = positions.reshape(b, 1, s)

    y_ret = _retention_branch(proj3, pos_q, gn_w, gn_b, mix_w)
    y_conv = _conv_branch(proj3, conv_w, mix_w)

    ik0 = iq_col0 + iq_w
    kit = jnp.swapaxes(proj3[:, :, ik0:ik0 + IDX_DH], 1, 2)
    mask = _dsa_select(proj3, kit, s, min(TOPK_MAX, s // 4), iq_col0 // iq_w)
    qn, kn = _qk_norm(proj3, q_norm, k_norm, mix_w)
    y_att = _dsa_attention(qn, kn, proj3, mask, pos_q, pos_k, rel_bias, mix_w)

    ys = [y.reshape(t, mix_w) for y in (y_ret, y_conv, y_att)]
    merged = _branch_merge(ys, w_br_l.astype(BF16), proj2, d_model, mix_w, gate_col0)
    return _matmul(merged, w_o_l.astype(BF16), F32, 1024, 512, res=x2, name="out_proj")


def kernel(x, positions, rel_bias, norm_mix, norm_ffn, w_in, w_br, w_o, ret_gn_w, ret_gn_b, conv_w,
           q_norm, k_norm, ffn_w1, ffn_w3, ffn_w2, moe_router, moe_w1, moe_w3, moe_w2):
    b, s, d_model = x.shape
    depth = w_in.shape[0]
    x2 = x.reshape(b * s, d_model)
    for l in range(depth):
        x2 = _mixer(x2, b, s, positions, rel_bias, norm_mix[l], w_in[l], w_br[l], w_o[l],
                    ret_gn_w[l], ret_gn_b[l], conv_w[l], q_norm[l], k_norm[l])
        if l % 2 == 0:
            i = l // 2
            h = _rmsnorm(x2, norm_ffn[l])
            u = _glu(h, ffn_w1[i].astype(BF16), ffn_w3[i].astype(BF16))
            x2 = _matmul(u, ffn_w2[i].astype(BF16), F32, 512, 512, res=x2, name="ffn_down")
        else:
            i = l // 2
            x2 = _moe(x2, norm_ffn[l], moe_router[i], moe_w1[i].astype(BF16), moe_w3[i].astype(BF16),
                      moe_w2[i].astype(BF16))
    return x2.reshape(b, s, d_model)
```

```python
import functools
import math

import numpy as np
import jax
import jax.numpy as jnp
from jax import lax
from jax.experimental import pallas as pl
from jax.experimental.pallas import tpu as pltpu

F32 = jnp.float32
BF16 = jnp.bfloat16
I32 = jnp.int32

EPS = 1e-6
N_BRANCH = 3
RET_HEADS = 8
RET_CHUNK = 128
RET_ROPE_BASE = 10000.0
CONV_K = 3
ATT_HEADS = 8
IDX_HEADS = 8
IDX_DH = 64
TOPK_MAX = 256
REL_BUCKETS = 32
REL_MAX_DIST = 128
TOP_K = 2

LANES = 128
SUBLANES = 8
VMEM_LIMIT_BYTES = 56 * 1024 * 1024
ROW_DMA_UNROLL = 8

NEG_MASK = -1e30
LOG2E = math.log2(math.e)
INT_MIN = -2 ** 31
KEY_BITS = 32


def _cparams(sem):
    return pltpu.CompilerParams(dimension_semantics=sem, vmem_limit_bytes=VMEM_LIMIT_BYTES)


def _sigmoid(x):
    return 1.0 / (1.0 + jnp.exp(-x))


def _tile(n, pref):
    t = min(n, pref)
    assert n % t == 0, (n, pref)
    return t


def _rmsnorm_kernel(x_ref, g_ref, o_ref):
    x = x_ref[...]
    ms = jnp.mean(x * x, axis=-1, keepdims=True)
    o_ref[...] = (x * lax.rsqrt(ms + EPS) * g_ref[...]).astype(o_ref.dtype)


def _rmsnorm(x2, g):
    t, d = x2.shape
    tm = _tile(t, 512)
    return pl.pallas_call(
        _rmsnorm_kernel,
        out_shape=jax.ShapeDtypeStruct((t, d), BF16),
        grid=(t // tm,),
        in_specs=[pl.BlockSpec((tm, d), lambda i: (i, 0)),
                  pl.BlockSpec((1, d), lambda i: (0, 0))],
        out_specs=pl.BlockSpec((tm, d), lambda i: (i, 0)),
        compiler_params=_cparams(("parallel",)),
        name="rmsnorm",
    )(x2, g.reshape(1, d))


def _mm_kernel(a_ref, b_ref, o_ref):
    o_ref[...] = jnp.dot(a_ref[...], b_ref[...], preferred_element_type=F32).astype(o_ref.dtype)


def _mm_res_kernel(a_ref, b_ref, r_ref, o_ref):
    o_ref[...] = (r_ref[...] + jnp.dot(a_ref[...], b_ref[...], preferred_element_type=F32)).astype(o_ref.dtype)


def _matmul(a, b, out_dtype, tm, tn, res=None, name="matmul"):
    m, k = a.shape
    k2, n = b.shape
    assert k == k2
    tm = _tile(m, tm)
    tn = _tile(n, tn)
    in_specs = [pl.BlockSpec((tm, k), lambda i, j: (i, 0)),
                pl.BlockSpec((k, tn), lambda i, j: (0, j))]
    args = [a, b]
    kern = _mm_kernel
    if res is not None:
        in_specs.append(pl.BlockSpec((tm, tn), lambda i, j: (i, j)))
        args.append(res)
        kern = _mm_res_kernel
    return pl.pallas_call(
        kern,
        out_shape=jax.ShapeDtypeStruct((m, n), out_dtype),
        grid=(m // tm, n // tn),
        in_specs=in_specs,
        out_specs=pl.BlockSpec((tm, tn), lambda i, j: (i, j)),
        compiler_params=_cparams(("parallel", "parallel")),
        name=name,
    )(*args)


def _ret_kernel(pos_ref, invf_ref, q_ref, k_ref, v_ref, g_ref, dec_ref, kdec_ref, qdec_ref,
                gnw_ref, gnb_ref, o_ref, state_ref, *, heads, dh, chunk_decay):
    @pl.when(pl.program_id(1) == 0)
    def _():
        state_ref[...] = jnp.zeros_like(state_ref)

    c = q_ref.shape[1]
    ang = pos_ref[0].astype(F32) * invf_ref[...]
    cos = jnp.cos(ang)
    sin = jnp.sin(ang)
    lane = lax.broadcasted_iota(I32, (c, dh), 1)
    sin_signed = jnp.where(lane < dh // 2, -sin, sin)
    scale = dh ** -0.5
    for h in range(heads):
        sl = slice(h * dh, (h + 1) * dh)
        q = q_ref[0, :, sl].astype(F32)
        k = k_ref[0, :, sl].astype(F32)
        v = v_ref[0, :, sl]
        qr = (q * cos + pltpu.roll(q, dh // 2, 1) * sin_signed).astype(BF16)
        kr = (k * cos + pltpu.roll(k, dh // 2, 1) * sin_signed) * scale
        scores = lax.dot_general(qr, kr.astype(BF16), (((1,), (1,)), ((), ())),
                                 preferred_element_type=F32) * dec_ref[h]
        intra = jnp.dot(scores.astype(BF16), v, preferred_element_type=F32)
        prev = state_ref[h]
        cross = jnp.dot(qr, prev.astype(BF16), preferred_element_type=F32) * qdec_ref[h]
        kd = (kr * kdec_ref[h]).astype(BF16)
        kv = lax.dot_general(kd, v, (((0,), (0,)), ((), ())), preferred_element_type=F32)
        state_ref[h] = prev * chunk_decay[h] + kv
        ret = intra + cross
        mu = jnp.mean(ret, axis=-1, keepdims=True)
        cen = ret - mu
        var = jnp.mean(cen * cen, axis=-1, keepdims=True)
        gn = cen * lax.rsqrt(var + EPS) * gnw_ref[:, sl] + gnb_ref[:, sl]
        g = g_ref[0, :, sl].astype(F32)
        o_ref[0, :, sl] = (g * _sigmoid(g) * gn).astype(o_ref.dtype)


def _retention_branch(proj3, pos3, gn_w, gn_b, mix_w):
    b, s, _ = proj3.shape
    heads, c = RET_HEADS, RET_CHUNK
    dh = mix_w // heads
    assert dh == LANES and s % c == 0
    log_g = np.log1p(-np.exp2(-5.0 - np.arange(heads, dtype=np.float64)))
    pos = np.arange(c, dtype=np.float64)
    diff = pos[:, None] - pos[None, :]
    intra_decay = np.where(diff >= 0, np.exp(np.maximum(diff, 0.0)[None] * log_g[:, None, None]), 0.0)
    k_decay = np.exp((c - 1 - pos)[None, :] * log_g[:, None])
    q_decay = np.exp((pos + 1)[None, :] * log_g[:, None])
    chunk_decay = tuple(float(x) for x in np.exp(c * log_g))
    dec = jnp.asarray(intra_decay, F32)
    kdec = jnp.asarray(np.broadcast_to(k_decay[:, :, None], (heads, c, dh)), F32)
    qdec = jnp.asarray(np.broadcast_to(q_decay[:, :, None], (heads, c, dh)), F32)
    inv_freq = RET_ROPE_BASE ** (-jnp.arange(0, dh, 2, dtype=F32) / dh)
    invf = jnp.concatenate([inv_freq, inv_freq]).reshape(1, dh)

    def col(cb):
        return pl.BlockSpec((1, c, mix_w), lambda bi, n, cb=cb: (bi, n, cb))

    const3 = pl.BlockSpec((heads, c, dh), lambda bi, n: (0, 0, 0))
    return pl.pallas_call(
        functools.partial(_ret_kernel, heads=heads, dh=dh, chunk_decay=chunk_decay),
        out_shape=jax.ShapeDtypeStruct((b, s, mix_w), BF16),
        grid=(b, s // c),
        in_specs=[pl.BlockSpec((1, c, 1), lambda bi, n: (bi, n, 0)),
                  pl.BlockSpec((1, dh), lambda bi, n: (0, 0)),
                  col(0), col(1), col(2), col(3),
                  const3, const3, const3,
                  pl.BlockSpec((1, mix_w), lambda bi, n: (0, 0)),
                  pl.BlockSpec((1, mix_w), lambda bi, n: (0, 0))],
        out_specs=pl.BlockSpec((1, c, mix_w), lambda bi, n: (bi, n, 0)),
        scratch_shapes=[pltpu.VMEM((heads, dh, dh), F32)],
        compiler_params=_cparams(("arbitrary", "arbitrary")),
        name="retention",
    )(pos3, invf, proj3, proj3, proj3, proj3, dec, kdec, qdec,
      gn_w.reshape(1, mix_w), gn_b.reshape(1, mix_w))


def _conv_kernel(b_ref, c_ref, u_ref, w_ref, o_ref, ubuf_ref):
    ts = o_ref.shape[1]
    halo = SUBLANES

    @pl.when(pl.program_id(1) == 0)
    def _():
        ubuf_ref[0:halo, :] = jnp.zeros((halo, ubuf_ref.shape[1]), F32)

    u = c_ref[0].astype(F32) * u_ref[0].astype(F32)
    ubuf_ref[halo:halo + ts, :] = u
    u1 = ubuf_ref[halo - 1:halo - 1 + ts, :]
    u2 = ubuf_ref[halo - 2:halo - 2 + ts, :]
    conv = w_ref[0:1, :] * u2 + w_ref[1:2, :] * u1 + w_ref[2:3, :] * u
    o_ref[0] = (b_ref[0].astype(F32) * conv).astype(o_ref.dtype)
    ubuf_ref[0:halo, :] = ubuf_ref[ts:ts + halo, :]


def _conv_branch(proj3, conv_w, mix_w):
    b, s, _ = proj3.shape
    ts = _tile(s, 512)

    def col(cb):
        return pl.BlockSpec((1, ts, mix_w), lambda bi, n, cb=cb: (bi, n, cb))

    return pl.pallas_call(
        _conv_kernel,
        out_shape=jax.ShapeDtypeStruct((b, s, mix_w), BF16),
        grid=(b, s // ts),
        in_specs=[col(4), col(5), col(6),
                  pl.BlockSpec((CONV_K, mix_w), lambda bi, n: (0, 0))],
        out_specs=pl.BlockSpec((1, ts, mix_w), lambda bi, n: (bi, n, 0)),
        scratch_shapes=[pltpu.VMEM((ts + SUBLANES, mix_w), F32)],
        compiler_params=_cparams(("arbitrary", "arbitrary")),
        name="short_conv",
    )(proj3, proj3, proj3, conv_w)


def _qknorm_kernel(q_ref, k_ref, qn_ref, kn_ref, qo_ref, ko_ref, *, heads, dh, q_scale):
    for h in range(heads):
        sl = slice(h * dh, (h + 1) * dh)
        q = q_ref[0, :, sl].astype(F32)
        k = k_ref[0, :, sl].astype(F32)
        qy = q * lax.rsqrt(jnp.mean(q * q, axis=-1, keepdims=True) + EPS) * qn_ref[...]
        ky = k * lax.rsqrt(jnp.mean(k * k, axis=-1, keepdims=True) + EPS) * kn_ref[...]
        qo_ref[0, :, sl] = (qy * q_scale).astype(qo_ref.dtype)
        ko_ref[0, :, sl] = ky.astype(ko_ref.dtype)


def _qk_norm(proj3, q_norm, k_norm, mix_w):
    b, s, _ = proj3.shape
    heads = ATT_HEADS
    dh = mix_w // heads
    ts = _tile(s, 512)
    out = jax.ShapeDtypeStruct((b, s, mix_w), BF16)
    blk = pl.BlockSpec((1, ts, mix_w), lambda bi, n: (bi, n, 0))
    return pl.pallas_call(
        functools.partial(_qknorm_kernel, heads=heads, dh=dh, q_scale=dh ** -0.5 * LOG2E),
        out_shape=(out, out),
        grid=(b, s // ts),
        in_specs=[pl.BlockSpec((1, ts, mix_w), lambda bi, n: (bi, n, 7)),
                  pl.BlockSpec((1, ts, mix_w), lambda bi, n: (bi, n, 8)),
                  pl.BlockSpec((1, dh), lambda bi, n: (0, 0)),
                  pl.BlockSpec((1, dh), lambda bi, n: (0, 0))],
        out_specs=(blk, blk),
        compiler_params=_cparams(("parallel", "parallel")),
        name="qk_norm",
    )(proj3, proj3, q_norm.reshape(1, dh), k_norm.reshape(1, dh))


def _select_kernel(iq_ref, sm_ref, kit_ref, tri_ref, o_ref, qall_ref, keys_ref, planes_ref, eq_ref, *,
                   tq, ck, ksel, heads, d_idx):
    qb = pl.program_id(1)
    row0 = qb * tq
    s_len = keys_ref.shape[1]
    nch = (row0 + tq + ck - 1) // ck
    ngrp = s_len // (KEY_BITS * LANES)
    for h in range(heads):
        qall_ref[h * tq:(h + 1) * tq, :] = iq_ref[0, :, h * d_idx:(h + 1) * d_idx]
    w = sm_ref[0, :, d_idx:d_idx + heads].astype(F32) * (heads ** -0.5 * d_idx ** -0.5)
    t_col = row0 + lax.broadcasted_iota(I32, (tq, 1), 0)
    lane_ck = lax.broadcasted_iota(I32, (1, ck), 1)

    def score_chunk(c, carry):
        off = pl.multiple_of(c * ck, ck)
        kt = kit_ref[0, :, pl.ds(off, ck)]
        sc = jnp.zeros((tq, ck), F32)
        for h in range(heads):
            d = jnp.dot(qall_ref[h * tq:(h + 1) * tq, :], kt, preferred_element_type=F32)
            sc = sc + w[:, h:h + 1] * jnp.maximum(d, 0.0)
        bits = pltpu.bitcast(sc, I32)
        key = bits ^ ((bits >> 31) & 0x7FFFFFFF)
        keys_ref[:, pl.ds(off, ck)] = jnp.where(off + lane_ck <= t_col, key, INT_MIN)
        return carry

    def clear_chunk(c, carry):
        keys_ref[:, pl.ds(pl.multiple_of(c * ck, ck), ck)] = jnp.full((tq, ck), INT_MIN, I32)
        return carry

    lax.fori_loop(0, nch, score_chunk, 0)
    lax.fori_loop(nch, s_len // ck, clear_chunk, 0)

    def transpose_rows(r, carry):
        rs = pl.ds(pl.multiple_of(r * SUBLANES, SUBLANES), SUBLANES)
        for g in range(ngrp):
            a = [keys_ref[rs, (g * KEY_BITS + j) * LANES:(g * KEY_BITS + j + 1) * LANES] ^ INT_MIN
                 for j in range(KEY_BITS)]
            j, m = KEY_BITS // 2, (1 << (KEY_BITS // 2)) - 1
            while j:
                k = 0
                while k < KEY_BITS:
                    t = (a[k] ^ (a[k + j] >> j)) & m
                    a[k] = a[k] ^ t
                    a[k + j] = a[k + j] ^ (t << j)
                    k = (k + j + 1) & ~j
                j >>= 1
                m = m ^ (m << j)
            for p in range(KEY_BITS):
                planes_ref[p, g, rs, :] = a[p]
        return carry

    lax.fori_loop(0, tq // SUBLANES, transpose_rows, 0)
    eq_ref[...] = jnp.full(eq_ref.shape, -1, I32)
    kf = float(ksel)

    def bit_pass(p, carry):
        above, thr_u = carry
        ones = jnp.zeros((tq, LANES), I32)
        for g in range(ngrp):
            ones = ones + lax.population_count(eq_ref[g] & planes_ref[p, g])
        c1 = jnp.sum(ones.astype(F32), axis=1, keepdims=True)
        take = (above + c1) >= kf
        for g in range(ngrp):
            e = eq_ref[g]
            x = e & planes_ref[p, g]
            eq_ref[g] = jnp.where(take, x, e ^ x)
        bit = jnp.left_shift(jnp.int32(1), KEY_BITS - 1 - p)
        return jnp.where(take, above, above + c1), jnp.where(take, thr_u | bit, thr_u)

    cnt_gt, thr_u = lax.fori_loop(0, KEY_BITS, bit_pass,
                                  (jnp.zeros((tq, 1), F32), jnp.zeros((tq, 1), I32)))
    thr = thr_u ^ INT_MIN
    eq_cnt = jnp.zeros((tq, LANES), I32)
    for g in range(ngrp):
        eq_cnt = eq_cnt + lax.population_count(eq_ref[g])
    cnt_eq = jnp.sum(eq_cnt.astype(F32), axis=1, keepdims=True)

    need = kf - cnt_gt
    excess = jnp.where(thr == INT_MIN, 0.0, cnt_eq - need)
    o_ref[...] = jnp.full(o_ref.shape, NEG_MASK, o_ref.dtype)

    def write_chunk(c, eq_before, ties):
        off = pl.multiple_of(c * ck, ck)
        sl = pl.ds(off, ck)
        k = keys_ref[:, sl]
        if ties:
            eq = jnp.where(k == thr, 1.0, 0.0).astype(BF16)
            rank = eq_before + jnp.dot(eq, tri_ref[...], preferred_element_type=F32)
            on_eq = jnp.where(rank <= need, 0.0, NEG_MASK)
            eq_before = rank[:, ck - 1:ck]
        else:
            on_eq = 0.0
        val = jnp.where(k > thr, 0.0, jnp.where(k == thr, on_eq, NEG_MASK))
        o_ref[0, :, sl] = jnp.where(off + lane_ck <= t_col, val, NEG_MASK).astype(o_ref.dtype)
        return eq_before

    any_excess = jnp.max(excess) > 0.0

    @pl.when(any_excess)
    def _():
        lax.fori_loop(0, nch, lambda c, e: write_chunk(c, e, True), jnp.zeros((tq, 1), F32))

    @pl.when(jnp.logical_not(any_excess))
    def _():
        lax.fori_loop(0, nch, lambda c, e: write_chunk(c, e, False), 0)


def _dsa_select(proj3, kit, s, ksel, iq_block):
    b = proj3.shape[0]
    tq = _tile(s, 256)
    ck = _tile(s, 512)
    heads, d_idx = IDX_HEADS, IDX_DH
    iq_w = heads * d_idx
    assert s % (KEY_BITS * LANES) == 0 or s < KEY_BITS * LANES
    sp = max(s, KEY_BITS * LANES)
    tri = (jnp.arange(ck)[:, None] <= jnp.arange(ck)[None, :]).astype(BF16)
    return pl.pallas_call(
        functools.partial(_select_kernel, tq=tq, ck=ck, ksel=ksel, heads=heads, d_idx=d_idx),
        out_shape=jax.ShapeDtypeStruct((b, s, s), BF16),
        grid=(b, s // tq),
        in_specs=[pl.BlockSpec((1, tq, iq_w), lambda bi, n: (bi, n, iq_block)),
                  pl.BlockSpec((1, tq, iq_w), lambda bi, n: (bi, n, iq_block + 1)),
                  pl.BlockSpec((1, d_idx, s), lambda bi, n: (bi, 0, 0)),
                  pl.BlockSpec((ck, ck), lambda bi, n: (0, 0))],
        out_specs=pl.BlockSpec((1, tq, s), lambda bi, n: (bi, n, 0)),
        scratch_shapes=[pltpu.VMEM((heads * tq, d_idx), BF16),
                        pltpu.VMEM((tq, sp), I32),
                        pltpu.VMEM((KEY_BITS, sp // (KEY_BITS * LANES), tq, LANES), I32),
                        pltpu.VMEM((sp // (KEY_BITS * LANES), tq, LANES), I32)],
        compiler_params=_cparams(("parallel", "parallel")),
        name="dsa_select",
    )(proj3, proj3, kit, tri)


def _t5_bucket(n):
    max_exact = REL_BUCKETS // 2
    nf = jnp.maximum(n, 1).astype(F32)
    large = max_exact + (jnp.log(nf / max_exact) / math.log(REL_MAX_DIST / max_exact)
                         * (REL_BUCKETS - max_exact)).astype(I32)
    large = jnp.minimum(large, REL_BUCKETS - 1)
    return jnp.where(n < max_exact, n, large)


def _attn_kernel(qb_ref, kb_ref, rb_ref, q_ref, k_ref, v_ref, mask_ref, pq_ref, pk_ref, o_ref,
                 acc_ref, m_ref, l_ref, bias_ref, s_ref, p_ref, *, tq, tk, heads, dh):
    qb = qb_ref[pl.program_id(1)]
    kb = kb_ref[pl.program_id(1)]
    kmax = ((qb + 1) * tq - 1) // tk
    nlt = tk // LANES
    gran = LANES

    @pl.when(kb == 0)
    def _():
        acc_ref[...] = jnp.zeros_like(acc_ref)
        m_ref[...] = jnp.full(m_ref.shape, -jnp.inf, F32)
        l_ref[...] = jnp.zeros_like(l_ref)

    def attend(bias_of):
        ones = jnp.ones((tk, LANES), BF16)
        for h in range(heads):
            sl = slice(h * dh, (h + 1) * dh)
            kh = k_ref[0, :, sl]
            v_aug = jnp.concatenate([v_ref[0, :, sl], ones], axis=1)
            add, const = bias_of(h)
            buf = h % 2
            s = lax.dot_general(q_ref[0, :, sl], kh, (((1,), (1,)), ((), ())),
                                preferred_element_type=F32) + add
            s_ref[buf] = s
            tile_max = s[:, 0:LANES]
            for c in range(1, nlt):
                tile_max = jnp.maximum(tile_max, s[:, c * LANES:(c + 1) * LANES])
            m_cur = jnp.broadcast_to(jnp.max(tile_max, axis=1, keepdims=True), (tq, LANES)) + const
            m_old = m_ref[h]
            m_new = jnp.maximum(m_old, m_cur)
            alpha = jnp.exp2(m_old - m_new)
            shift = m_new - const
            for c in range(nlt):
                cs = slice(c * LANES, (c + 1) * LANES)
                p_ref[buf, :, cs] = jnp.exp2(s_ref[buf, :, cs] - shift).astype(BF16)
            pv = jnp.dot(p_ref[buf], v_aug, preferred_element_type=F32)
            m_ref[h] = m_new
            l_ref[h] = alpha * l_ref[h] + pv[:, dh:]
            acc_ref[:, sl] = alpha * acc_ref[:, sl] + pv[:, :dh]

    @pl.when(kb <= kmax)
    def _():
        pq = pq_ref[0]
        pk = pk_ref[0]
        all_far = (jnp.min(pq) - jnp.max(pk)) >= REL_MAX_DIST

        @pl.when(all_far)
        def _():
            attend(lambda h: (mask_ref[0].astype(F32), rb_ref[REL_BUCKETS - 1, h] * LOG2E))

        @pl.when(jnp.logical_not(all_far))
        def _():
            n_lane = lax.broadcasted_iota(I32, (1, REL_MAX_DIST), 1)
            bucket = _t5_bucket(n_lane)
            tabs = []
            for h in range(heads):
                tab = jnp.zeros((1, REL_MAX_DIST), F32)
                for j in range(REL_BUCKETS):
                    tab = jnp.where(bucket == j, rb_ref[j, h] * LOG2E, tab)
                tabs.append(jnp.broadcast_to(tab, (gran, REL_MAX_DIST)))
            k_lo = [jnp.min(pk[:, cj * gran:(cj + 1) * gran]) for cj in range(tk // gran)]
            k_hi = [jnp.max(pk[:, cj * gran:(cj + 1) * gran]) for cj in range(tk // gran)]
            for ri in range(tq // gran):
                rs = slice(ri * gran, (ri + 1) * gran)
                pq_g = pq[rs]
                q_lo = jnp.min(pq_g)
                q_hi = jnp.max(pq_g)
                for cj in range(tk // gran):
                    cs = slice(cj * gran, (cj + 1) * gran)
                    pk_g = pk[:, cs]
                    lo = q_lo - k_hi[cj]
                    hi = q_hi - k_lo[cj]
                    is_far = lo >= REL_MAX_DIST
                    is_zero = hi <= 0
                    maskf = mask_ref[0, rs, cs].astype(F32)

                    @pl.when(is_far | is_zero)
                    def _():
                        for h in range(heads):
                            c_h = jnp.where(is_far, rb_ref[REL_BUCKETS - 1, h], rb_ref[0, h]) * LOG2E
                            bias_ref[h, rs, cs] = maskf + c_h

                    @pl.when(jnp.logical_not(is_far | is_zero))
                    def _():
                        dist = jnp.clip(pq_g - pk_g, 0, REL_MAX_DIST - 1)
                        for h in range(heads):
                            bias_ref[h, rs, cs] = maskf + jnp.take_along_axis(tabs[h], dist, axis=1)
            attend(lambda h: (bias_ref[h], 0.0))

    @pl.when(kb == kmax)
    def _():
        for h in range(heads):
            sl = slice(h * dh, (h + 1) * dh)
            o_ref[0, :, sl] = (acc_ref[:, sl] / l_ref[h]).astype(o_ref.dtype)


def _dsa_attention(qn, kn, proj3, mask, pos_q, pos_k, rel_bias, mix_w):
    b, s, _ = qn.shape
    heads = ATT_HEADS
    dh = mix_w // heads
    tq = _tile(s, 512)
    tk = tq
    assert REL_MAX_DIST == LANES

    tiles = [(qb, kb) for qb in range(s // tq) for kb in range(((qb + 1) * tq - 1) // tk + 1)]
    qb_of = jnp.asarray([t[0] for t in tiles], I32)
    kb_of = jnp.asarray([t[1] for t in tiles], I32)

    return pl.pallas_call(
        functools.partial(_attn_kernel, tq=tq, tk=tk, heads=heads, dh=dh),
        out_shape=jax.ShapeDtypeStruct((b, s, mix_w), BF16),
        grid_spec=pltpu.PrefetchScalarGridSpec(
            num_scalar_prefetch=2,
            grid=(b, len(tiles)),
            in_specs=[pl.BlockSpec(memory_space=pltpu.SMEM),
                      pl.BlockSpec((1, tq, mix_w), lambda bi, st, qo, ko: (bi, qo[st], 0)),
                      pl.BlockSpec((1, tk, mix_w), lambda bi, st, qo, ko: (bi, ko[st], 0)),
                      pl.BlockSpec((1, tk, mix_w), lambda bi, st, qo, ko: (bi, ko[st], 9)),
                      pl.BlockSpec((1, tq, tk), lambda bi, st, qo, ko: (bi, qo[st], ko[st])),
                      pl.BlockSpec((1, tq, 1), lambda bi, st, qo, ko: (bi, qo[st], 0)),
                      pl.BlockSpec((1, 1, tk), lambda bi, st, qo, ko: (bi, 0, ko[st]))],
            out_specs=pl.BlockSpec((1, tq, mix_w), lambda bi, st, qo, ko: (bi, qo[st], 0)),
            scratch_shapes=[pltpu.VMEM((tq, mix_w), F32),
                            pltpu.VMEM((heads, tq, LANES), F32),
                            pltpu.VMEM((heads, tq, LANES), F32),
                            pltpu.VMEM((heads, tq, tk), F32),
                            pltpu.VMEM((2, tq, tk), F32),
                            pltpu.VMEM((2, tq, tk), BF16)]),
        compiler_params=_cparams(("parallel", "arbitrary")),
        name="dsa_attention",
    )(qb_of, kb_of, rel_bias, qn, kn, proj3, mask, pos_q, pos_k)


def _merge_kernel(y0_ref, y1_ref, y2_ref, w_ref, g0_ref, g1_ref, g2_ref, o_ref):
    acc = None
    for n, (y_ref, g_ref) in enumerate(((y0_ref, g0_ref), (y1_ref, g1_ref), (y2_ref, g2_ref))):
        br = jnp.dot(y_ref[...], w_ref[n], preferred_element_type=F32)
        term = _sigmoid(g_ref[...].astype(F32)) * br
        acc = term if acc is None else acc + term
    o_ref[...] = acc.astype(o_ref.dtype)


def _branch_merge(ys, w_br, proj2, d_model, mix_w, gate_col0):
    t = proj2.shape[0]
    tm = _tile(t, 1024)
    tn = _tile(d_model, 512)
    yspec = pl.BlockSpec((tm, mix_w), lambda i, j: (i, 0))

    def gspec(n):
        base = (gate_col0 + n * d_model) // tn
        return pl.BlockSpec((tm, tn), lambda i, j, base=base: (i, base + j))

    return pl.pallas_call(
        _merge_kernel,
        out_shape=jax.ShapeDtypeStruct((t, d_model), BF16),
        grid=(t // tm, d_model // tn),
        in_specs=[yspec, yspec, yspec,
                  pl.BlockSpec((N_BRANCH, mix_w, tn), lambda i, j: (0, 0, j)),
                  gspec(0), gspec(1), gspec(2)],
        out_specs=pl.BlockSpec((tm, tn), lambda i, j: (i, j)),
        compiler_params=_cparams(("parallel", "parallel")),
        name="branch_merge",
    )(ys[0], ys[1], ys[2], w_br, proj2, proj2, proj2)


def _glu_kernel(a_ref, w1_ref, w3_ref, o_ref):
    a = a_ref[...]
    h1 = jnp.dot(a, w1_ref[...], preferred_element_type=F32)
    h3 = jnp.dot(a, w3_ref[...], preferred_element_type=F32)
    o_ref[...] = (h1 * _sigmoid(h1) * h3).astype(o_ref.dtype)


def _glu(a, w1, w3):
    t, d = a.shape
    ff = w1.shape[1]
    tm = _tile(t, 1024)
    tn = _tile(ff, 512)
    return pl.pallas_call(
        _glu_kernel,
        out_shape=jax.ShapeDtypeStruct((t, ff), BF16),
        grid=(t // tm, ff // tn),
        in_specs=[pl.BlockSpec((tm, d), lambda i, j: (i, 0)),
                  pl.BlockSpec((d, tn), lambda i, j: (0, j)),
                  pl.BlockSpec((d, tn), lambda i, j: (0, j))],
        out_specs=pl.BlockSpec((tm, tn), lambda i, j: (i, j)),
        compiler_params=_cparams(("parallel", "parallel")),
        name="swiglu_up",
    )(a, w1, w3)


def _router_kernel(x_ref, g_ref, r_ref, comb_ref, sel_ref, *, n_exp):
    x = x_ref[...]
    h = (x * lax.rsqrt(jnp.mean(x * x, axis=-1, keepdims=True) + EPS) * g_ref[...]).astype(BF16)
    logits = jnp.dot(h, r_ref[...], preferred_element_type=F32)
    lane = lax.broadcasted_iota(I32, logits.shape, 1)
    lg = jnp.where(lane < n_exp, logits, -jnp.inf)
    v1 = jnp.max(lg, axis=1, keepdims=True)
    i1 = jnp.min(jnp.where(lg == v1, lane, LANES), axis=1, keepdims=True)
    lg2 = jnp.where(lane == i1, -jnp.inf, lg)
    v2 = jnp.max(lg2, axis=1, keepdims=True)
    i2 = jnp.min(jnp.where(lg2 == v2, lane, LANES), axis=1, keepdims=True)
    e = jnp.exp(v2 - v1)
    w1 = 1.0 / (1.0 + e)
    w2 = e / (1.0 + e)
    comb_ref[...] = jnp.where(lane == i1, w1, jnp.where(lane == i2, w2, 0.0))
    sel_ref[...] = jnp.where((lane == i1) | (lane == i2), 1, 0)


def _router(x2, g, router_w):
    t, d = x2.shape
    n_exp = router_w.shape[1]
    tm = _tile(t, 512)
    rpad = jnp.zeros((d, LANES), BF16).at[:, :n_exp].set(router_w.astype(BF16))
    comb, sel = pl.pallas_call(
        functools.partial(_router_kernel, n_exp=n_exp),
        out_shape=(jax.ShapeDtypeStruct((t, LANES), F32), jax.ShapeDtypeStruct((t, LANES), I32)),
        grid=(t // tm,),
        in_specs=[pl.BlockSpec((tm, d), lambda i: (i, 0)),
                  pl.BlockSpec((1, d), lambda i: (0, 0)),
                  pl.BlockSpec((d, LANES), lambda i: (0, 0))],
        out_specs=(pl.BlockSpec((tm, LANES), lambda i: (i, 0)),
                   pl.BlockSpec((tm, LANES), lambda i: (i, 0))),
        compiler_params=_cparams(("parallel",)),
        name="moe_router",
    )(x2, g.reshape(1, d), rpad)
    return comb[:, :n_exp], sel[:, :n_exp]


def _row_gather(idx_ref, src_hbm, buf_ref, sem, slot, rows, start):
    def body(r8, carry):
        for u in range(ROW_DMA_UNROLL):
            r = r8 * ROW_DMA_UNROLL + u
            cp = pltpu.make_async_copy(src_hbm.at[pl.ds(idx_ref[0, 0, r], 1)],
                                       buf_ref.at[slot, pl.ds(r, 1)], sem.at[slot])
            if start:
                cp.start()
            else:
                cp.wait()
        return carry

    lax.fori_loop(0, rows // ROW_DMA_UNROLL, body, 0)


def _dispatch_kernel(cur_ref, nxt_ref, x_hbm, g_ref, o_ref, buf_ref, sem, *, rows, nsteps):
    i = pl.program_id(0)
    slot = i % 2

    @pl.when(i == 0)
    def _():
        _row_gather(cur_ref, x_hbm, buf_ref, sem, 0, rows, start=True)

    @pl.when(i + 1 < nsteps)
    def _():
        _row_gather(nxt_ref, x_hbm, buf_ref, sem, 1 - slot, rows, start=True)

    _row_gather(cur_ref, x_hbm, buf_ref, sem, slot, rows, start=False)
    x = buf_ref[slot]
    o_ref[...] = (x * lax.rsqrt(jnp.mean(x * x, axis=-1, keepdims=True) + EPS) * g_ref[...]).astype(o_ref.dtype)


def _dispatch(x2, g, src_tok, rows):
    t, d = x2.shape
    p = src_tok.shape[0]
    assert p % rows == 0 and rows % ROW_DMA_UNROLL == 0
    nsteps = p // rows
    src3 = src_tok.reshape(nsteps, 1, rows)
    return pl.pallas_call(
        functools.partial(_dispatch_kernel, rows=rows, nsteps=nsteps),
        out_shape=jax.ShapeDtypeStruct((p, d), BF16),
        grid=(nsteps,),
        in_specs=[pl.BlockSpec((1, 1, rows), lambda i: (i, 0, 0), memory_space=pltpu.SMEM),
                  pl.BlockSpec((1, 1, rows), lambda i: (jnp.minimum(i + 1, nsteps - 1), 0, 0),
                               memory_space=pltpu.SMEM),
                  pl.BlockSpec(memory_space=pl.ANY),
                  pl.BlockSpec((1, d), lambda i: (0, 0))],
        out_specs=pl.BlockSpec((rows, d), lambda i: (i, 0)),
        scratch_shapes=[pltpu.VMEM((2, rows, d), F32), pltpu.SemaphoreType.DMA((2,))],
        compiler_params=_cparams(("arbitrary",)),
        name="moe_dispatch",
    )(src3, src3, x2, g.reshape(1, d))


def _moe_up_kernel(te_ref, tv_ref, a_ref, w1_ref, w3_ref, o_ref):
    i = pl.program_id(0)

    @pl.when(tv_ref[i] == 1)
    def _():
        a = a_ref[...]
        h1 = jnp.dot(a, w1_ref[0], preferred_element_type=F32)
        h3 = jnp.dot(a, w3_ref[0], preferred_element_type=F32)
        o_ref[...] = (h1 * _sigmoid(h1) * h3).astype(o_ref.dtype)

    @pl.when(tv_ref[i] == 0)
    def _():
        o_ref[...] = jnp.zeros_like(o_ref)


def _moe_up(hs, w1, w3, tile_expert, tile_valid, tm):
    p, d = hs.shape
    ff = w1.shape[2]
    tn = _tile(ff, 512)
    nj = ff // tn

    def wmap(i, j, te, tv):
        return (te[i], 0, jnp.where(tv[i] == 1, j, nj - 1))

    return pl.pallas_call(
        _moe_up_kernel,
        out_shape=jax.ShapeDtypeStruct((p, ff), BF16),
        grid_spec=pltpu.PrefetchScalarGridSpec(
            num_scalar_prefetch=2,
            grid=(p // tm, nj),
            in_specs=[pl.BlockSpec((tm, d), lambda i, j, te, tv: (i, 0)),
                      pl.BlockSpec((1, d, tn), wmap),
                      pl.BlockSpec((1, d, tn), wmap)],
            out_specs=pl.BlockSpec((tm, tn), lambda i, j, te, tv: (i, j))),
        compiler_params=_cparams(("arbitrary", "arbitrary")),
        name="moe_up",
    )(tile_expert, tile_valid, hs, w1, w3)


def _moe_down_kernel(te_ref, tv_ref, a_ref, w_ref, o_ref):
    i = pl.program_id(0)

    @pl.when(tv_ref[i] == 1)
    def _():
        o_ref[...] = jnp.dot(a_ref[...], w_ref[0], preferred_element_type=F32)

    @pl.when(tv_ref[i] == 0)
    def _():
        o_ref[...] = jnp.zeros_like(o_ref)


def _moe_down(us, w2, tile_expert, tile_valid, tm):
    p, ff = us.shape
    d = w2.shape[2]
    tn = _tile(d, 512)
    nj = d // tn

    def wmap(i, j, te, tv):
        return (te[i], 0, jnp.where(tv[i] == 1, j, nj - 1))

    return pl.pallas_call(
        _moe_down_kernel,
        out_shape=jax.ShapeDtypeStruct((p, d), F32),
        grid_spec=pltpu.PrefetchScalarGridSpec(
            num_scalar_prefetch=2,
            grid=(p // tm, nj),
            in_specs=[pl.BlockSpec((tm, ff), lambda i, j, te, tv: (i, 0)),
                      pl.BlockSpec((1, ff, tn), wmap)],
            out_specs=pl.BlockSpec((tm, tn), lambda i, j, te, tv: (i, j))),
        compiler_params=_cparams(("arbitrary", "arbitrary")),
        name="moe_down",
    )(tile_expert, tile_valid, us, w2)


def _combine_kernel(p0_ref, p1_ref, n0_ref, n1_ref, x_ref, w0_ref, w1_ref, ys_hbm, o_ref,
                    a_ref, b_ref, sem_a, sem_b, *, rows, nsteps):
    i = pl.program_id(0)
    slot = i % 2

    @pl.when(i == 0)
    def _():
        _row_gather(p0_ref, ys_hbm, a_ref, sem_a, 0, rows, start=True)
        _row_gather(p1_ref, ys_hbm, b_ref, sem_b, 0, rows, start=True)

    @pl.when(i + 1 < nsteps)
    def _():
        _row_gather(n0_ref, ys_hbm, a_ref, sem_a, 1 - slot, rows, start=True)
        _row_gather(n1_ref, ys_hbm, b_ref, sem_b, 1 - slot, rows, start=True)

    _row_gather(p0_ref, ys_hbm, a_ref, sem_a, slot, rows, start=False)
    _row_gather(p1_ref, ys_hbm, b_ref, sem_b, slot, rows, start=False)
    o_ref[...] = x_ref[...] + (w0_ref[...] * a_ref[slot] + w1_ref[...] * b_ref[slot])


def _combine(x2, ys, pos0, pos1, w0, w1, rows):
    t, d = x2.shape
    assert t % rows == 0 and rows % ROW_DMA_UNROLL == 0
    nsteps = t // rows
    cur = pl.BlockSpec((1, 1, rows), lambda i: (i, 0, 0), memory_space=pltpu.SMEM)
    nxt = pl.BlockSpec((1, 1, rows), lambda i: (jnp.minimum(i + 1, nsteps - 1), 0, 0),
                       memory_space=pltpu.SMEM)
    p0 = pos0.reshape(nsteps, 1, rows)
    p1 = pos1.reshape(nsteps, 1, rows)
    return pl.pallas_call(
        functools.partial(_combine_kernel, rows=rows, nsteps=nsteps),
        out_shape=jax.ShapeDtypeStruct((t, d), F32),
        grid=(nsteps,),
        in_specs=[cur, cur, nxt, nxt,
                  pl.BlockSpec((rows, d), lambda i: (i, 0)),
                  pl.BlockSpec((rows, 1), lambda i: (i, 0)),
                  pl.BlockSpec((rows, 1), lambda i: (i, 0)),
                  pl.BlockSpec(memory_space=pl.ANY)],
        out_specs=pl.BlockSpec((rows, d), lambda i: (i, 0)),
        scratch_shapes=[pltpu.VMEM((2, rows, d), F32), pltpu.VMEM((2, rows, d), F32),
                        pltpu.SemaphoreType.DMA((2,)), pltpu.SemaphoreType.DMA((2,))],
        compiler_params=_cparams(("arbitrary",)),
        name="moe_combine",
    )(p0, p1, p0, p1, x2, w0, w1, ys)


def _moe(x2, g, router_w, w1, w3, w2):
    t, d = x2.shape
    n_exp = router_w.shape[1]
    tm = _tile(t, 512)
    comb, sel = _router(x2, g, router_w)
    rank = jnp.cumsum(sel, axis=0) - sel
    counts = jnp.sum(sel, axis=0)
    gsize = ((counts + tm - 1) // tm) * tm
    gend = jnp.cumsum(gsize)
    goff = gend - gsize
    dest = goff[None, :] + rank
    e0 = jnp.argmax(sel, axis=1)
    e1 = (n_exp - 1) - jnp.argmax(sel[:, ::-1], axis=1)
    pos0 = jnp.take_along_axis(dest, e0[:, None], axis=1)[:, 0].astype(I32)
    pos1 = jnp.take_along_axis(dest, e1[:, None], axis=1)[:, 0].astype(I32)
    wt0 = jnp.take_along_axis(comb, e0[:, None], axis=1)
    wt1 = jnp.take_along_axis(comb, e1[:, None], axis=1)
    p = t * TOP_K + n_exp * tm
    tok = jnp.arange(t, dtype=I32)
    src_tok = jnp.zeros((p,), I32).at[jnp.concatenate([pos0, pos1])].set(jnp.concatenate([tok, tok]))
    n_tiles = p // tm
    tstart = jnp.arange(n_tiles, dtype=I32) * tm
    tile_valid = (tstart < gend[-1]).astype(I32)
    te_raw = jnp.minimum(jnp.sum((tstart[:, None] >= gend[None, :]).astype(I32), axis=1), n_exp - 1)
    te_last = te_raw[jnp.maximum(gend[-1] // tm - 1, 0)]
    tile_expert = jnp.where(tile_valid == 1, te_raw, te_last).astype(I32)

    hs = _dispatch(x2, g, src_tok, rows=_tile(p, 256))
    us = _moe_up(hs, w1, w3, tile_expert, tile_valid, tm)
    ys = _moe_down(us, w2, tile_expert, tile_valid, tm)
    return _combine(x2, ys, pos0, pos1, wt0, wt1, rows=_tile(t, 256))


def _pack_w_in(w_in_l, d_model, mix_w):
    iq_w = IDX_HEADS * IDX_DH
    n_main = 10 * mix_w
    small = IDX_DH + IDX_HEADS
    gates0 = n_main + iq_w + small
    pad = iq_w - small
    w = w_in_l.astype(BF16)
    return jnp.concatenate([w[:, :n_main], w[:, gates0:gates0 + N_BRANCH * d_model],
                            w[:, n_main:n_main + iq_w], w[:, n_main + iq_w:gates0],
                            jnp.zeros((d_model, pad), BF16)], axis=1)


def _mixer(x2, b, s, positions, rel_bias, norm_g, w_in_l, w_br_l, w_o_l, gn_w, gn_b, conv_w, q_norm, k_norm):
    t, d_model = x2.shape
    mix_w = d_model // 2
    iq_w = IDX_HEADS * IDX_DH
    gate_col0 = 10 * mix_w
    iq_col0 = gate_col0 + N_BRANCH * d_model
    assert iq_col0 % iq_w == 0

    h = _rmsnorm(x2, norm_g)
    proj2 = _matmul(h, _pack_w_in(w_in_l, d_model, mix_w), BF16, 1024, 512, name="in_proj")
    proj3 = proj2.reshape(b, s, proj2.shape[1])
    pos_q = positions.reshape(b, s, 1)
    pos_k = positions.reshape(b, 1, s)

    y_ret = _retention_branch(proj3, pos_q, gn_w, gn_b, mix_w)
    y_conv = _conv_branch(proj3, conv_w, mix_w)

    ik0 = iq_col0 + iq_w
    kit = jnp.swapaxes(proj3[:, :, ik0:ik0 + IDX_DH], 1, 2)
    mask = _dsa_select(proj3, kit, s, min(TOPK_MAX, s // 4), iq_col0 // iq_w)
    qn, kn = _qk_norm(proj3, q_norm, k_norm, mix_w)
    y_att = _dsa_attention(qn, kn, proj3, mask, pos_q, pos_k, rel_bias, mix_w)

    ys = [y.reshape(t, mix_w) for y in (y_ret, y_conv, y_att)]
    merged = _branch_merge(ys, w_br_l.astype(BF16), proj2, d_model, mix_w, gate_col0)
    return _matmul(merged, w_o_l.astype(BF16), F32, 1024, 512, res=x2, name="out_proj")


def kernel(x, positions, rel_bias, norm_mix, norm_ffn, w_in, w_br, w_o, ret_gn_w, ret_gn_b, conv_w,
           q_norm, k_norm, ffn_w1, ffn_w3, ffn_w2, moe_router, moe_w1, moe_w3, moe_w2):
    b, s, d_model = x.shape
    depth = w_in.shape[0]
    x2 = x.reshape(b * s, d_model)
    for l in range(depth):
        x2 = _mixer(x2, b, s, positions, rel_bias, norm_mix[l], w_in[l], w_br[l], w_o[l],
                    ret_gn_w[l], ret_gn_b[l], conv_w[l], q_norm[l], k_norm[l])
        if l % 2 == 0:
            i = l // 2
            h = _rmsnorm(x2, norm_ffn[l])
            u = _glu(h, ffn_w1[i].astype(BF16), ffn_w3[i].astype(BF16))
            x2 = _matmul(u, ffn_w2[i].astype(BF16), F32, 512, 512, res=x2, name="ffn_down")
        else:
            i = l // 2
            x2 = _moe(x2, norm_ffn[l], moe_router[i], moe_w1[i].astype(BF16), moe_w3[i].astype(BF16),
                      moe_w2[i].astype(BF16))
    return x2.reshape(b, s, d_model)
```

```python
import functools
import math

import numpy as np
import jax
import jax.numpy as jnp
from jax import lax
from jax.experimental import pallas as pl
from jax.experimental.pallas import tpu as pltpu

F32 = jnp.float32
BF16 = jnp.bfloat16
I32 = jnp.int32

EPS = 1e-6
N_BRANCH = 3
RET_HEADS = 8
RET_CHUNK = 128
RET_ROPE_BASE = 10000.0
CONV_K = 3
ATT_HEADS = 8
IDX_HEADS = 8
IDX_DH = 64
TOPK_MAX = 256
REL_BUCKETS = 32
REL_MAX_DIST = 128
TOP_K = 2

LANES = 128
SUBLANES = 8
VMEM_LIMIT_BYTES = 56 * 1024 * 1024
ROW_DMA_UNROLL = 8

NEG_MASK = -1e30
LOG2E = math.log2(math.e)
INT_MIN = -2 ** 31
KEY_BITS = 32


def _cparams(sem):
    return pltpu.CompilerParams(dimension_semantics=sem, vmem_limit_bytes=VMEM_LIMIT_BYTES)


def _sigmoid(x):
    return 1.0 / (1.0 + jnp.exp(-x))


def _tile(n, pref):
    t = min(n, pref)
    assert n % t == 0, (n, pref)
    return t


def _rmsnorm_kernel(x_ref, g_ref, o_ref):
    x = x_ref[...]
    ms = jnp.mean(x * x, axis=-1, keepdims=True)
    o_ref[...] = (x * lax.rsqrt(ms + EPS) * g_ref[...]).astype(o_ref.dtype)


def _rmsnorm(x2, g):
    t, d = x2.shape
    tm = _tile(t, 512)
    return pl.pallas_call(
        _rmsnorm_kernel,
        out_shape=jax.ShapeDtypeStruct((t, d), BF16),
        grid=(t // tm,),
        in_specs=[pl.BlockSpec((tm, d), lambda i: (i, 0)),
                  pl.BlockSpec((1, d), lambda i: (0, 0))],
        out_specs=pl.BlockSpec((tm, d), lambda i: (i, 0)),
        compiler_params=_cparams(("parallel",)),
        name="rmsnorm",
    )(x2, g.reshape(1, d))


def _mm_kernel(a_ref, b_ref, o_ref):
    o_ref[...] = jnp.dot(a_ref[...], b_ref[...], preferred_element_type=F32).astype(o_ref.dtype)


def _mm_res_kernel(a_ref, b_ref, r_ref, o_ref):
    o_ref[...] = (r_ref[...] + jnp.dot(a_ref[...], b_ref[...], preferred_element_type=F32)).astype(o_ref.dtype)


def _matmul(a, b, out_dtype, tm, tn, res=None, name="matmul"):
    m, k = a.shape
    k2, n = b.shape
    assert k == k2
    tm = _tile(m, tm)
    tn = _tile(n, tn)
    in_specs = [pl.BlockSpec((tm, k), lambda i, j: (i, 0)),
                pl.BlockSpec((k, tn), lambda i, j: (0, j))]
    args = [a, b]
    kern = _mm_kernel
    if res is not None:
        in_specs.append(pl.BlockSpec((tm, tn), lambda i, j: (i, j)))
        args.append(res)
        kern = _mm_res_kernel
    return pl.pallas_call(
        kern,
        out_shape=jax.ShapeDtypeStruct((m, n), out_dtype),
        grid=(m // tm, n // tn),
        in_specs=in_specs,
        out_specs=pl.BlockSpec((tm, tn), lambda i, j: (i, j)),
        compiler_params=_cparams(("parallel", "parallel")),
        name=name,
    )(*args)


def _ret_kernel(pos_ref, invf_ref, q_ref, k_ref, v_ref, g_ref, dec_ref, kdec_ref, qdec_ref,
                gnw_ref, gnb_ref, o_ref, state_ref, *, heads, dh, chunk_decay):
    @pl.when(pl.program_id(1) == 0)
    def _():
        state_ref[...] = jnp.zeros_like(state_ref)

    c = q_ref.shape[1]
    ang = pos_ref[0].astype(F32) * invf_ref[...]
    cos = jnp.cos(ang)
    sin = jnp.sin(ang)
    lane = lax.broadcasted_iota(I32, (c, dh), 1)
    sin_signed = jnp.where(lane < dh // 2, -sin, sin)
    scale = dh ** -0.5
    for h in range(heads):
        sl = slice(h * dh, (h + 1) * dh)
        q = q_ref[0, :, sl].astype(F32)
        k = k_ref[0, :, sl].astype(F32)
        v = v_ref[0, :, sl]
        qr = (q * cos + pltpu.roll(q, dh // 2, 1) * sin_signed).astype(BF16)
        kr = (k * cos + pltpu.roll(k, dh // 2, 1) * sin_signed) * scale
        scores = lax.dot_general(qr, kr.astype(BF16), (((1,), (1,)), ((), ())),
                                 preferred_element_type=F32) * dec_ref[h]
        intra = jnp.dot(scores.astype(BF16), v, preferred_element_type=F32)
        prev = state_ref[h]
        cross = jnp.dot(qr, prev.astype(BF16), preferred_element_type=F32) * qdec_ref[h]
        kd = (kr * kdec_ref[h]).astype(BF16)
        kv = lax.dot_general(kd, v, (((0,), (0,)), ((), ())), preferred_element_type=F32)
        state_ref[h] = prev * chunk_decay[h] + kv
        ret = intra + cross
        mu = jnp.mean(ret, axis=-1, keepdims=True)
        cen = ret - mu
        var = jnp.mean(cen * cen, axis=-1, keepdims=True)
        gn = cen * lax.rsqrt(var + EPS) * gnw_ref[:, sl] + gnb_ref[:, sl]
        g = g_ref[0, :, sl].astype(F32)
        o_ref[0, :, sl] = (g * _sigmoid(g) * gn).astype(o_ref.dtype)


def _retention_branch(proj3, pos3, gn_w, gn_b, mix_w):
    b, s, _ = proj3.shape
    heads, c = RET_HEADS, RET_CHUNK
    dh = mix_w // heads
    assert dh == LANES and s % c == 0
    log_g = np.log1p(-np.exp2(-5.0 - np.arange(heads, dtype=np.float64)))
    pos = np.arange(c, dtype=np.float64)
    diff = pos[:, None] - pos[None, :]
    intra_decay = np.where(diff >= 0, np.exp(np.maximum(diff, 0.0)[None] * log_g[:, None, None]), 0.0)
    k_decay = np.exp((c - 1 - pos)[None, :] * log_g[:, None])
    q_decay = np.exp((pos + 1)[None, :] * log_g[:, None])
    chunk_decay = tuple(float(x) for x in np.exp(c * log_g))
    dec = jnp.asarray(intra_decay, F32)
    kdec = jnp.asarray(np.broadcast_to(k_decay[:, :, None], (heads, c, dh)), F32)
    qdec = jnp.asarray(np.broadcast_to(q_decay[:, :, None], (heads, c, dh)), F32)
    inv_freq = RET_ROPE_BASE ** (-jnp.arange(0, dh, 2, dtype=F32) / dh)
    invf = jnp.concatenate([inv_freq, inv_freq]).reshape(1, dh)

    def col(cb):
        return pl.BlockSpec((1, c, mix_w), lambda bi, n, cb=cb: (bi, n, cb))

    const3 = pl.BlockSpec((heads, c, dh), lambda bi, n: (0, 0, 0))
    return pl.pallas_call(
        functools.partial(_ret_kernel, heads=heads, dh=dh, chunk_decay=chunk_decay),
        out_shape=jax.ShapeDtypeStruct((b, s, mix_w), BF16),
        grid=(b, s // c),
        in_specs=[pl.BlockSpec((1, c, 1), lambda bi, n: (bi, n, 0)),
                  pl.BlockSpec((1, dh), lambda bi, n: (0, 0)),
                  col(0), col(1), col(2), col(3),
                  const3, const3, const3,
                  pl.BlockSpec((1, mix_w), lambda bi, n: (0, 0)),
                  pl.BlockSpec((1, mix_w), lambda bi, n: (0, 0))],
        out_specs=pl.BlockSpec((1, c, mix_w), lambda bi, n: (bi, n, 0)),
        scratch_shapes=[pltpu.VMEM((heads, dh, dh), F32)],
        compiler_params=_cparams(("arbitrary", "arbitrary")),
        name="retention",
    )(pos3, invf, proj3, proj3, proj3, proj3, dec, kdec, qdec,
      gn_w.reshape(1, mix_w), gn_b.reshape(1, mix_w))


def _conv_kernel(b_ref, c_ref, u_ref, w_ref, o_ref, ubuf_ref):
    ts = o_ref.shape[1]
    halo = SUBLANES

    @pl.when(pl.program_id(1) == 0)
    def _():
        ubuf_ref[0:halo, :] = jnp.zeros((halo, ubuf_ref.shape[1]), F32)

    u = c_ref[0].astype(F32) * u_ref[0].astype(F32)
    ubuf_ref[halo:halo + ts, :] = u
    u1 = ubuf_ref[halo - 1:halo - 1 + ts, :]
    u2 = ubuf_ref[halo - 2:halo - 2 + ts, :]
    conv = w_ref[0:1, :] * u2 + w_ref[1:2, :] * u1 + w_ref[2:3, :] * u
    o_ref[0] = (b_ref[0].astype(F32) * conv).astype(o_ref.dtype)
    ubuf_ref[0:halo, :] = ubuf_ref[ts:ts + halo, :]


def _conv_branch(proj3, conv_w, mix_w):
    b, s, _ = proj3.shape
    ts = _tile(s, 512)

    def col(cb):
        return pl.BlockSpec((1, ts, mix_w), lambda bi, n, cb=cb: (bi, n, cb))

    return pl.pallas_call(
        _conv_kernel,
        out_shape=jax.ShapeDtypeStruct((b, s, mix_w), BF16),
        grid=(b, s // ts),
        in_specs=[col(4), col(5), col(6),
                  pl.BlockSpec((CONV_K, mix_w), lambda bi, n: (0, 0))],
        out_specs=pl.BlockSpec((1, ts, mix_w), lambda bi, n: (bi, n, 0)),
        scratch_shapes=[pltpu.VMEM((ts + SUBLANES, mix_w), F32)],
        compiler_params=_cparams(("arbitrary", "arbitrary")),
        name="short_conv",
    )(proj3, proj3, proj3, conv_w)


def _qknorm_kernel(q_ref, k_ref, qn_ref, kn_ref, qo_ref, ko_ref, *, heads, dh, q_scale):
    for h in range(heads):
        sl = slice(h * dh, (h + 1) * dh)
        q = q_ref[0, :, sl].astype(F32)
        k = k_ref[0, :, sl].astype(F32)
        qy = q * lax.rsqrt(jnp.mean(q * q, axis=-1, keepdims=True) + EPS) * qn_ref[...]
        ky = k * lax.rsqrt(jnp.mean(k * k, axis=-1, keepdims=True) + EPS) * kn_ref[...]
        qo_ref[0, :, sl] = (qy * q_scale).astype(qo_ref.dtype)
        ko_ref[0, :, sl] = ky.astype(ko_ref.dtype)


def _qk_norm(proj3, q_norm, k_norm, mix_w):
    b, s, _ = proj3.shape
    heads = ATT_HEADS
    dh = mix_w // heads
    ts = _tile(s, 512)
    out = jax.ShapeDtypeStruct((b, s, mix_w), BF16)
    blk = pl.BlockSpec((1, ts, mix_w), lambda bi, n: (bi, n, 0))
    return pl.pallas_call(
        functools.partial(_qknorm_kernel, heads=heads, dh=dh, q_scale=dh ** -0.5 * LOG2E),
        out_shape=(out, out),
        grid=(b, s // ts),
        in_specs=[pl.BlockSpec((1, ts, mix_w), lambda bi, n: (bi, n, 7)),
                  pl.BlockSpec((1, ts, mix_w), lambda bi, n: (bi, n, 8)),
                  pl.BlockSpec((1, dh), lambda bi, n: (0, 0)),
                  pl.BlockSpec((1, dh), lambda bi, n: (0, 0))],
        out_specs=(blk, blk),
        compiler_params=_cparams(("parallel", "parallel")),
        name="qk_norm",
    )(proj3, proj3, q_norm.reshape(1, dh), k_norm.reshape(1, dh))


def _select_kernel(iq_ref, sm_ref, kit_ref, tri_ref, o_ref, qall_ref, keys_ref, planes_ref, eq_ref, *,
                   tq, ck, ksel, heads, d_idx):
    qb = pl.program_id(1)
    row0 = qb * tq
    nch = (row0 + tq + ck - 1) // ck
    grp = KEY_BITS * LANES
    ngrp = keys_ref.shape[1] // grp
    nlive = (row0 + tq + grp - 1) // grp
    for h in range(heads):
        qall_ref[h * tq:(h + 1) * tq, :] = iq_ref[0, :, h * d_idx:(h + 1) * d_idx]
    w = sm_ref[0, :, d_idx:d_idx + heads].astype(F32) * (heads ** -0.5 * d_idx ** -0.5)
    t_col = row0 + lax.broadcasted_iota(I32, (tq, 1), 0)
    lane_ck = lax.broadcasted_iota(I32, (1, ck), 1)

    def score_chunk(c, carry):
        off = pl.multiple_of(c * ck, ck)
        kt = kit_ref[0, :, pl.ds(off, ck)]
        sc = jnp.zeros((tq, ck), F32)
        for h in range(heads):
            d = jnp.dot(qall_ref[h * tq:(h + 1) * tq, :], kt, preferred_element_type=F32)
            sc = sc + w[:, h:h + 1] * jnp.maximum(d, 0.0)
        bits = pltpu.bitcast(sc, I32)
        key = bits ^ ((bits >> 31) & 0x7FFFFFFF)
        keys_ref[:, pl.ds(off, ck)] = jnp.where(off + lane_ck <= t_col, key, INT_MIN)
        return carry

    def clear_chunk(c, carry):
        keys_ref[:, pl.ds(pl.multiple_of(c * ck, ck), ck)] = jnp.full((tq, ck), INT_MIN, I32)
        return carry

    lax.fori_loop(0, nch, score_chunk, 0)
    lax.fori_loop(nch, nlive * (grp // ck), clear_chunk, 0)

    def transpose_group(g):
        def rows(r, carry):
            rs = pl.ds(pl.multiple_of(r * SUBLANES, SUBLANES), SUBLANES)
            a = [keys_ref[rs, (g * KEY_BITS + j) * LANES:(g * KEY_BITS + j + 1) * LANES] ^ INT_MIN
                 for j in range(KEY_BITS)]
            j, m = KEY_BITS // 2, (1 << (KEY_BITS // 2)) - 1
            while j:
                k = 0
                while k < KEY_BITS:
                    t = (a[k] ^ (a[k + j] >> j)) & m
                    a[k] = a[k] ^ t
                    a[k + j] = a[k + j] ^ (t << j)
                    k = (k + j + 1) & ~j
                j >>= 1
                m = m ^ (m << j)
            for p in range(KEY_BITS):
                planes_ref[p, g, rs, :] = a[p]
            return carry

        lax.fori_loop(0, tq // SUBLANES, rows, 0)

    @pl.when(qb == 0)
    def _():
        planes_ref[...] = jnp.zeros_like(planes_ref)

    for g in range(ngrp):
        @pl.when(g < nlive)
        def _(g=g):
            transpose_group(g)
            eq_ref[g] = jnp.full(eq_ref.shape[1:], -1, I32)

        @pl.when(g >= nlive)
        def _(g=g):
            eq_ref[g] = jnp.zeros(eq_ref.shape[1:], I32)

    kf = float(ksel)

    def bit_pass(p, carry):
        above, thr_u = carry
        ones = jnp.zeros((tq, LANES), I32)
        for g in range(ngrp):
            ones = ones + lax.population_count(eq_ref[g] & planes_ref[p, g])
        c1 = jnp.sum(ones.astype(F32), axis=1, keepdims=True)
        take = (above + c1) >= kf
        for g in range(ngrp):
            e = eq_ref[g]
            x = e & planes_ref[p, g]
            eq_ref[g] = jnp.where(take, x, e ^ x)
        bit = jnp.left_shift(jnp.int32(1), KEY_BITS - 1 - p)
        return jnp.where(take, above, above + c1), jnp.where(take, thr_u | bit, thr_u)

    cnt_gt, thr_u = lax.fori_loop(0, KEY_BITS, bit_pass,
                                  (jnp.zeros((tq, 1), F32), jnp.zeros((tq, 1), I32)))
    thr = thr_u ^ INT_MIN
    eq_cnt = jnp.zeros((tq, LANES), I32)
    for g in range(ngrp):
        eq_cnt = eq_cnt + lax.population_count(eq_ref[g])
    cnt_eq = jnp.sum(eq_cnt.astype(F32), axis=1, keepdims=True)

    need = kf - cnt_gt
    excess = jnp.where(thr == INT_MIN, 0.0, cnt_eq - need)
    o_ref[...] = jnp.full(o_ref.shape, NEG_MASK, o_ref.dtype)

    def write_chunk(c, eq_before, ties):
        off = pl.multiple_of(c * ck, ck)
        sl = pl.ds(off, ck)
        k = keys_ref[:, sl]
        if ties:
            eq = jnp.where(k == thr, 1.0, 0.0).astype(BF16)
            rank = eq_before + jnp.dot(eq, tri_ref[...], preferred_element_type=F32)
            on_eq = jnp.where(rank <= need, 0.0, NEG_MASK)
            eq_before = rank[:, ck - 1:ck]
        else:
            on_eq = 0.0
        val = jnp.where(k > thr, 0.0, jnp.where(k == thr, on_eq, NEG_MASK))
        o_ref[0, :, sl] = jnp.where(off + lane_ck <= t_col, val, NEG_MASK).astype(o_ref.dtype)
        return eq_before

    any_excess = jnp.max(excess) > 0.0

    @pl.when(any_excess)
    def _():
        lax.fori_loop(0, nch, lambda c, e: write_chunk(c, e, True), jnp.zeros((tq, 1), F32))

    @pl.when(jnp.logical_not(any_excess))
    def _():
        lax.fori_loop(0, nch, lambda c, e: write_chunk(c, e, False), 0)


def _dsa_select(proj3, kit, s, ksel, iq_block):
    b = proj3.shape[0]
    tq = _tile(s, 256)
    ck = _tile(s, 512)
    heads, d_idx = IDX_HEADS, IDX_DH
    iq_w = heads * d_idx
    assert (s % (KEY_BITS * LANES) == 0 or s < KEY_BITS * LANES) and (KEY_BITS * LANES) % ck == 0
    sp = max(s, KEY_BITS * LANES)
    tri = (jnp.arange(ck)[:, None] <= jnp.arange(ck)[None, :]).astype(BF16)
    return pl.pallas_call(
        functools.partial(_select_kernel, tq=tq, ck=ck, ksel=ksel, heads=heads, d_idx=d_idx),
        out_shape=jax.ShapeDtypeStruct((b, s, s), BF16),
        grid=(b, s // tq),
        in_specs=[pl.BlockSpec((1, tq, iq_w), lambda bi, n: (bi, n, iq_block)),
                  pl.BlockSpec((1, tq, iq_w), lambda bi, n: (bi, n, iq_block + 1)),
                  pl.BlockSpec((1, d_idx, s), lambda bi, n: (bi, 0, 0)),
                  pl.BlockSpec((ck, ck), lambda bi, n: (0, 0))],
        out_specs=pl.BlockSpec((1, tq, s), lambda bi, n: (bi, n, 0)),
        scratch_shapes=[pltpu.VMEM((heads * tq, d_idx), BF16),
                        pltpu.VMEM((tq, sp), I32),
                        pltpu.VMEM((KEY_BITS, sp // (KEY_BITS * LANES), tq, LANES), I32),
                        pltpu.VMEM((sp // (KEY_BITS * LANES), tq, LANES), I32)],
        compiler_params=_cparams(("arbitrary", "arbitrary")),
        name="dsa_select",
    )(proj3, proj3, kit, tri)


def _t5_bucket(n):
    max_exact = REL_BUCKETS // 2
    nf = jnp.maximum(n, 1).astype(F32)
    large = max_exact + (jnp.log(nf / max_exact) / math.log(REL_MAX_DIST / max_exact)
                         * (REL_BUCKETS - max_exact)).astype(I32)
    large = jnp.minimum(large, REL_BUCKETS - 1)
    return jnp.where(n < max_exact, n, large)


def _attn_kernel(qb_ref, kb_ref, rb_ref, q_ref, k_ref, v_ref, mask_ref, pq_ref, pk_ref, o_ref,
                 acc_ref, m_ref, l_ref, bias_ref, s_ref, p_ref, *, tq, tk, heads, dh):
    qb = qb_ref[pl.program_id(1)]
    kb = kb_ref[pl.program_id(1)]
    kmax = ((qb + 1) * tq - 1) // tk
    nlt = tk // LANES
    gran = LANES

    @pl.when(kb == 0)
    def _():
        acc_ref[...] = jnp.zeros_like(acc_ref)
        m_ref[...] = jnp.full(m_ref.shape, -jnp.inf, F32)
        l_ref[...] = jnp.zeros_like(l_ref)

    def attend(bias_of):
        ones = jnp.ones((tk, LANES), BF16)
        for h in range(heads):
            sl = slice(h * dh, (h + 1) * dh)
            kh = k_ref[0, :, sl]
            v_aug = jnp.concatenate([v_ref[0, :, sl], ones], axis=1)
            add, const = bias_of(h)
            buf = h % 2
            s = lax.dot_general(q_ref[0, :, sl], kh, (((1,), (1,)), ((), ())),
                                preferred_element_type=F32) + add
            s_ref[buf] = s
            tile_max = s[:, 0:LANES]
            for c in range(1, nlt):
                tile_max = jnp.maximum(tile_max, s[:, c * LANES:(c + 1) * LANES])
            m_cur = jnp.broadcast_to(jnp.max(tile_max, axis=1, keepdims=True), (tq, LANES)) + const
            m_old = m_ref[h]
            m_new = jnp.maximum(m_old, m_cur)
            alpha = jnp.exp2(m_old - m_new)
            shift = m_new - const
            for c in range(nlt):
                cs = slice(c * LANES, (c + 1) * LANES)
                p_ref[buf, :, cs] = jnp.exp2(s_ref[buf, :, cs] - shift).astype(BF16)
            pv = jnp.dot(p_ref[buf], v_aug, preferred_element_type=F32)
            m_ref[h] = m_new
            l_ref[h] = alpha * l_ref[h] + pv[:, dh:]
            acc_ref[:, sl] = alpha * acc_ref[:, sl] + pv[:, :dh]

    @pl.when(kb <= kmax)
    def _():
        pq = pq_ref[0]
        pk = pk_ref[0]
        all_far = (jnp.min(pq) - jnp.max(pk)) >= REL_MAX_DIST

        @pl.when(all_far)
        def _():
            attend(lambda h: (mask_ref[0].astype(F32), rb_ref[REL_BUCKETS - 1, h] * LOG2E))

        @pl.when(jnp.logical_not(all_far))
        def _():
            n_lane = lax.broadcasted_iota(I32, (1, REL_MAX_DIST), 1)
            bucket = _t5_bucket(n_lane)
            tabs = []
            for h in range(heads):
                tab = jnp.zeros((1, REL_MAX_DIST), F32)
                for j in range(REL_BUCKETS):
                    tab = jnp.where(bucket == j, rb_ref[j, h] * LOG2E, tab)
                tabs.append(jnp.broadcast_to(tab, (gran, REL_MAX_DIST)))
            k_lo = [jnp.min(pk[:, cj * gran:(cj + 1) * gran]) for cj in range(tk // gran)]
            k_hi = [jnp.max(pk[:, cj * gran:(cj + 1) * gran]) for cj in range(tk // gran)]
            for ri in range(tq // gran):
                rs = slice(ri * gran, (ri + 1) * gran)
                pq_g = pq[rs]
                q_lo = jnp.min(pq_g)
                q_hi = jnp.max(pq_g)
                for cj in range(tk // gran):
                    cs = slice(cj * gran, (cj + 1) * gran)
                    pk_g = pk[:, cs]
                    lo = q_lo - k_hi[cj]
                    hi = q_hi - k_lo[cj]
                    is_far = lo >= REL_MAX_DIST
                    is_zero = hi <= 0
                    maskf = mask_ref[0, rs, cs].astype(F32)

                    @pl.when(is_far | is_zero)
                    def _():
                        for h in range(heads):
                            c_h = jnp.where(is_far, rb_ref[REL_BUCKETS - 1, h], rb_ref[0, h]) * LOG2E
                            bias_ref[h, rs, cs] = maskf + c_h

                    @pl.when(jnp.logical_not(is_far | is_zero))
                    def _():
                        dist = jnp.clip(pq_g - pk_g, 0, REL_MAX_DIST - 1)
                        for h in range(heads):
                            bias_ref[h, rs, cs] = maskf + jnp.take_along_axis(tabs[h], dist, axis=1)
            attend(lambda h: (bias_ref[h], 0.0))

    @pl.when(kb == kmax)
    def _():
        for h in range(heads):
            sl = slice(h * dh, (h + 1) * dh)
            o_ref[0, :, sl] = (acc_ref[:, sl] / l_ref[h]).astype(o_ref.dtype)


def _dsa_attention(qn, kn, proj3, mask, pos_q, pos_k, rel_bias, mix_w):
    b, s, _ = qn.shape
    heads = ATT_HEADS
    dh = mix_w // heads
    tq = _tile(s, 512)
    tk = tq
    assert REL_MAX_DIST == LANES

    tiles = [(qb, kb) for qb in range(s // tq) for kb in range(((qb + 1) * tq - 1) // tk + 1)]
    qb_of = jnp.asarray([t[0] for t in tiles], I32)
    kb_of = jnp.asarray([t[1] for t in tiles], I32)

    return pl.pallas_call(
        functools.partial(_attn_kernel, tq=tq, tk=tk, heads=heads, dh=dh),
        out_shape=jax.ShapeDtypeStruct((b, s, mix_w), BF16),
        grid_spec=pltpu.PrefetchScalarGridSpec(
            num_scalar_prefetch=2,
            grid=(b, len(tiles)),
            in_specs=[pl.BlockSpec(memory_space=pltpu.SMEM),
                      pl.BlockSpec((1, tq, mix_w), lambda bi, st, qo, ko: (bi, qo[st], 0)),
                      pl.BlockSpec((1, tk, mix_w), lambda bi, st, qo, ko: (bi, ko[st], 0)),
                      pl.BlockSpec((1, tk, mix_w), lambda bi, st, qo, ko: (bi, ko[st], 9)),
                      pl.BlockSpec((1, tq, tk), lambda bi, st, qo, ko: (bi, qo[st], ko[st])),
                      pl.BlockSpec((1, tq, 1), lambda bi, st, qo, ko: (bi, qo[st], 0)),
                      pl.BlockSpec((1, 1, tk), lambda bi, st, qo, ko: (bi, 0, ko[st]))],
            out_specs=pl.BlockSpec((1, tq, mix_w), lambda bi, st, qo, ko: (bi, qo[st], 0)),
            scratch_shapes=[pltpu.VMEM((tq, mix_w), F32),
                            pltpu.VMEM((heads, tq, LANES), F32),
                            pltpu.VMEM((heads, tq, LANES), F32),
                            pltpu.VMEM((heads, tq, tk), F32),
                            pltpu.VMEM((2, tq, tk), F32),
                            pltpu.VMEM((2, tq, tk), BF16)]),
        compiler_params=_cparams(("parallel", "arbitrary")),
        name="dsa_attention",
    )(qb_of, kb_of, rel_bias, qn, kn, proj3, mask, pos_q, pos_k)


def _merge_kernel(y0_ref, y1_ref, y2_ref, w_ref, g0_ref, g1_ref, g2_ref, o_ref):
    acc = None
    for n, (y_ref, g_ref) in enumerate(((y0_ref, g0_ref), (y1_ref, g1_ref), (y2_ref, g2_ref))):
        br = jnp.dot(y_ref[...], w_ref[n], preferred_element_type=F32)
        term = _sigmoid(g_ref[...].astype(F32)) * br
        acc = term if acc is None else acc + term
    o_ref[...] = acc.astype(o_ref.dtype)


def _branch_merge(ys, w_br, proj2, d_model, mix_w, gate_col0):
    t = proj2.shape[0]
    tm = _tile(t, 1024)
    tn = _tile(d_model, 512)
    yspec = pl.BlockSpec((tm, mix_w), lambda i, j: (i, 0))

    def gspec(n):
        base = (gate_col0 + n * d_model) // tn
        return pl.BlockSpec((tm, tn), lambda i, j, base=base: (i, base + j))

    return pl.pallas_call(
        _merge_kernel,
        out_shape=jax.ShapeDtypeStruct((t, d_model), BF16),
        grid=(t // tm, d_model // tn),
        in_specs=[yspec, yspec, yspec,
                  pl.BlockSpec((N_BRANCH, mix_w, tn), lambda i, j: (0, 0, j)),
                  gspec(0), gspec(1), gspec(2)],
        out_specs=pl.BlockSpec((tm, tn), lambda i, j: (i, j)),
        compiler_params=_cparams(("parallel", "parallel")),
        name="branch_merge",
    )(ys[0], ys[1], ys[2], w_br, proj2, proj2, proj2)


def _glu_kernel(a_ref, w1_ref, w3_ref, o_ref):
    a = a_ref[...]
    h1 = jnp.dot(a, w1_ref[...], preferred_element_type=F32)
    h3 = jnp.dot(a, w3_ref[...], preferred_element_type=F32)
    o_ref[...] = (h1 * _sigmoid(h1) * h3).astype(o_ref.dtype)


def _glu(a, w1, w3):
    t, d = a.shape
    ff = w1.shape[1]
    tm = _tile(t, 1024)
    tn = _tile(ff, 512)
    return pl.pallas_call(
        _glu_kernel,
        out_shape=jax.ShapeDtypeStruct((t, ff), BF16),
        grid=(t // tm, ff // tn),
        in_specs=[pl.BlockSpec((tm, d), lambda i, j: (i, 0)),
                  pl.BlockSpec((d, tn), lambda i, j: (0, j)),
                  pl.BlockSpec((d, tn), lambda i, j: (0, j))],
        out_specs=pl.BlockSpec((tm, tn), lambda i, j: (i, j)),
        compiler_params=_cparams(("parallel", "parallel")),
        name="swiglu_up",
    )(a, w1, w3)


def _router_kernel(x_ref, g_ref, r_ref, comb_ref, sel_ref, *, n_exp):
    x = x_ref[...]
    h = (x * lax.rsqrt(jnp.mean(x * x, axis=-1, keepdims=True) + EPS) * g_ref[...]).astype(BF16)
    logits = jnp.dot(h, r_ref[...], preferred_element_type=F32)
    lane = lax.broadcasted_iota(I32, logits.shape, 1)
    lg = jnp.where(lane < n_exp, logits, -jnp.inf)
    v1 = jnp.max(lg, axis=1, keepdims=True)
    i1 = jnp.min(jnp.where(lg == v1, lane, LANES), axis=1, keepdims=True)
    lg2 = jnp.where(lane == i1, -jnp.inf, lg)
    v2 = jnp.max(lg2, axis=1, keepdims=True)
    i2 = jnp.min(jnp.where(lg2 == v2, lane, LANES), axis=1, keepdims=True)
    e = jnp.exp(v2 - v1)
    w1 = 1.0 / (1.0 + e)
    w2 = e / (1.0 + e)
    comb_ref[...] = jnp.where(lane == i1, w1, jnp.where(lane == i2, w2, 0.0))
    sel_ref[...] = jnp.where((lane == i1) | (lane == i2), 1, 0)


def _router(x2, g, router_w):
    t, d = x2.shape
    n_exp = router_w.shape[1]
    tm = _tile(t, 512)
    rpad = jnp.zeros((d, LANES), BF16).at[:, :n_exp].set(router_w.astype(BF16))
    comb, sel = pl.pallas_call(
        functools.partial(_router_kernel, n_exp=n_exp),
        out_shape=(jax.ShapeDtypeStruct((t, LANES), F32), jax.ShapeDtypeStruct((t, LANES), I32)),
        grid=(t // tm,),
        in_specs=[pl.BlockSpec((tm, d), lambda i: (i, 0)),
                  pl.BlockSpec((1, d), lambda i: (0, 0)),
                  pl.BlockSpec((d, LANES), lambda i: (0, 0))],
        out_specs=(pl.BlockSpec((tm, LANES), lambda i: (i, 0)),
                   pl.BlockSpec((tm, LANES), lambda i: (i, 0))),
        compiler_params=_cparams(("parallel",)),
        name="moe_router",
    )(x2, g.reshape(1, d), rpad)
    return comb[:, :n_exp], sel[:, :n_exp]


def _row_gather(idx_ref, src_hbm, buf_ref, sem, slot, rows, start):
    def body(r8, carry):
        for u in range(ROW_DMA_UNROLL):
            r = r8 * ROW_DMA_UNROLL + u
            cp = pltpu.make_async_copy(src_hbm.at[pl.ds(idx_ref[0, 0, r], 1)],
                                       buf_ref.at[slot, pl.ds(r, 1)], sem.at[slot])
            if start:
                cp.start(priority=u % 2)
            else:
                cp.wait()
        return carry

    lax.fori_loop(0, rows // ROW_DMA_UNROLL, body, 0)


def _dispatch_kernel(cur_ref, nxt_ref, x_hbm, g_ref, o_ref, buf_ref, sem, *, rows, nsteps):
    i = pl.program_id(0)
    slot = i % 2

    @pl.when(i == 0)
    def _():
        _row_gather(cur_ref, x_hbm, buf_ref, sem, 0, rows, start=True)

    @pl.when(i + 1 < nsteps)
    def _():
        _row_gather(nxt_ref, x_hbm, buf_ref, sem, 1 - slot, rows, start=True)

    _row_gather(cur_ref, x_hbm, buf_ref, sem, slot, rows, start=False)
    x = buf_ref[slot]
    o_ref[...] = (x * lax.rsqrt(jnp.mean(x * x, axis=-1, keepdims=True) + EPS) * g_ref[...]).astype(o_ref.dtype)


def _dispatch(x2, g, src_tok, rows):
    t, d = x2.shape
    p = src_tok.shape[0]
    assert p % rows == 0 and rows % ROW_DMA_UNROLL == 0
    nsteps = p // rows
    src3 = src_tok.reshape(nsteps, 1, rows)
    return pl.pallas_call(
        functools.partial(_dispatch_kernel, rows=rows, nsteps=nsteps),
        out_shape=jax.ShapeDtypeStruct((p, d), BF16),
        grid=(nsteps,),
        in_specs=[pl.BlockSpec((1, 1, rows), lambda i: (i, 0, 0), memory_space=pltpu.SMEM),
                  pl.BlockSpec((1, 1, rows), lambda i: (jnp.minimum(i + 1, nsteps - 1), 0, 0),
                               memory_space=pltpu.SMEM),
                  pl.BlockSpec(memory_space=pl.ANY),
                  pl.BlockSpec((1, d), lambda i: (0, 0))],
        out_specs=pl.BlockSpec((rows, d), lambda i: (i, 0)),
        scratch_shapes=[pltpu.VMEM((2, rows, d), F32), pltpu.SemaphoreType.DMA((2,))],
        compiler_params=_cparams(("arbitrary",)),
        name="moe_dispatch",
    )(src3, src3, x2, g.reshape(1, d))


def _moe_up_kernel(ti_ref, tj_ref, wj_ref, te_ref, first_ref, valid_ref, a_ref, w1_ref, w3_ref, o_ref,
                   w1b_ref, w3b_ref):
    s = pl.program_id(0)

    @pl.when(first_ref[s] == 1)
    def _():
        w1b_ref[...] = w1_ref[0].astype(BF16)
        w3b_ref[...] = w3_ref[0].astype(BF16)

    @pl.when(valid_ref[s] == 1)
    def _():
        a = a_ref[...]
        h1 = jnp.dot(a, w1b_ref[...], preferred_element_type=F32)
        h3 = jnp.dot(a, w3b_ref[...], preferred_element_type=F32)
        o_ref[...] = (h1 * _sigmoid(h1) * h3).astype(o_ref.dtype)

    @pl.when(valid_ref[s] == 0)
    def _():
        o_ref[...] = jnp.zeros_like(o_ref)


def _moe_up(hs, w1, w3, tile_expert, tile_valid, gfirst, gtiles, tm):
    p, d = hs.shape
    ff = w1.shape[2]
    tn = _tile(ff, 512)
    nj = ff // tn
    n_tiles = p // tm
    step = jnp.arange(n_tiles * nj, dtype=I32)
    t0 = step // nj
    e_s = tile_expert[t0]
    valid = tile_valid[t0]
    n_run = jnp.maximum(gtiles[e_s], 1)
    local = step - gfirst[e_s] * nj
    ti = jnp.where(valid == 1, gfirst[e_s] + local % n_run, t0).astype(I32)
    tj = jnp.where(valid == 1, local // n_run, step % nj).astype(I32)
    wj = jnp.where(valid == 1, local // n_run, nj - 1).astype(I32)
    first = jnp.where((valid == 1) & (local % n_run == 0), 1, 0).astype(I32)

    def amap(s, ti_r, tj_r, wj_r, te_r, f_r, v_r):
        return (ti_r[s], 0)

    def wmap(s, ti_r, tj_r, wj_r, te_r, f_r, v_r):
        return (te_r[s], 0, wj_r[s])

    def omap(s, ti_r, tj_r, wj_r, te_r, f_r, v_r):
        return (ti_r[s], tj_r[s])

    return pl.pallas_call(
        _moe_up_kernel,
        out_shape=jax.ShapeDtypeStruct((p, ff), BF16),
        grid_spec=pltpu.PrefetchScalarGridSpec(
            num_scalar_prefetch=6,
            grid=(n_tiles * nj,),
            in_specs=[pl.BlockSpec((tm, d), amap),
                      pl.BlockSpec((1, d, tn), wmap),
                      pl.BlockSpec((1, d, tn), wmap)],
            out_specs=pl.BlockSpec((tm, tn), omap),
            scratch_shapes=[pltpu.VMEM((d, tn), BF16), pltpu.VMEM((d, tn), BF16)]),
        compiler_params=_cparams(("arbitrary",)),
        name="moe_up",
    )(ti, tj, wj, e_s.astype(I32), first, valid.astype(I32), hs, w1, w3)


def _moe_down_kernel(te_ref, tv_ref, a_ref, w_ref, o_ref):
    i = pl.program_id(0)

    @pl.when(tv_ref[i] == 1)
    def _():
        o_ref[...] = jnp.dot(a_ref[...], w_ref[0], preferred_element_type=F32)

    @pl.when(tv_ref[i] == 0)
    def _():
        o_ref[...] = jnp.zeros_like(o_ref)


def _moe_down(us, w2, tile_expert, tile_valid, tm):
    p, ff = us.shape
    d = w2.shape[2]
    tn = _tile(d, 512)
    nj = d // tn

    def wmap(i, j, te, tv):
        return (te[i], 0, jnp.where(tv[i] == 1, j, nj - 1))

    return pl.pallas_call(
        _moe_down_kernel,
        out_shape=jax.ShapeDtypeStruct((p, d), F32),
        grid_spec=pltpu.PrefetchScalarGridSpec(
            num_scalar_prefetch=2,
            grid=(p // tm, nj),
            in_specs=[pl.BlockSpec((tm, ff), lambda i, j, te, tv: (i, 0)),
                      pl.BlockSpec((1, ff, tn), wmap)],
            out_specs=pl.BlockSpec((tm, tn), lambda i, j, te, tv: (i, j))),
        compiler_params=_cparams(("arbitrary", "arbitrary")),
        name="moe_down",
    )(tile_expert, tile_valid, us, w2)


def _combine_kernel(p0_ref, p1_ref, n0_ref, n1_ref, x_ref, w0_ref, w1_ref, ys_hbm, o_ref,
                    a_ref, b_ref, sem_a, sem_b, *, rows, nsteps):
    i = pl.program_id(0)
    slot = i % 2

    @pl.when(i == 0)
    def _():
        _row_gather(p0_ref, ys_hbm, a_ref, sem_a, 0, rows, start=True)
        _row_gather(p1_ref, ys_hbm, b_ref, sem_b, 0, rows, start=True)

    @pl.when(i + 1 < nsteps)
    def _():
        _row_gather(n0_ref, ys_hbm, a_ref, sem_a, 1 - slot, rows, start=True)
        _row_gather(n1_ref, ys_hbm, b_ref, sem_b, 1 - slot, rows, start=True)

    _row_gather(p0_ref, ys_hbm, a_ref, sem_a, slot, rows, start=False)
    _row_gather(p1_ref, ys_hbm, b_ref, sem_b, slot, rows, start=False)
    o_ref[...] = x_ref[...] + (w0_ref[...] * a_ref[slot] + w1_ref[...] * b_ref[slot])


def _combine(x2, ys, pos0, pos1, w0, w1, rows):
    t, d = x2.shape
    assert t % rows == 0 and rows % ROW_DMA_UNROLL == 0
    nsteps = t // rows
    cur = pl.BlockSpec((1, 1, rows), lambda i: (i, 0, 0), memory_space=pltpu.SMEM)
    nxt = pl.BlockSpec((1, 1, rows), lambda i: (jnp.minimum(i + 1, nsteps - 1), 0, 0),
                       memory_space=pltpu.SMEM)
    p0 = pos0.reshape(nsteps, 1, rows)
    p1 = pos1.reshape(nsteps, 1, rows)
    return pl.pallas_call(
        functools.partial(_combine_kernel, rows=rows, nsteps=nsteps),
        out_shape=jax.ShapeDtypeStruct((t, d), F32),
        grid=(nsteps,),
        in_specs=[cur, cur, nxt, nxt,
                  pl.BlockSpec((rows, d), lambda i: (i, 0)),
                  pl.BlockSpec((rows, 1), lambda i: (i, 0)),
                  pl.BlockSpec((rows, 1), lambda i: (i, 0)),
                  pl.BlockSpec(memory_space=pl.ANY)],
        out_specs=pl.BlockSpec((rows, d), lambda i: (i, 0)),
        scratch_shapes=[pltpu.VMEM((2, rows, d), F32), pltpu.VMEM((2, rows, d), F32),
                        pltpu.SemaphoreType.DMA((2,)), pltpu.SemaphoreType.DMA((2,))],
        compiler_params=_cparams(("arbitrary",)),
        name="moe_combine",
    )(p0, p1, p0, p1, x2, w0, w1, ys)


def _moe(x2, g, router_w, w1, w3, w2):
    t, d = x2.shape
    n_exp = router_w.shape[1]
    tm = _tile(t, 512)
    comb, sel = _router(x2, g, router_w)
    rank = jnp.cumsum(sel, axis=0) - sel
    counts = jnp.sum(sel, axis=0)
    gsize = ((counts + tm - 1) // tm) * tm
    gend = jnp.cumsum(gsize)
    goff = gend - gsize
    dest = goff[None, :] + rank
    e0 = jnp.argmax(sel, axis=1)
    e1 = (n_exp - 1) - jnp.argmax(sel[:, ::-1], axis=1)
    pos0 = jnp.take_along_axis(dest, e0[:, None], axis=1)[:, 0].astype(I32)
    pos1 = jnp.take_along_axis(dest, e1[:, None], axis=1)[:, 0].astype(I32)
    wt0 = jnp.take_along_axis(comb, e0[:, None], axis=1)
    wt1 = jnp.take_along_axis(comb, e1[:, None], axis=1)
    p = t * TOP_K + n_exp * tm
    tok = jnp.arange(t, dtype=I32)
    src_tok = jnp.zeros((p,), I32).at[jnp.concatenate([pos0, pos1])].set(jnp.concatenate([tok, tok]))
    n_tiles = p // tm
    tstart = jnp.arange(n_tiles, dtype=I32) * tm
    tile_valid = (tstart < gend[-1]).astype(I32)
    te_raw = jnp.minimum(jnp.sum((tstart[:, None] >= gend[None, :]).astype(I32), axis=1), n_exp - 1)
    te_last = te_raw[jnp.maximum(gend[-1] // tm - 1, 0)]
    tile_expert = jnp.where(tile_valid == 1, te_raw, te_last).astype(I32)

    hs = _dispatch(x2, g, src_tok, rows=_tile(p, 256))
    us = _moe_up(hs, w1, w3, tile_expert, tile_valid, (goff // tm).astype(I32), (gsize // tm).astype(I32), tm)
    ys = _moe_down(us, w2, tile_expert, tile_valid, tm)
    return _combine(x2, ys, pos0, pos1, wt0, wt1, rows=_tile(t, 256))


def _pack_w_in(w_in_l, d_model, mix_w):
    iq_w = IDX_HEADS * IDX_DH
    n_main = 10 * mix_w
    small = IDX_DH + IDX_HEADS
    gates0 = n_main + iq_w + small
    pad = iq_w - small
    w = w_in_l.astype(BF16)
    return jnp.concatenate([w[:, :n_main], w[:, gates0:gates0 + N_BRANCH * d_model],
                            w[:, n_main:n_main + iq_w], w[:, n_main + iq_w:gates0],
                            jnp.zeros((d_model, pad), BF16)], axis=1)


def _mixer(x2, b, s, positions, rel_bias, norm_g, w_in_l, w_br_l, w_o_l, gn_w, gn_b, conv_w, q_norm, k_norm):
    t, d_model = x2.shape
    mix_w = d_model // 2
    iq_w = IDX_HEADS * IDX_DH
    gate_col0 = 10 * mix_w
    iq_col0 = gate_col0 + N_BRANCH * d_model
    assert iq_col0 % iq_w == 0

    h = _rmsnorm(x2, norm_g)
    proj2 = _matmul(h, _pack_w_in(w_in_l, d_model, mix_w), BF16, 2048, 512, name="in_proj")
    proj3 = proj2.reshape(b, s, proj2.shape[1])
    pos_q = positions.reshape(b, s, 1)
    pos_k = positions.reshape(b, 1, s)

    y_ret = _retention_branch(proj3, pos_q, gn_w, gn_b, mix_w)
    y_conv = _conv_branch(proj3, conv_w, mix_w)

    ik0 = iq_col0 + iq_w
    kit = jnp.swapaxes(proj3[:, :, ik0:ik0 + IDX_DH], 1, 2)
    mask = _dsa_select(proj3, kit, s, min(TOPK_MAX, s // 4), iq_col0 // iq_w)
    qn, kn = _qk_norm(proj3, q_norm, k_norm, mix_w)
    y_att = _dsa_attention(qn, kn, proj3, mask, pos_q, pos_k, rel_bias, mix_w)

    ys = [y.reshape(t, mix_w) for y in (y_ret, y_conv, y_att)]
    merged = _branch_merge(ys, w_br_l.astype(BF16), proj2, d_model, mix_w, gate_col0)
    return _matmul(merged, w_o_l.astype(BF16), F32, 1024, 512, res=x2, name="out_proj")


def kernel(x, positions, rel_bias, norm_mix, norm_ffn, w_in, w_br, w_o, ret_gn_w, ret_gn_b, conv_w,
           q_norm, k_norm, ffn_w1, ffn_w3, ffn_w2, moe_router, moe_w1, moe_w3, moe_w2):
    b, s, d_model = x.shape
    depth = w_in.shape[0]
    x2 = x.reshape(b * s, d_model)
    for l in range(depth):
        x2 = _mixer(x2, b, s, positions, rel_bias, norm_mix[l], w_in[l], w_br[l], w_o[l],
                    ret_gn_w[l], ret_gn_b[l], conv_w[l], q_norm[l], k_norm[l])
        if l % 2 == 0:
            i = l // 2
            h = _rmsnorm(x2, norm_ffn[l])
            u = _glu(h, ffn_w1[i].astype(BF16), ffn_w3[i].astype(BF16))
            x2 = _matmul(u, ffn_w2[i].astype(BF16), F32, 512, 512, res=x2, name="ffn_down")
        else:
            i = l // 2
            x2 = _moe(x2, norm_ffn[l], moe_router[i], moe_w1[i], moe_w3[i], moe_w2[i].astype(BF16))
    return x2.reshape(b, s, d_model)
```

```python
import functools
import math

import numpy as np
import jax
import jax.numpy as jnp
from jax import lax
from jax.experimental import pallas as pl
from jax.experimental.pallas import tpu as pltpu

F32 = jnp.float32
BF16 = jnp.bfloat16
I32 = jnp.int32

EPS = 1e-6
N_BRANCH = 3
RET_HEADS = 8
RET_CHUNK = 128
RET_ROPE_BASE = 10000.0
CONV_K = 3
ATT_HEADS = 8
IDX_HEADS = 8
IDX_DH = 64
TOPK_MAX = 256
REL_BUCKETS = 32
REL_MAX_DIST = 128
TOP_K = 2

LANES = 128
SUBLANES = 8
VMEM_LIMIT_BYTES = 56 * 1024 * 1024
ROW_DMA_UNROLL = 8

NEG_MASK = -1e30
LOG2E = math.log2(math.e)
INT_MIN = -2 ** 31
KEY_BITS = 32


def _cparams(sem):
    return pltpu.CompilerParams(dimension_semantics=sem, vmem_limit_bytes=VMEM_LIMIT_BYTES)


def _sigmoid(x):
    return 1.0 / (1.0 + jnp.exp(-x))


def _tile(n, pref):
    t = min(n, pref)
    assert n % t == 0, (n, pref)
    return t


def _rmsnorm_kernel(x_ref, g_ref, o_ref):
    x = x_ref[...]
    ms = jnp.mean(x * x, axis=-1, keepdims=True)
    o_ref[...] = (x * lax.rsqrt(ms + EPS) * g_ref[...]).astype(o_ref.dtype)


def _rmsnorm(x2, g):
    t, d = x2.shape
    tm = _tile(t, 512)
    return pl.pallas_call(
        _rmsnorm_kernel,
        out_shape=jax.ShapeDtypeStruct((t, d), BF16),
        grid=(t // tm,),
        in_specs=[pl.BlockSpec((tm, d), lambda i: (i, 0)),
                  pl.BlockSpec((1, d), lambda i: (0, 0))],
        out_specs=pl.BlockSpec((tm, d), lambda i: (i, 0)),
        compiler_params=_cparams(("parallel",)),
        name="rmsnorm",
    )(x2, g.reshape(1, d))


def _mm_kernel(a_ref, b_ref, o_ref):
    o_ref[...] = jnp.dot(a_ref[...], b_ref[...], preferred_element_type=F32).astype(o_ref.dtype)


def _mm_res_kernel(a_ref, b_ref, r_ref, o_ref):
    o_ref[...] = (r_ref[...] + jnp.dot(a_ref[...], b_ref[...], preferred_element_type=F32)).astype(o_ref.dtype)


def _matmul(a, b, out_dtype, tm, tn, res=None, name="matmul"):
    m, k = a.shape
    k2, n = b.shape
    assert k == k2
    tm = _tile(m, tm)
    tn = _tile(n, tn)
    in_specs = [pl.BlockSpec((tm, k), lambda i, j: (i, 0)),
                pl.BlockSpec((k, tn), lambda i, j: (0, j))]
    args = [a, b]
    kern = _mm_kernel
    if res is not None:
        in_specs.append(pl.BlockSpec((tm, tn), lambda i, j: (i, j)))
        args.append(res)
        kern = _mm_res_kernel
    return pl.pallas_call(
        kern,
        out_shape=jax.ShapeDtypeStruct((m, n), out_dtype),
        grid=(m // tm, n // tn),
        in_specs=in_specs,
        out_specs=pl.BlockSpec((tm, tn), lambda i, j: (i, j)),
        compiler_params=_cparams(("parallel", "parallel")),
        name=name,
    )(*args)


def _ret_kernel(pos_ref, invf_ref, q_ref, k_ref, v_ref, g_ref, dec_ref, kdec_ref, qdec_ref,
                gnw_ref, gnb_ref, o_ref, state_ref, *, heads, dh, chunk_decay):
    @pl.when(pl.program_id(1) == 0)
    def _():
        state_ref[...] = jnp.zeros_like(state_ref)

    c = q_ref.shape[1]
    ang = pos_ref[0].astype(F32) * invf_ref[...]
    cos = jnp.cos(ang)
    sin = jnp.sin(ang)
    lane = lax.broadcasted_iota(I32, (c, dh), 1)
    sin_signed = jnp.where(lane < dh // 2, -sin, sin)
    scale = dh ** -0.5
    for h in range(heads):
        sl = slice(h * dh, (h + 1) * dh)
        q = q_ref[0, :, sl].astype(F32)
        k = k_ref[0, :, sl].astype(F32)
        v = v_ref[0, :, sl]
        qr = (q * cos + pltpu.roll(q, dh // 2, 1) * sin_signed).astype(BF16)
        kr = (k * cos + pltpu.roll(k, dh // 2, 1) * sin_signed) * scale
        scores = lax.dot_general(qr, kr.astype(BF16), (((1,), (1,)), ((), ())),
                                 preferred_element_type=F32) * dec_ref[h]
        intra = jnp.dot(scores.astype(BF16), v, preferred_element_type=F32)
        prev = state_ref[h]
        cross = jnp.dot(qr, prev.astype(BF16), preferred_element_type=F32) * qdec_ref[h]
        kd = (kr * kdec_ref[h]).astype(BF16)
        kv = lax.dot_general(kd, v, (((0,), (0,)), ((), ())), preferred_element_type=F32)
        state_ref[h] = prev * chunk_decay[h] + kv
        ret = intra + cross
        mu = jnp.mean(ret, axis=-1, keepdims=True)
        cen = ret - mu
        var = jnp.mean(cen * cen, axis=-1, keepdims=True)
        gn = cen * lax.rsqrt(var + EPS) * gnw_ref[:, sl] + gnb_ref[:, sl]
        g = g_ref[0, :, sl].astype(F32)
        o_ref[0, :, sl] = (g * _sigmoid(g) * gn).astype(o_ref.dtype)


def _retention_branch(proj3, pos3, gn_w, gn_b, mix_w):
    b, s, _ = proj3.shape
    heads, c = RET_HEADS, RET_CHUNK
    dh = mix_w // heads
    assert dh == LANES and s % c == 0
    log_g = np.log1p(-np.exp2(-5.0 - np.arange(heads, dtype=np.float64)))
    pos = np.arange(c, dtype=np.float64)
    diff = pos[:, None] - pos[None, :]
    intra_decay = np.where(diff >= 0, np.exp(np.maximum(diff, 0.0)[None] * log_g[:, None, None]), 0.0)
    k_decay = np.exp((c - 1 - pos)[None, :] * log_g[:, None])
    q_decay = np.exp((pos + 1)[None, :] * log_g[:, None])
    chunk_decay = tuple(float(x) for x in np.exp(c * log_g))
    dec = jnp.asarray(intra_decay, F32)
    kdec = jnp.asarray(np.broadcast_to(k_decay[:, :, None], (heads, c, dh)), F32)
    qdec = jnp.asarray(np.broadcast_to(q_decay[:, :, None], (heads, c, dh)), F32)
    inv_freq = RET_ROPE_BASE ** (-jnp.arange(0, dh, 2, dtype=F32) / dh)
    invf = jnp.concatenate([inv_freq, inv_freq]).reshape(1, dh)

    def col(cb):
        return pl.BlockSpec((1, c, mix_w), lambda bi, n, cb=cb: (bi, n, cb))

    const3 = pl.BlockSpec((heads, c, dh), lambda bi, n: (0, 0, 0))
    return pl.pallas_call(
        functools.partial(_ret_kernel, heads=heads, dh=dh, chunk_decay=chunk_decay),
        out_shape=jax.ShapeDtypeStruct((b, s, mix_w), BF16),
        grid=(b, s // c),
        in_specs=[pl.BlockSpec((1, c, 1), lambda bi, n: (bi, n, 0)),
                  pl.BlockSpec((1, dh), lambda bi, n: (0, 0)),
                  col(0), col(1), col(2), col(3),
                  const3, const3, const3,
                  pl.BlockSpec((1, mix_w), lambda bi, n: (0, 0)),
                  pl.BlockSpec((1, mix_w), lambda bi, n: (0, 0))],
        out_specs=pl.BlockSpec((1, c, mix_w), lambda bi, n: (bi, n, 0)),
        scratch_shapes=[pltpu.VMEM((heads, dh, dh), F32)],
        compiler_params=_cparams(("arbitrary", "arbitrary")),
        name="retention",
    )(pos3, invf, proj3, proj3, proj3, proj3, dec, kdec, qdec,
      gn_w.reshape(1, mix_w), gn_b.reshape(1, mix_w))


def _conv_kernel(b_ref, c_ref, u_ref, w_ref, o_ref, ubuf_ref):
    ts = o_ref.shape[1]
    halo = SUBLANES

    @pl.when(pl.program_id(1) == 0)
    def _():
        ubuf_ref[0:halo, :] = jnp.zeros((halo, ubuf_ref.shape[1]), F32)

    u = c_ref[0].astype(F32) * u_ref[0].astype(F32)
    ubuf_ref[halo:halo + ts, :] = u
    u1 = ubuf_ref[halo - 1:halo - 1 + ts, :]
    u2 = ubuf_ref[halo - 2:halo - 2 + ts, :]
    conv = w_ref[0:1, :] * u2 + w_ref[1:2, :] * u1 + w_ref[2:3, :] * u
    o_ref[0] = (b_ref[0].astype(F32) * conv).astype(o_ref.dtype)
    ubuf_ref[0:halo, :] = ubuf_ref[ts:ts + halo, :]


def _conv_branch(proj3, conv_w, mix_w):
    b, s, _ = proj3.shape
    ts = _tile(s, 512)

    def col(cb):
        return pl.BlockSpec((1, ts, mix_w), lambda bi, n, cb=cb: (bi, n, cb))

    return pl.pallas_call(
        _conv_kernel,
        out_shape=jax.ShapeDtypeStruct((b, s, mix_w), BF16),
        grid=(b, s // ts),
        in_specs=[col(4), col(5), col(6),
                  pl.BlockSpec((CONV_K, mix_w), lambda bi, n: (0, 0))],
        out_specs=pl.BlockSpec((1, ts, mix_w), lambda bi, n: (bi, n, 0)),
        scratch_shapes=[pltpu.VMEM((ts + SUBLANES, mix_w), F32)],
        compiler_params=_cparams(("arbitrary", "arbitrary")),
        name="short_conv",
    )(proj3, proj3, proj3, conv_w)


def _qknorm_kernel(q_ref, k_ref, qn_ref, kn_ref, qo_ref, ko_ref, *, heads, dh, q_scale):
    for h in range(heads):
        sl = slice(h * dh, (h + 1) * dh)
        q = q_ref[0, :, sl].astype(F32)
        k = k_ref[0, :, sl].astype(F32)
        qy = q * lax.rsqrt(jnp.mean(q * q, axis=-1, keepdims=True) + EPS) * qn_ref[...]
        ky = k * lax.rsqrt(jnp.mean(k * k, axis=-1, keepdims=True) + EPS) * kn_ref[...]
        qo_ref[0, :, sl] = (qy * q_scale).astype(qo_ref.dtype)
        ko_ref[0, :, sl] = ky.astype(ko_ref.dtype)


def _qk_norm(proj3, q_norm, k_norm, mix_w):
    b, s, _ = proj3.shape
    heads = ATT_HEADS
    dh = mix_w // heads
    ts = _tile(s, 512)
    out = jax.ShapeDtypeStruct((b, s, mix_w), BF16)
    blk = pl.BlockSpec((1, ts, mix_w), lambda bi, n: (bi, n, 0))
    return pl.pallas_call(
        functools.partial(_qknorm_kernel, heads=heads, dh=dh, q_scale=dh ** -0.5 * LOG2E),
        out_shape=(out, out),
        grid=(b, s // ts),
        in_specs=[pl.BlockSpec((1, ts, mix_w), lambda bi, n: (bi, n, 7)),
                  pl.BlockSpec((1, ts, mix_w), lambda bi, n: (bi, n, 8)),
                  pl.BlockSpec((1, dh), lambda bi, n: (0, 0)),
                  pl.BlockSpec((1, dh), lambda bi, n: (0, 0))],
        out_specs=(blk, blk),
        compiler_params=_cparams(("parallel", "parallel")),
        name="qk_norm",
    )(proj3, proj3, q_norm.reshape(1, dh), k_norm.reshape(1, dh))


def _select_kernel(iq_ref, sm_ref, kit_ref, tri_ref, o_ref, qall_ref, keys_ref, planes_ref, eq_ref, *,
                   tq, ck, ksel, heads, d_idx):
    qb = pl.program_id(1)
    row0 = qb * tq
    nch = (row0 + tq + ck - 1) // ck
    grp = KEY_BITS * LANES
    ngrp = keys_ref.shape[1] // grp
    nlive = (row0 + tq + grp - 1) // grp
    for h in range(heads):
        qall_ref[h * tq:(h + 1) * tq, :] = iq_ref[0, :, h * d_idx:(h + 1) * d_idx]
    w = sm_ref[0, :, d_idx:d_idx + heads].astype(F32) * (heads ** -0.5 * d_idx ** -0.5)
    t_col = row0 + lax.broadcasted_iota(I32, (tq, 1), 0)
    lane_ck = lax.broadcasted_iota(I32, (1, ck), 1)

    def score_chunk(c, carry):
        off = pl.multiple_of(c * ck, ck)
        kt = kit_ref[0, :, pl.ds(off, ck)]
        sc = jnp.zeros((tq, ck), F32)
        for h in range(heads):
            d = jnp.dot(qall_ref[h * tq:(h + 1) * tq, :], kt, preferred_element_type=F32)
            sc = sc + w[:, h:h + 1] * jnp.maximum(d, 0.0)
        bits = pltpu.bitcast(sc, I32)
        key = bits ^ ((bits >> 31) & 0x7FFFFFFF)
        keys_ref[:, pl.ds(off, ck)] = jnp.where(off + lane_ck <= t_col, key, INT_MIN)
        return carry

    def clear_chunk(c, carry):
        keys_ref[:, pl.ds(pl.multiple_of(c * ck, ck), ck)] = jnp.full((tq, ck), INT_MIN, I32)
        return carry

    lax.fori_loop(0, nch, score_chunk, 0)
    lax.fori_loop(nch, nlive * (grp // ck), clear_chunk, 0)

    def transpose_group(g):
        def rows(r, carry):
            rs = pl.ds(pl.multiple_of(r * SUBLANES, SUBLANES), SUBLANES)
            a = [keys_ref[rs, (g * KEY_BITS + j) * LANES:(g * KEY_BITS + j + 1) * LANES] ^ INT_MIN
                 for j in range(KEY_BITS)]
            j, m = KEY_BITS // 2, (1 << (KEY_BITS // 2)) - 1
            while j:
                k = 0
                while k < KEY_BITS:
                    t = (a[k] ^ (a[k + j] >> j)) & m
                    a[k] = a[k] ^ t
                    a[k + j] = a[k + j] ^ (t << j)
                    k = (k + j + 1) & ~j
                j >>= 1
                m = m ^ (m << j)
            for p in range(KEY_BITS):
                planes_ref[p, g, rs, :] = a[p]
            return carry

        lax.fori_loop(0, tq // SUBLANES, rows, 0)

    @pl.when(qb == 0)
    def _():
        planes_ref[...] = jnp.zeros_like(planes_ref)

    for g in range(ngrp):
        @pl.when(g < nlive)
        def _(g=g):
            transpose_group(g)
            eq_ref[g] = jnp.full(eq_ref.shape[1:], -1, I32)

        @pl.when(g >= nlive)
        def _(g=g):
            eq_ref[g] = jnp.zeros(eq_ref.shape[1:], I32)

    kf = float(ksel)

    def bit_pass(p, carry):
        above, thr_u = carry
        ones = jnp.zeros((tq, LANES), I32)
        for g in range(ngrp):
            ones = ones + lax.population_count(eq_ref[g] & planes_ref[p, g])
        c1 = jnp.sum(ones.astype(F32), axis=1, keepdims=True)
        take = (above + c1) >= kf
        for g in range(ngrp):
            e = eq_ref[g]
            x = e & planes_ref[p, g]
            eq_ref[g] = jnp.where(take, x, e ^ x)
        bit = jnp.left_shift(jnp.int32(1), KEY_BITS - 1 - p)
        return jnp.where(take, above, above + c1), jnp.where(take, thr_u | bit, thr_u)

    cnt_gt, thr_u = lax.fori_loop(0, KEY_BITS, bit_pass,
                                  (jnp.zeros((tq, 1), F32), jnp.zeros((tq, 1), I32)))
    thr = thr_u ^ INT_MIN
    eq_cnt = jnp.zeros((tq, LANES), I32)
    for g in range(ngrp):
        eq_cnt = eq_cnt + lax.population_count(eq_ref[g])
    cnt_eq = jnp.sum(eq_cnt.astype(F32), axis=1, keepdims=True)

    need = kf - cnt_gt
    excess = jnp.where(thr == INT_MIN, 0.0, cnt_eq - need)
    o_ref[...] = jnp.full(o_ref.shape, NEG_MASK, o_ref.dtype)

    def write_chunk(c, eq_before, ties):
        off = pl.multiple_of(c * ck, ck)
        sl = pl.ds(off, ck)
        k = keys_ref[:, sl]
        if ties:
            eq = jnp.where(k == thr, 1.0, 0.0).astype(BF16)
            rank = eq_before + jnp.dot(eq, tri_ref[...], preferred_element_type=F32)
            on_eq = jnp.where(rank <= need, 0.0, NEG_MASK)
            eq_before = rank[:, ck - 1:ck]
        else:
            on_eq = 0.0
        val = jnp.where(k > thr, 0.0, jnp.where(k == thr, on_eq, NEG_MASK))
        o_ref[0, :, sl] = jnp.where(off + lane_ck <= t_col, val, NEG_MASK).astype(o_ref.dtype)
        return eq_before

    any_excess = jnp.max(excess) > 0.0

    @pl.when(any_excess)
    def _():
        lax.fori_loop(0, nch, lambda c, e: write_chunk(c, e, True), jnp.zeros((tq, 1), F32))

    @pl.when(jnp.logical_not(any_excess))
    def _():
        lax.fori_loop(0, nch, lambda c, e: write_chunk(c, e, False), 0)


def _dsa_select(proj3, kit, s, ksel, iq_block):
    b = proj3.shape[0]
    tq = _tile(s, 256)
    ck = _tile(s, 512)
    heads, d_idx = IDX_HEADS, IDX_DH
    iq_w = heads * d_idx
    assert (s % (KEY_BITS * LANES) == 0 or s < KEY_BITS * LANES) and (KEY_BITS * LANES) % ck == 0
    sp = max(s, KEY_BITS * LANES)
    tri = (jnp.arange(ck)[:, None] <= jnp.arange(ck)[None, :]).astype(BF16)
    return pl.pallas_call(
        functools.partial(_select_kernel, tq=tq, ck=ck, ksel=ksel, heads=heads, d_idx=d_idx),
        out_shape=jax.ShapeDtypeStruct((b, s, s), BF16),
        grid=(b, s // tq),
        in_specs=[pl.BlockSpec((1, tq, iq_w), lambda bi, n: (bi, n, iq_block)),
                  pl.BlockSpec((1, tq, iq_w), lambda bi, n: (bi, n, iq_block + 1)),
                  pl.BlockSpec((1, d_idx, s), lambda bi, n: (bi, 0, 0)),
                  pl.BlockSpec((ck, ck), lambda bi, n: (0, 0))],
        out_specs=pl.BlockSpec((1, tq, s), lambda bi, n: (bi, n, 0)),
        scratch_shapes=[pltpu.VMEM((heads * tq, d_idx), BF16),
                        pltpu.VMEM((tq, sp), I32),
                        pltpu.VMEM((KEY_BITS, sp // (KEY_BITS * LANES), tq, LANES), I32),
                        pltpu.VMEM((sp // (KEY_BITS * LANES), tq, LANES), I32)],
        compiler_params=_cparams(("arbitrary", "arbitrary")),
        name="dsa_select",
    )(proj3, proj3, kit, tri)


def _t5_bucket(n):
    max_exact = REL_BUCKETS // 2
    nf = jnp.maximum(n, 1).astype(F32)
    large = max_exact + (jnp.log(nf / max_exact) / math.log(REL_MAX_DIST / max_exact)
                         * (REL_BUCKETS - max_exact)).astype(I32)
    large = jnp.minimum(large, REL_BUCKETS - 1)
    return jnp.where(n < max_exact, n, large)


def _attn_kernel(qb_ref, kb_ref, rb_ref, q_ref, k_ref, v_ref, mask_ref, pq_ref, pk_ref, o_ref,
                 acc_ref, m_ref, l_ref, bias_ref, s_ref, p_ref, *, tq, tk, heads, dh):
    qb = qb_ref[pl.program_id(1)]
    kb = kb_ref[pl.program_id(1)]
    kmax = ((qb + 1) * tq - 1) // tk
    nlt = tk // LANES
    gran = LANES

    @pl.when(kb == 0)
    def _():
        acc_ref[...] = jnp.zeros_like(acc_ref)
        m_ref[...] = jnp.full(m_ref.shape, -jnp.inf, F32)
        l_ref[...] = jnp.zeros_like(l_ref)

    def attend(bias_of):
        ones = jnp.ones((tk, LANES), BF16)
        for h in range(heads):
            sl = slice(h * dh, (h + 1) * dh)
            kh = k_ref[0, :, sl]
            v_aug = jnp.concatenate([v_ref[0, :, sl], ones], axis=1)
            add, const = bias_of(h)
            buf = h % 2
            s = lax.dot_general(q_ref[0, :, sl], kh, (((1,), (1,)), ((), ())),
                                preferred_element_type=F32) + add
            s_ref[buf] = s
            tile_max = s[:, 0:LANES]
            for c in range(1, nlt):
                tile_max = jnp.maximum(tile_max, s[:, c * LANES:(c + 1) * LANES])
            m_cur = jnp.broadcast_to(jnp.max(tile_max, axis=1, keepdims=True), (tq, LANES)) + const
            m_old = m_ref[h]
            m_new = jnp.maximum(m_old, m_cur)
            alpha = jnp.exp2(m_old - m_new)
            shift = m_new - const
            for c in range(nlt):
                cs = slice(c * LANES, (c + 1) * LANES)
                p_ref[buf, :, cs] = jnp.exp2(s_ref[buf, :, cs] - shift).astype(BF16)
            pv = jnp.dot(p_ref[buf], v_aug, preferred_element_type=F32)
            m_ref[h] = m_new
            l_ref[h] = alpha * l_ref[h] + pv[:, dh:]
            acc_ref[:, sl] = alpha * acc_ref[:, sl] + pv[:, :dh]

    @pl.when(kb <= kmax)
    def _():
        pq = pq_ref[0]
        pk = pk_ref[0]
        all_far = (jnp.min(pq) - jnp.max(pk)) >= REL_MAX_DIST

        @pl.when(all_far)
        def _():
            attend(lambda h: (mask_ref[0].astype(F32), rb_ref[REL_BUCKETS - 1, h] * LOG2E))

        @pl.when(jnp.logical_not(all_far))
        def _():
            n_lane = lax.broadcasted_iota(I32, (1, REL_MAX_DIST), 1)
            bucket = _t5_bucket(n_lane)
            tabs = []
            for h in range(heads):
                tab = jnp.zeros((1, REL_MAX_DIST), F32)
                for j in range(REL_BUCKETS):
                    tab = jnp.where(bucket == j, rb_ref[j, h] * LOG2E, tab)
                tabs.append(jnp.broadcast_to(tab, (gran, REL_MAX_DIST)))
            k_lo = [jnp.min(pk[:, cj * gran:(cj + 1) * gran]) for cj in range(tk // gran)]
            k_hi = [jnp.max(pk[:, cj * gran:(cj + 1) * gran]) for cj in range(tk // gran)]
            for ri in range(tq // gran):
                rs = slice(ri * gran, (ri + 1) * gran)
                pq_g = pq[rs]
                q_lo = jnp.min(pq_g)
                q_hi = jnp.max(pq_g)
                for cj in range(tk // gran):
                    cs = slice(cj * gran, (cj + 1) * gran)
                    pk_g = pk[:, cs]
                    lo = q_lo - k_hi[cj]
                    hi = q_hi - k_lo[cj]
                    is_far = lo >= REL_MAX_DIST
                    is_zero = hi <= 0
                    maskf = mask_ref[0, rs, cs].astype(F32)

                    @pl.when(is_far | is_zero)
                    def _():
                        for h in range(heads):
                            c_h = jnp.where(is_far, rb_ref[REL_BUCKETS - 1, h], rb_ref[0, h]) * LOG2E
                            bias_ref[h, rs, cs] = maskf + c_h

                    @pl.when(jnp.logical_not(is_far | is_zero))
                    def _():
                        dist = jnp.clip(pq_g - pk_g, 0, REL_MAX_DIST - 1)
                        for h in range(heads):
                            bias_ref[h, rs, cs] = maskf + jnp.take_along_axis(tabs[h], dist, axis=1)
            attend(lambda h: (bias_ref[h], 0.0))

    @pl.when(kb == kmax)
    def _():
        for h in range(heads):
            sl = slice(h * dh, (h + 1) * dh)
            o_ref[0, :, sl] = (acc_ref[:, sl] / l_ref[h]).astype(o_ref.dtype)


def _dsa_attention(qn, kn, proj3, mask, pos_q, pos_k, rel_bias, mix_w):
    b, s, _ = qn.shape
    heads = ATT_HEADS
    dh = mix_w // heads
    tq = _tile(s, 512)
    tk = tq
    assert REL_MAX_DIST == LANES

    tiles = [(qb, kb) for qb in range(s // tq) for kb in range(((qb + 1) * tq - 1) // tk + 1)]
    qb_of = jnp.asarray([t[0] for t in tiles], I32)
    kb_of = jnp.asarray([t[1] for t in tiles], I32)

    return pl.pallas_call(
        functools.partial(_attn_kernel, tq=tq, tk=tk, heads=heads, dh=dh),
        out_shape=jax.ShapeDtypeStruct((b, s, mix_w), BF16),
        grid_spec=pltpu.PrefetchScalarGridSpec(
            num_scalar_prefetch=2,
            grid=(b, len(tiles)),
            in_specs=[pl.BlockSpec(memory_space=pltpu.SMEM),
                      pl.BlockSpec((1, tq, mix_w), lambda bi, st, qo, ko: (bi, qo[st], 0)),
                      pl.BlockSpec((1, tk, mix_w), lambda bi, st, qo, ko: (bi, ko[st], 0)),
                      pl.BlockSpec((1, tk, mix_w), lambda bi, st, qo, ko: (bi, ko[st], 9)),
                      pl.BlockSpec((1, tq, tk), lambda bi, st, qo, ko: (bi, qo[st], ko[st])),
                      pl.BlockSpec((1, tq, 1), lambda bi, st, qo, ko: (bi, qo[st], 0)),
                      pl.BlockSpec((1, 1, tk), lambda bi, st, qo, ko: (bi, 0, ko[st]))],
            out_specs=pl.BlockSpec((1, tq, mix_w), lambda bi, st, qo, ko: (bi, qo[st], 0)),
            scratch_shapes=[pltpu.VMEM((tq, mix_w), F32),
                            pltpu.VMEM((heads, tq, LANES), F32),
                            pltpu.VMEM((heads, tq, LANES), F32),
                            pltpu.VMEM((heads, tq, tk), F32),
                            pltpu.VMEM((2, tq, tk), F32),
                            pltpu.VMEM((2, tq, tk), BF16)]),
        compiler_params=_cparams(("parallel", "arbitrary")),
        name="dsa_attention",
    )(qb_of, kb_of, rel_bias, qn, kn, proj3, mask, pos_q, pos_k)


def _merge_kernel(y0_ref, y1_ref, y2_ref, w_ref, g0_ref, g1_ref, g2_ref, o_ref):
    acc = None
    for n, (y_ref, g_ref) in enumerate(((y0_ref, g0_ref), (y1_ref, g1_ref), (y2_ref, g2_ref))):
        br = jnp.dot(y_ref[...], w_ref[n], preferred_element_type=F32)
        term = _sigmoid(g_ref[...].astype(F32)) * br
        acc = term if acc is None else acc + term
    o_ref[...] = acc.astype(o_ref.dtype)


def _branch_merge(ys, w_br, proj2, d_model, mix_w, gate_col0):
    t = proj2.shape[0]
    tm = _tile(t, 1024)
    tn = _tile(d_model, 512)
    yspec = pl.BlockSpec((tm, mix_w), lambda i, j: (i, 0))

    def gspec(n):
        base = (gate_col0 + n * d_model) // tn
        return pl.BlockSpec((tm, tn), lambda i, j, base=base: (i, base + j))

    return pl.pallas_call(
        _merge_kernel,
        out_shape=jax.ShapeDtypeStruct((t, d_model), BF16),
        grid=(t // tm, d_model // tn),
        in_specs=[yspec, yspec, yspec,
                  pl.BlockSpec((N_BRANCH, mix_w, tn), lambda i, j: (0, 0, j)),
                  gspec(0), gspec(1), gspec(2)],
        out_specs=pl.BlockSpec((tm, tn), lambda i, j: (i, j)),
        compiler_params=_cparams(("parallel", "parallel")),
        name="branch_merge",
    )(ys[0], ys[1], ys[2], w_br, proj2, proj2, proj2)


def _glu_kernel(a_ref, w1_ref, w3_ref, o_ref):
    a = a_ref[...]
    h1 = jnp.dot(a, w1_ref[...], preferred_element_type=F32)
    h3 = jnp.dot(a, w3_ref[...], preferred_element_type=F32)
    o_ref[...] = (h1 * _sigmoid(h1) * h3).astype(o_ref.dtype)


def _glu(a, w1, w3):
    t, d = a.shape
    ff = w1.shape[1]
    tm = _tile(t, 1024)
    tn = _tile(ff, 512)
    return pl.pallas_call(
        _glu_kernel,
        out_shape=jax.ShapeDtypeStruct((t, ff), BF16),
        grid=(t // tm, ff // tn),
        in_specs=[pl.BlockSpec((tm, d), lambda i, j: (i, 0)),
                  pl.BlockSpec((d, tn), lambda i, j: (0, j)),
                  pl.BlockSpec((d, tn), lambda i, j: (0, j))],
        out_specs=pl.BlockSpec((tm, tn), lambda i, j: (i, j)),
        compiler_params=_cparams(("parallel", "parallel")),
        name="swiglu_up",
    )(a, w1, w3)


def _router_kernel(x_ref, g_ref, r_ref, comb_ref, sel_ref, *, n_exp):
    x = x_ref[...]
    h = (x * lax.rsqrt(jnp.mean(x * x, axis=-1, keepdims=True) + EPS) * g_ref[...]).astype(BF16)
    logits = jnp.dot(h, r_ref[...], preferred_element_type=F32)
    lane = lax.broadcasted_iota(I32, logits.shape, 1)
    lg = jnp.where(lane < n_exp, logits, -jnp.inf)
    v1 = jnp.max(lg, axis=1, keepdims=True)
    i1 = jnp.min(jnp.where(lg == v1, lane, LANES), axis=1, keepdims=True)
    lg2 = jnp.where(lane == i1, -jnp.inf, lg)
    v2 = jnp.max(lg2, axis=1, keepdims=True)
    i2 = jnp.min(jnp.where(lg2 == v2, lane, LANES), axis=1, keepdims=True)
    e = jnp.exp(v2 - v1)
    w1 = 1.0 / (1.0 + e)
    w2 = e / (1.0 + e)
    comb_ref[...] = jnp.where(lane == i1, w1, jnp.where(lane == i2, w2, 0.0))
    sel_ref[...] = jnp.where((lane == i1) | (lane == i2), 1, 0)


def _router(x2, g, router_w):
    t, d = x2.shape
    n_exp = router_w.shape[1]
    tm = _tile(t, 512)
    rpad = jnp.zeros((d, LANES), BF16).at[:, :n_exp].set(router_w.astype(BF16))
    comb, sel = pl.pallas_call(
        functools.partial(_router_kernel, n_exp=n_exp),
        out_shape=(jax.ShapeDtypeStruct((t, LANES), F32), jax.ShapeDtypeStruct((t, LANES), I32)),
        grid=(t // tm,),
        in_specs=[pl.BlockSpec((tm, d), lambda i: (i, 0)),
                  pl.BlockSpec((1, d), lambda i: (0, 0)),
                  pl.BlockSpec((d, LANES), lambda i: (0, 0))],
        out_specs=(pl.BlockSpec((tm, LANES), lambda i: (i, 0)),
                   pl.BlockSpec((tm, LANES), lambda i: (i, 0))),
        compiler_params=_cparams(("parallel",)),
        name="moe_router",
    )(x2, g.reshape(1, d), rpad)
    return comb[:, :n_exp], sel[:, :n_exp]


def _row_gather(idx_ref, src_hbm, buf_ref, sem, slot, rows, start):
    def body(r8, carry):
        for u in range(ROW_DMA_UNROLL):
            r = r8 * ROW_DMA_UNROLL + u
            cp = pltpu.make_async_copy(src_hbm.at[pl.ds(idx_ref[0, 0, r], 1)],
                                       buf_ref.at[slot, pl.ds(r, 1)], sem.at[slot])
            if start:
                cp.start(priority=u % 2)
            else:
                cp.wait()
        return carry

    lax.fori_loop(0, rows // ROW_DMA_UNROLL, body, 0)


def _dispatch_kernel(cur_ref, nxt_ref, x_hbm, g_ref, o_ref, buf_ref, sem, *, rows, nsteps):
    i = pl.program_id(0)
    slot = i % 2

    @pl.when(i == 0)
    def _():
        _row_gather(cur_ref, x_hbm, buf_ref, sem, 0, rows, start=True)

    @pl.when(i + 1 < nsteps)
    def _():
        _row_gather(nxt_ref, x_hbm, buf_ref, sem, 1 - slot, rows, start=True)

    _row_gather(cur_ref, x_hbm, buf_ref, sem, slot, rows, start=False)
    x = buf_ref[slot]
    o_ref[...] = (x * lax.rsqrt(jnp.mean(x * x, axis=-1, keepdims=True) + EPS) * g_ref[...]).astype(o_ref.dtype)


def _dispatch(x2, g, src_tok, rows):
    t, d = x2.shape
    p = src_tok.shape[0]
    assert p % rows == 0 and rows % ROW_DMA_UNROLL == 0
    nsteps = p // rows
    src3 = src_tok.reshape(nsteps, 1, rows)
    return pl.pallas_call(
        functools.partial(_dispatch_kernel, rows=rows, nsteps=nsteps),
        out_shape=jax.ShapeDtypeStruct((p, d), BF16),
        grid=(nsteps,),
        in_specs=[pl.BlockSpec((1, 1, rows), lambda i: (i, 0, 0), memory_space=pltpu.SMEM),
                  pl.BlockSpec((1, 1, rows), lambda i: (jnp.minimum(i + 1, nsteps - 1), 0, 0),
                               memory_space=pltpu.SMEM),
                  pl.BlockSpec(memory_space=pl.ANY),
                  pl.BlockSpec((1, d), lambda i: (0, 0))],
        out_specs=pl.BlockSpec((rows, d), lambda i: (i, 0)),
        scratch_shapes=[pltpu.VMEM((2, rows, d), F32), pltpu.SemaphoreType.DMA((2,))],
        compiler_params=_cparams(("arbitrary",)),
        name="moe_dispatch",
    )(src3, src3, x2, g.reshape(1, d))


def _moe_up_kernel(ti_ref, tj_ref, wj_ref, te_ref, first_ref, valid_ref, a_ref, w1_ref, w3_ref, o_ref,
                   w1b_ref, w3b_ref):
    s = pl.program_id(0)

    @pl.when(first_ref[s] == 1)
    def _():
        w1b_ref[...] = w1_ref[0].astype(BF16)
        w3b_ref[...] = w3_ref[0].astype(BF16)

    @pl.when(valid_ref[s] == 1)
    def _():
        a = a_ref[...]
        h1 = jnp.dot(a, w1b_ref[...], preferred_element_type=F32)
        h3 = jnp.dot(a, w3b_ref[...], preferred_element_type=F32)
        o_ref[...] = (h1 * _sigmoid(h1) * h3).astype(o_ref.dtype)

    @pl.when(valid_ref[s] == 0)
    def _():
        o_ref[...] = jnp.zeros_like(o_ref)


def _moe_up(hs, w1, w3, tile_expert, tile_valid, gfirst, gtiles, tm):
    p, d = hs.shape
    ff = w1.shape[2]
    tn = _tile(ff, 512)
    nj = ff // tn
    n_tiles = p // tm
    step = jnp.arange(n_tiles * nj, dtype=I32)
    t0 = step // nj
    e_s = tile_expert[t0]
    valid = tile_valid[t0]
    n_run = jnp.maximum(gtiles[e_s], 1)
    local = step - gfirst[e_s] * nj
    ti = jnp.where(valid == 1, gfirst[e_s] + local % n_run, t0).astype(I32)
    tj = jnp.where(valid == 1, local // n_run, step % nj).astype(I32)
    wj = jnp.where(valid == 1, local // n_run, nj - 1).astype(I32)
    first = jnp.where((valid == 1) & (local % n_run == 0), 1, 0).astype(I32)

    def amap(s, ti_r, tj_r, wj_r, te_r, f_r, v_r):
        return (ti_r[s], 0)

    def wmap(s, ti_r, tj_r, wj_r, te_r, f_r, v_r):
        return (te_r[s], 0, wj_r[s])

    def omap(s, ti_r, tj_r, wj_r, te_r, f_r, v_r):
        return (ti_r[s], tj_r[s])

    return pl.pallas_call(
        _moe_up_kernel,
        out_shape=jax.ShapeDtypeStruct((p, ff), BF16),
        grid_spec=pltpu.PrefetchScalarGridSpec(
            num_scalar_prefetch=6,
            grid=(n_tiles * nj,),
            in_specs=[pl.BlockSpec((tm, d), amap),
                      pl.BlockSpec((1, d, tn), wmap),
                      pl.BlockSpec((1, d, tn), wmap)],
            out_specs=pl.BlockSpec((tm, tn), omap),
            scratch_shapes=[pltpu.VMEM((d, tn), BF16), pltpu.VMEM((d, tn), BF16)]),
        compiler_params=_cparams(("arbitrary",)),
        name="moe_up",
    )(ti, tj, wj, e_s.astype(I32), first, valid.astype(I32), hs, w1, w3)


def _moe_down_kernel(te_ref, tv_ref, a_ref, w_ref, o_ref):
    i = pl.program_id(0)

    @pl.when(tv_ref[i] == 1)
    def _():
        o_ref[...] = jnp.dot(a_ref[...], w_ref[0], preferred_element_type=F32)

    @pl.when(tv_ref[i] == 0)
    def _():
        o_ref[...] = jnp.zeros_like(o_ref)


def _moe_down(us, w2, tile_expert, tile_valid, tm):
    p, ff = us.shape
    d = w2.shape[2]
    tn = _tile(d, 512)
    nj = d // tn

    def wmap(i, j, te, tv):
        return (te[i], 0, jnp.where(tv[i] == 1, j, nj - 1))

    return pl.pallas_call(
        _moe_down_kernel,
        out_shape=jax.ShapeDtypeStruct((p, d), F32),
        grid_spec=pltpu.PrefetchScalarGridSpec(
            num_scalar_prefetch=2,
            grid=(p // tm, nj),
            in_specs=[pl.BlockSpec((tm, ff), lambda i, j, te, tv: (i, 0)),
                      pl.BlockSpec((1, ff, tn), wmap)],
            out_specs=pl.BlockSpec((tm, tn), lambda i, j, te, tv: (i, j))),
        compiler_params=_cparams(("arbitrary", "arbitrary")),
        name="moe_down",
    )(tile_expert, tile_valid, us, w2)


def _combine_kernel(p0_ref, p1_ref, n0_ref, n1_ref, x_ref, w0_ref, w1_ref, ys_hbm, o_ref,
                    a_ref, b_ref, sem_a, sem_b, *, rows, nsteps):
    i = pl.program_id(0)
    slot = i % 2

    @pl.when(i == 0)
    def _():
        _row_gather(p0_ref, ys_hbm, a_ref, sem_a, 0, rows, start=True)
        _row_gather(p1_ref, ys_hbm, b_ref, sem_b, 0, rows, start=True)

    @pl.when(i + 1 < nsteps)
    def _():
        _row_gather(n0_ref, ys_hbm, a_ref, sem_a, 1 - slot, rows, start=True)
        _row_gather(n1_ref, ys_hbm, b_ref, sem_b, 1 - slot, rows, start=True)

    _row_gather(p0_ref, ys_hbm, a_ref, sem_a, slot, rows, start=False)
    _row_gather(p1_ref, ys_hbm, b_ref, sem_b, slot, rows, start=False)
    o_ref[...] = x_ref[...] + (w0_ref[...] * a_ref[slot] + w1_ref[...] * b_ref[slot])


def _combine(x2, ys, pos0, pos1, w0, w1, rows):
    t, d = x2.shape
    assert t % rows == 0 and rows % ROW_DMA_UNROLL == 0
    nsteps = t // rows
    cur = pl.BlockSpec((1, 1, rows), lambda i: (i, 0, 0), memory_space=pltpu.SMEM)
    nxt = pl.BlockSpec((1, 1, rows), lambda i: (jnp.minimum(i + 1, nsteps - 1), 0, 0),
                       memory_space=pltpu.SMEM)
    p0 = pos0.reshape(nsteps, 1, rows)
    p1 = pos1.reshape(nsteps, 1, rows)
    return pl.pallas_call(
        functools.partial(_combine_kernel, rows=rows, nsteps=nsteps),
        out_shape=jax.ShapeDtypeStruct((t, d), F32),
        grid=(nsteps,),
        in_specs=[cur, cur, nxt, nxt,
                  pl.BlockSpec((rows, d), lambda i: (i, 0)),
                  pl.BlockSpec((rows, 1), lambda i: (i, 0)),
                  pl.BlockSpec((rows, 1), lambda i: (i, 0)),
                  pl.BlockSpec(memory_space=pl.ANY)],
        out_specs=pl.BlockSpec((rows, d), lambda i: (i, 0)),
        scratch_shapes=[pltpu.VMEM((2, rows, d), F32), pltpu.VMEM((2, rows, d), F32),
                        pltpu.SemaphoreType.DMA((2,)), pltpu.SemaphoreType.DMA((2,))],
        compiler_params=_cparams(("arbitrary",)),
        name="moe_combine",
    )(p0, p1, p0, p1, x2, w0, w1, ys)


def _moe(x2, g, router_w, w1, w3, w2):
    t, d = x2.shape
    n_exp = router_w.shape[1]
    tm = _tile(t, 512)
    comb, sel = _router(x2, g, router_w)
    rank = jnp.cumsum(sel, axis=0) - sel
    counts = jnp.sum(sel, axis=0)
    gsize = ((counts + tm - 1) // tm) * tm
    gend = jnp.cumsum(gsize)
    goff = gend - gsize
    dest = goff[None, :] + rank
    e0 = jnp.argmax(sel, axis=1)
    e1 = (n_exp - 1) - jnp.argmax(sel[:, ::-1], axis=1)
    pos0 = jnp.take_along_axis(dest, e0[:, None], axis=1)[:, 0].astype(I32)
    pos1 = jnp.take_along_axis(dest, e1[:, None], axis=1)[:, 0].astype(I32)
    wt0 = jnp.take_along_axis(comb, e0[:, None], axis=1)
    wt1 = jnp.take_along_axis(comb, e1[:, None], axis=1)
    p = t * TOP_K + n_exp * tm
    tok = jnp.arange(t, dtype=I32)
    src_tok = jnp.zeros((p,), I32).at[jnp.concatenate([pos0, pos1])].set(jnp.concatenate([tok, tok]))
    n_tiles = p // tm
    tstart = jnp.arange(n_tiles, dtype=I32) * tm
    tile_valid = (tstart < gend[-1]).astype(I32)
    te_raw = jnp.minimum(jnp.sum((tstart[:, None] >= gend[None, :]).astype(I32), axis=1), n_exp - 1)
    te_last = te_raw[jnp.maximum(gend[-1] // tm - 1, 0)]
    tile_expert = jnp.where(tile_valid == 1, te_raw, te_last).astype(I32)

    hs = _dispatch(x2, g, src_tok, rows=_tile(p, 256))
    us = _moe_up(hs, w1, w3, tile_expert, tile_valid, (goff // tm).astype(I32), (gsize // tm).astype(I32), tm)
    ys = _moe_down(us, w2, tile_expert, tile_valid, tm)
    return _combine(x2, ys, pos0, pos1, wt0, wt1, rows=_tile(t, 256))


def _pack_kernel(ia_ref, ib_ref, sh_ref, nv_ref, a_ref, b_ref, o_ref, *, shift):
    j = pl.program_id(0)
    a = a_ref[0]
    tn = a.shape[1]
    shifted = jnp.concatenate([a[:, shift:], b_ref[0][:, :shift]], axis=1)
    out = jnp.where(sh_ref[j] == 1, shifted, a)
    lane = lax.broadcasted_iota(I32, (1, tn), 1)
    o_ref[...] = jnp.where(lane < nv_ref[j], out, jnp.zeros_like(out))


def _pack_w_in(w_in_bf, layer, d_model, mix_w):
    iq_w = IDX_HEADS * IDX_DH
    tn = iq_w
    n_main = 10 * mix_w
    small = IDX_DH + IDX_HEADS
    gates0 = n_main + iq_w + small
    n_gates = N_BRANCH * d_model
    assert n_main % tn == 0 and n_gates % tn == 0 and small < tn
    shift = gates0 % tn
    ia = ([j for j in range(n_main // tn)]
          + [gates0 // tn + q for q in range(n_gates // tn)]
          + [n_main // tn]
          + [(n_main + iq_w) // tn])
    nblk = len(ia)
    sh = [0] * (n_main // tn) + [1] * (n_gates // tn) + [0, 0]
    nv = [tn] * (nblk - 1) + [small]
    ib = [a + 1 if s_ == 1 else a for a, s_ in zip(ia, sh)]
    assert (max(ib) + 1) * tn >= gates0 + n_gates and max(ib) * tn < w_in_bf.shape[2]

    def amap(j, ia_r, ib_r, sh_r, nv_r):
        return (layer, 0, ia_r[j])

    def bmap(j, ia_r, ib_r, sh_r, nv_r):
        return (layer, 0, ib_r[j])

    return pl.pallas_call(
        functools.partial(_pack_kernel, shift=shift),
        out_shape=jax.ShapeDtypeStruct((d_model, nblk * tn), BF16),
        grid_spec=pltpu.PrefetchScalarGridSpec(
            num_scalar_prefetch=4,
            grid=(nblk,),
            in_specs=[pl.BlockSpec((1, d_model, tn), amap),
                      pl.BlockSpec((1, d_model, tn), bmap)],
            out_specs=pl.BlockSpec((d_model, tn), lambda j, ia_r, ib_r, sh_r, nv_r: (0, j))),
        compiler_params=_cparams(("arbitrary",)),
        name="pack_w_in",
    )(jnp.asarray(ia, I32), jnp.asarray(ib, I32), jnp.asarray(sh, I32), jnp.asarray(nv, I32),
      w_in_bf, w_in_bf)


def _mixer(x2, b, s, positions, rel_bias, norm_g, w_in_bf, layer, w_br_l, w_o_l, gn_w, gn_b, conv_w,
           q_norm, k_norm):
    t, d_model = x2.shape
    mix_w = d_model // 2
    iq_w = IDX_HEADS * IDX_DH
    gate_col0 = 10 * mix_w
    iq_col0 = gate_col0 + N_BRANCH * d_model
    assert iq_col0 % iq_w == 0

    h = _rmsnorm(x2, norm_g)
    proj2 = _matmul(h, _pack_w_in(w_in_bf, layer, d_model, mix_w), BF16, 2048, 512, name="in_proj")
    proj3 = proj2.reshape(b, s, proj2.shape[1])
    pos_q = positions.reshape(b, s, 1)
    pos_k = positions.reshape(b, 1, s)

    y_ret = _retention_branch(proj3, pos_q, gn_w, gn_b, mix_w)
    y_conv = _conv_branch(proj3, conv_w, mix_w)

    ik0 = iq_col0 + iq_w
    kit = jnp.swapaxes(proj3[:, :, ik0:ik0 + IDX_DH], 1, 2)
    mask = _dsa_select(proj3, kit, s, min(TOPK_MAX, s // 4), iq_col0 // iq_w)
    qn, kn = _qk_norm(proj3, q_norm, k_norm, mix_w)
    y_att = _dsa_attention(qn, kn, proj3, mask, pos_q, pos_k, rel_bias, mix_w)

    ys = [y.reshape(t, mix_w) for y in (y_ret, y_conv, y_att)]
    merged = _branch_merge(ys, w_br_l.astype(BF16), proj2, d_model, mix_w, gate_col0)
    return _matmul(merged, w_o_l.astype(BF16), F32, 2048, 512, res=x2, name="out_proj")


def kernel(x, positions, rel_bias, norm_mix, norm_ffn, w_in, w_br, w_o, ret_gn_w, ret_gn_b, conv_w,
           q_norm, k_norm, ffn_w1, ffn_w3, ffn_w2, moe_router, moe_w1, moe_w3, moe_w2):
    b, s, d_model = x.shape
    depth = w_in.shape[0]
    x2 = x.reshape(b * s, d_model)
    w_in_bf = w_in.astype(BF16)
    for l in range(depth):
        x2 = _mixer(x2, b, s, positions, rel_bias, norm_mix[l], w_in_bf, l, w_br[l], w_o[l],
                    ret_gn_w[l], ret_gn_b[l], conv_w[l], q_norm[l], k_norm[l])
        if l % 2 == 0:
            i = l // 2
            h = _rmsnorm(x2, norm_ffn[l])
            u = _glu(h, ffn_w1[i].astype(BF16), ffn_w3[i].astype(BF16))
            x2 = _matmul(u, ffn_w2[i].astype(BF16), F32, 1024, 512, res=x2, name="ffn_down")
        else:
            i = l // 2
            x2 = _moe(x2, norm_ffn[l], moe_router[i], moe_w1[i], moe_w3[i], moe_w2[i].astype(BF16))
    return x2.reshape(b, s, d_model)
```

```python
import functools
import math

import numpy as np
import jax
import jax.numpy as jnp
from jax import lax
from jax.experimental import pallas as pl
from jax.experimental.pallas import tpu as pltpu

F32 = jnp.float32
BF16 = jnp.bfloat16
I32 = jnp.int32

EPS = 1e-6
N_BRANCH = 3
RET_HEADS = 8
RET_CHUNK = 128
RET_ROPE_BASE = 10000.0
CONV_K = 3
ATT_HEADS = 8
IDX_HEADS = 8
IDX_DH = 64
TOPK_MAX = 256
REL_BUCKETS = 32
REL_MAX_DIST = 128
TOP_K = 2

LANES = 128
SUBLANES = 8
VMEM_LIMIT_BYTES = 56 * 1024 * 1024
ROW_DMA_UNROLL = 8

TM_ROWWISE = 512
TILE_IN_PROJ = (2048, 1024)
TILE_OUT_PROJ = (2048, 512)
TILE_FFN_DOWN = (1024, 512)
TILE_FFN_UP = (1024, 512)
TILE_MERGE = (1024, 512)
TQ_SELECT, CK_SELECT = 256, 512
T_ATTN = 512
TM_MOE, TN_MOE = 512, 512
ROWS_GATHER = 256

NEG_MASK = -1e30
LOG2E = math.log2(math.e)
INT_MIN = -2 ** 31
KEY_BITS = 32


def _cparams(sem):
    return pltpu.CompilerParams(dimension_semantics=sem, vmem_limit_bytes=VMEM_LIMIT_BYTES)


def _sigmoid(x):
    return 1.0 / (1.0 + jnp.exp(-x))


def _tile(n, pref):
    t = min(n, pref)
    assert n % t == 0, (n, pref)
    return t


def _rmsnorm_kernel(x_ref, g_ref, o_ref):
    x = x_ref[...]
    ms = jnp.mean(x * x, axis=-1, keepdims=True)
    o_ref[...] = (x * lax.rsqrt(ms + EPS) * g_ref[...]).astype(o_ref.dtype)


def _rmsnorm(x2, g):
    t, d = x2.shape
    tm = _tile(t, TM_ROWWISE)
    return pl.pallas_call(
        _rmsnorm_kernel,
        out_shape=jax.ShapeDtypeStruct((t, d), BF16),
        grid=(t // tm,),
        in_specs=[pl.BlockSpec((tm, d), lambda i: (i, 0)),
                  pl.BlockSpec((1, d), lambda i: (0, 0))],
        out_specs=pl.BlockSpec((tm, d), lambda i: (i, 0)),
        compiler_params=_cparams(("parallel",)),
        name="rmsnorm",
    )(x2, g.reshape(1, d))


def _mm_kernel(a_ref, b_ref, o_ref):
    o_ref[...] = jnp.dot(a_ref[...], b_ref[...], preferred_element_type=F32).astype(o_ref.dtype)


def _mm_res_kernel(a_ref, b_ref, r_ref, o_ref):
    o_ref[...] = (r_ref[...] + jnp.dot(a_ref[...], b_ref[...], preferred_element_type=F32)).astype(o_ref.dtype)


def _matmul(a, b, out_dtype, tm, tn, res=None, name="matmul"):
    m, k = a.shape
    k2, n = b.shape
    assert k == k2
    tm = _tile(m, tm)
    tn = _tile(n, tn)
    in_specs = [pl.BlockSpec((tm, k), lambda i, j: (i, 0)),
                pl.BlockSpec((k, tn), lambda i, j: (0, j))]
    args = [a, b]
    kern = _mm_kernel
    if res is not None:
        in_specs.append(pl.BlockSpec((tm, tn), lambda i, j: (i, j)))
        args.append(res)
        kern = _mm_res_kernel
    return pl.pallas_call(
        kern,
        out_shape=jax.ShapeDtypeStruct((m, n), out_dtype),
        grid=(m // tm, n // tn),
        in_specs=in_specs,
        out_specs=pl.BlockSpec((tm, tn), lambda i, j: (i, j)),
        compiler_params=_cparams(("parallel", "parallel")),
        name=name,
    )(*args)


def _ret_kernel(pos_ref, invf_ref, q_ref, k_ref, v_ref, g_ref, dec_ref, kdec_ref, qdec_ref,
                gnw_ref, gnb_ref, o_ref, state_ref, *, heads, dh, chunk_decay):
    @pl.when(pl.program_id(1) == 0)
    def _():
        state_ref[...] = jnp.zeros_like(state_ref)

    c = q_ref.shape[1]
    ang = pos_ref[0].astype(F32) * invf_ref[...]
    cos = jnp.cos(ang)
    sin = jnp.sin(ang)
    lane = lax.broadcasted_iota(I32, (c, dh), 1)
    sin_signed = jnp.where(lane < dh // 2, -sin, sin)
    scale = dh ** -0.5
    for h in range(heads):
        sl = slice(h * dh, (h + 1) * dh)
        q = q_ref[0, :, sl].astype(F32)
        k = k_ref[0, :, sl].astype(F32)
        v = v_ref[0, :, sl]
        qr = (q * cos + pltpu.roll(q, dh // 2, 1) * sin_signed).astype(BF16)
        kr = (k * cos + pltpu.roll(k, dh // 2, 1) * sin_signed) * scale
        scores = lax.dot_general(qr, kr.astype(BF16), (((1,), (1,)), ((), ())),
                                 preferred_element_type=F32) * dec_ref[h]
        intra = jnp.dot(scores.astype(BF16), v, preferred_element_type=F32)
        prev = state_ref[h]
        cross = jnp.dot(qr, prev.astype(BF16), preferred_element_type=F32) * qdec_ref[h]
        kd = (kr * kdec_ref[h]).astype(BF16)
        kv = lax.dot_general(kd, v, (((0,), (0,)), ((), ())), preferred_element_type=F32)
        state_ref[h] = prev * chunk_decay[h] + kv
        ret = intra + cross
        mu = jnp.mean(ret, axis=-1, keepdims=True)
        cen = ret - mu
        var = jnp.mean(cen * cen, axis=-1, keepdims=True)
        gn = cen * lax.rsqrt(var + EPS) * gnw_ref[:, sl] + gnb_ref[:, sl]
        g = g_ref[0, :, sl].astype(F32)
        o_ref[0, :, sl] = (g * _sigmoid(g) * gn).astype(o_ref.dtype)


def _retention_branch(proj3, pos3, gn_w, gn_b, mix_w):
    b, s, _ = proj3.shape
    heads, c = RET_HEADS, RET_CHUNK
    dh = mix_w // heads
    assert dh == LANES and s % c == 0
    log_g = np.log1p(-np.exp2(-5.0 - np.arange(heads, dtype=np.float64)))
    pos = np.arange(c, dtype=np.float64)
    diff = pos[:, None] - pos[None, :]
    intra_decay = np.where(diff >= 0, np.exp(np.maximum(diff, 0.0)[None] * log_g[:, None, None]), 0.0)
    k_decay = np.exp((c - 1 - pos)[None, :] * log_g[:, None])
    q_decay = np.exp((pos + 1)[None, :] * log_g[:, None])
    chunk_decay = tuple(float(x) for x in np.exp(c * log_g))
    dec = jnp.asarray(intra_decay, F32)
    kdec = jnp.asarray(np.broadcast_to(k_decay[:, :, None], (heads, c, dh)), F32)
    qdec = jnp.asarray(np.broadcast_to(q_decay[:, :, None], (heads, c, dh)), F32)
    inv_freq = RET_ROPE_BASE ** (-jnp.arange(0, dh, 2, dtype=F32) / dh)
    invf = jnp.concatenate([inv_freq, inv_freq]).reshape(1, dh)

    def col(cb):
        return pl.BlockSpec((1, c, mix_w), lambda bi, n, cb=cb: (bi, n, cb))

    const3 = pl.BlockSpec((heads, c, dh), lambda bi, n: (0, 0, 0))
    return pl.pallas_call(
        functools.partial(_ret_kernel, heads=heads, dh=dh, chunk_decay=chunk_decay),
        out_shape=jax.ShapeDtypeStruct((b, s, mix_w), BF16),
        grid=(b, s // c),
        in_specs=[pl.BlockSpec((1, c, 1), lambda bi, n: (bi, n, 0)),
                  pl.BlockSpec((1, dh), lambda bi, n: (0, 0)),
                  col(0), col(1), col(2), col(3),
                  const3, const3, const3,
                  pl.BlockSpec((1, mix_w), lambda bi, n: (0, 0)),
                  pl.BlockSpec((1, mix_w), lambda bi, n: (0, 0))],
        out_specs=pl.BlockSpec((1, c, mix_w), lambda bi, n: (bi, n, 0)),
        scratch_shapes=[pltpu.VMEM((heads, dh, dh), F32)],
        compiler_params=_cparams(("arbitrary", "arbitrary")),
        name="retention",
    )(pos3, invf, proj3, proj3, proj3, proj3, dec, kdec, qdec,
      gn_w.reshape(1, mix_w), gn_b.reshape(1, mix_w))


def _conv_kernel(b_ref, c_ref, u_ref, w_ref, o_ref, ubuf_ref):
    ts = o_ref.shape[1]
    halo = SUBLANES

    @pl.when(pl.program_id(1) == 0)
    def _():
        ubuf_ref[0:halo, :] = jnp.zeros((halo, ubuf_ref.shape[1]), F32)

    u = c_ref[0].astype(F32) * u_ref[0].astype(F32)
    ubuf_ref[halo:halo + ts, :] = u
    u1 = ubuf_ref[halo - 1:halo - 1 + ts, :]
    u2 = ubuf_ref[halo - 2:halo - 2 + ts, :]
    conv = w_ref[0:1, :] * u2 + w_ref[1:2, :] * u1 + w_ref[2:3, :] * u
    o_ref[0] = (b_ref[0].astype(F32) * conv).astype(o_ref.dtype)
    ubuf_ref[0:halo, :] = ubuf_ref[ts:ts + halo, :]


def _conv_branch(proj3, conv_w, mix_w):
    b, s, _ = proj3.shape
    ts = _tile(s, TM_ROWWISE)

    def col(cb):
        return pl.BlockSpec((1, ts, mix_w), lambda bi, n, cb=cb: (bi, n, cb))

    return pl.pallas_call(
        _conv_kernel,
        out_shape=jax.ShapeDtypeStruct((b, s, mix_w), BF16),
        grid=(b, s // ts),
        in_specs=[col(4), col(5), col(6),
                  pl.BlockSpec((CONV_K, mix_w), lambda bi, n: (0, 0))],
        out_specs=pl.BlockSpec((1, ts, mix_w), lambda bi, n: (bi, n, 0)),
        scratch_shapes=[pltpu.VMEM((ts + SUBLANES, mix_w), F32)],
        compiler_params=_cparams(("arbitrary", "arbitrary")),
        name="short_conv",
    )(proj3, proj3, proj3, conv_w)


def _qknorm_kernel(q_ref, k_ref, qn_ref, kn_ref, qo_ref, ko_ref, *, heads, dh, q_scale):
    for h in range(heads):
        sl = slice(h * dh, (h + 1) * dh)
        q = q_ref[0, :, sl].astype(F32)
        k = k_ref[0, :, sl].astype(F32)
        qy = q * lax.rsqrt(jnp.mean(q * q, axis=-1, keepdims=True) + EPS) * qn_ref[...]
        ky = k * lax.rsqrt(jnp.mean(k * k, axis=-1, keepdims=True) + EPS) * kn_ref[...]
        qo_ref[0, :, sl] = (qy * q_scale).astype(qo_ref.dtype)
        ko_ref[0, :, sl] = ky.astype(ko_ref.dtype)


def _qk_norm(proj3, q_norm, k_norm, mix_w):
    b, s, _ = proj3.shape
    heads = ATT_HEADS
    dh = mix_w // heads
    ts = _tile(s, TM_ROWWISE)
    out = jax.ShapeDtypeStruct((b, s, mix_w), BF16)
    blk = pl.BlockSpec((1, ts, mix_w), lambda bi, n: (bi, n, 0))
    return pl.pallas_call(
        functools.partial(_qknorm_kernel, heads=heads, dh=dh, q_scale=dh ** -0.5 * LOG2E),
        out_shape=(out, out),
        grid=(b, s // ts),
        in_specs=[pl.BlockSpec((1, ts, mix_w), lambda bi, n: (bi, n, 7)),
                  pl.BlockSpec((1, ts, mix_w), lambda bi, n: (bi, n, 8)),
                  pl.BlockSpec((1, dh), lambda bi, n: (0, 0)),
                  pl.BlockSpec((1, dh), lambda bi, n: (0, 0))],
        out_specs=(blk, blk),
        compiler_params=_cparams(("parallel", "parallel")),
        name="qk_norm",
    )(proj3, proj3, q_norm.reshape(1, dh), k_norm.reshape(1, dh))


def _select_kernel(iq_ref, sm_ref, kit_ref, tri_ref, o_ref, qall_ref, keys_ref, planes_ref, eq_ref, *,
                   tq, ck, ksel, heads, d_idx):
    qb = pl.program_id(1)
    row0 = qb * tq
    nch = (row0 + tq + ck - 1) // ck
    grp = KEY_BITS * LANES
    ngrp = keys_ref.shape[1] // grp
    nlive = (row0 + tq + grp - 1) // grp
    for h in range(heads):
        qall_ref[h * tq:(h + 1) * tq, :] = iq_ref[0, :, h * d_idx:(h + 1) * d_idx]
    w = sm_ref[0, :, d_idx:d_idx + heads].astype(F32) * (heads ** -0.5 * d_idx ** -0.5)
    t_col = row0 + lax.broadcasted_iota(I32, (tq, 1), 0)
    lane_ck = lax.broadcasted_iota(I32, (1, ck), 1)

    def score_chunk(c, carry):
        off = pl.multiple_of(c * ck, ck)
        kt = kit_ref[0, :, pl.ds(off, ck)]
        sc = jnp.zeros((tq, ck), F32)
        for h in range(heads):
            d = jnp.dot(qall_ref[h * tq:(h + 1) * tq, :], kt, preferred_element_type=F32)
            sc = sc + w[:, h:h + 1] * jnp.maximum(d, 0.0)
        bits = pltpu.bitcast(sc, I32)
        key = bits ^ ((bits >> 31) & 0x7FFFFFFF)
        keys_ref[:, pl.ds(off, ck)] = jnp.where(off + lane_ck <= t_col, key, INT_MIN)
        return carry

    def clear_chunk(c, carry):
        keys_ref[:, pl.ds(pl.multiple_of(c * ck, ck), ck)] = jnp.full((tq, ck), INT_MIN, I32)
        return carry

    lax.fori_loop(0, nch, score_chunk, 0)
    lax.fori_loop(nch, nlive * (grp // ck), clear_chunk, 0)

    def transpose_group(g):
        def rows(r, carry):
            rs = pl.ds(pl.multiple_of(r * SUBLANES, SUBLANES), SUBLANES)
            a = [keys_ref[rs, (g * KEY_BITS + j) * LANES:(g * KEY_BITS + j + 1) * LANES] ^ INT_MIN
                 for j in range(KEY_BITS)]
            j, m = KEY_BITS // 2, (1 << (KEY_BITS // 2)) - 1
            while j:
                k = 0
                while k < KEY_BITS:
                    t = (a[k] ^ (a[k + j] >> j)) & m
                    a[k] = a[k] ^ t
                    a[k + j] = a[k + j] ^ (t << j)
                    k = (k + j + 1) & ~j
                j >>= 1
                m = m ^ (m << j)
            for p in range(KEY_BITS):
                planes_ref[p, g, rs, :] = a[p]
            return carry

        lax.fori_loop(0, tq // SUBLANES, rows, 0)

    @pl.when(qb == 0)
    def _():
        planes_ref[...] = jnp.zeros_like(planes_ref)

    for g in range(ngrp):
        @pl.when(g < nlive)
        def _(g=g):
            transpose_group(g)
            eq_ref[g] = jnp.full(eq_ref.shape[1:], -1, I32)

        @pl.when(g >= nlive)
        def _(g=g):
            eq_ref[g] = jnp.zeros(eq_ref.shape[1:], I32)

    kf = float(ksel)

    def bit_pass(p, carry):
        above, thr_u = carry
        ones = jnp.zeros((tq, LANES), I32)
        for g in range(ngrp):
            ones = ones + lax.population_count(eq_ref[g] & planes_ref[p, g])
        c1 = jnp.sum(ones.astype(F32), axis=1, keepdims=True)
        take = (above + c1) >= kf
        for g in range(ngrp):
            e = eq_ref[g]
            x = e & planes_ref[p, g]
            eq_ref[g] = jnp.where(take, x, e ^ x)
        bit = jnp.left_shift(jnp.int32(1), KEY_BITS - 1 - p)
        return jnp.where(take, above, above + c1), jnp.where(take, thr_u | bit, thr_u)

    cnt_gt, thr_u = lax.fori_loop(0, KEY_BITS, bit_pass,
                                  (jnp.zeros((tq, 1), F32), jnp.zeros((tq, 1), I32)))
    thr = thr_u ^ INT_MIN
    eq_cnt = jnp.zeros((tq, LANES), I32)
    for g in range(ngrp):
        eq_cnt = eq_cnt + lax.population_count(eq_ref[g])
    cnt_eq = jnp.sum(eq_cnt.astype(F32), axis=1, keepdims=True)

    need = kf - cnt_gt
    excess = jnp.where(thr == INT_MIN, 0.0, cnt_eq - need)
    o_ref[...] = jnp.full(o_ref.shape, NEG_MASK, o_ref.dtype)

    def write_chunk(c, eq_before, ties):
        off = pl.multiple_of(c * ck, ck)
        sl = pl.ds(off, ck)
        k = keys_ref[:, sl]
        if ties:
            eq = jnp.where(k == thr, 1.0, 0.0).astype(BF16)
            rank = eq_before + jnp.dot(eq, tri_ref[...], preferred_element_type=F32)
            on_eq = jnp.where(rank <= need, 0.0, NEG_MASK)
            eq_before = rank[:, ck - 1:ck]
        else:
            on_eq = 0.0
        val = jnp.where(k > thr, 0.0, jnp.where(k == thr, on_eq, NEG_MASK))
        o_ref[0, :, sl] = jnp.where(off + lane_ck <= t_col, val, NEG_MASK).astype(o_ref.dtype)
        return eq_before

    any_excess = jnp.max(excess) > 0.0

    @pl.when(any_excess)
    def _():
        lax.fori_loop(0, nch, lambda c, e: write_chunk(c, e, True), jnp.zeros((tq, 1), F32))

    @pl.when(jnp.logical_not(any_excess))
    def _():
        lax.fori_loop(0, nch, lambda c, e: write_chunk(c, e, False), 0)


def _dsa_select(proj3, kit, s, ksel, iq_block):
    b = proj3.shape[0]
    tq = _tile(s, TQ_SELECT)
    ck = _tile(s, CK_SELECT)
    heads, d_idx = IDX_HEADS, IDX_DH
    iq_w = heads * d_idx
    assert (s % (KEY_BITS * LANES) == 0 or s < KEY_BITS * LANES) and (KEY_BITS * LANES) % ck == 0
    sp = max(s, KEY_BITS * LANES)
    tri = (jnp.arange(ck)[:, None] <= jnp.arange(ck)[None, :]).astype(BF16)
    return pl.pallas_call(
        functools.partial(_select_kernel, tq=tq, ck=ck, ksel=ksel, heads=heads, d_idx=d_idx),
        out_shape=jax.ShapeDtypeStruct((b, s, s), BF16),
        grid=(b, s // tq),
        in_specs=[pl.BlockSpec((1, tq, iq_w), lambda bi, n: (bi, n, iq_block)),
                  pl.BlockSpec((1, tq, iq_w), lambda bi, n: (bi, n, iq_block + 1)),
                  pl.BlockSpec((1, d_idx, s), lambda bi, n: (bi, 0, 0)),
                  pl.BlockSpec((ck, ck), lambda bi, n: (0, 0))],
        out_specs=pl.BlockSpec((1, tq, s), lambda bi, n: (bi, n, 0)),
        scratch_shapes=[pltpu.VMEM((heads * tq, d_idx), BF16),
                        pltpu.VMEM((tq, sp), I32),
                        pltpu.VMEM((KEY_BITS, sp // (KEY_BITS * LANES), tq, LANES), I32),
                        pltpu.VMEM((sp // (KEY_BITS * LANES), tq, LANES), I32)],
        compiler_params=_cparams(("arbitrary", "arbitrary")),
        name="dsa_select",
    )(proj3, proj3, kit, tri)


def _t5_bucket(n):
    max_exact = REL_BUCKETS // 2
    nf = jnp.maximum(n, 1).astype(F32)
    large = max_exact + (jnp.log(nf / max_exact) / math.log(REL_MAX_DIST / max_exact)
                         * (REL_BUCKETS - max_exact)).astype(I32)
    large = jnp.minimum(large, REL_BUCKETS - 1)
    return jnp.where(n < max_exact, n, large)


def _attn_kernel(qb_ref, kb_ref, rb_ref, q_ref, k_ref, v_ref, mask_ref, pq_ref, pk_ref, o_ref,
                 acc_ref, m_ref, l_ref, bias_ref, s_ref, p_ref, *, tq, tk, heads, dh):
    qb = qb_ref[pl.program_id(1)]
    kb = kb_ref[pl.program_id(1)]
    kmax = ((qb + 1) * tq - 1) // tk
    nlt = tk // LANES
    gran = LANES

    @pl.when(kb == 0)
    def _():
        acc_ref[...] = jnp.zeros_like(acc_ref)
        m_ref[...] = jnp.full(m_ref.shape, -jnp.inf, F32)
        l_ref[...] = jnp.zeros_like(l_ref)

    def attend(bias_of):
        ones = jnp.ones((tk, LANES), BF16)
        for h in range(heads):
            sl = slice(h * dh, (h + 1) * dh)
            kh = k_ref[0, :, sl]
            v_aug = jnp.concatenate([v_ref[0, :, sl], ones], axis=1)
            add, const = bias_of(h)
            buf = h % 2
            s = lax.dot_general(q_ref[0, :, sl], kh, (((1,), (1,)), ((), ())),
                                preferred_element_type=F32) + add
            s_ref[buf] = s
            tile_max = s[:, 0:LANES]
            for c in range(1, nlt):
                tile_max = jnp.maximum(tile_max, s[:, c * LANES:(c + 1) * LANES])
            m_cur = jnp.broadcast_to(jnp.max(tile_max, axis=1, keepdims=True), (tq, LANES)) + const
            m_old = m_ref[h]
            m_new = jnp.maximum(m_old, m_cur)
            alpha = jnp.exp2(m_old - m_new)
            shift = m_new - const
            for c in range(nlt):
                cs = slice(c * LANES, (c + 1) * LANES)
                p_ref[buf, :, cs] = jnp.exp2(s_ref[buf, :, cs] - shift).astype(BF16)
            pv = jnp.dot(p_ref[buf], v_aug, preferred_element_type=F32)
            m_ref[h] = m_new
            l_ref[h] = alpha * l_ref[h] + pv[:, dh:]
            acc_ref[:, sl] = alpha * acc_ref[:, sl] + pv[:, :dh]

    @pl.when(kb <= kmax)
    def _():
        pq = pq_ref[0]
        pk = pk_ref[0]
        all_far = (jnp.min(pq) - jnp.max(pk)) >= REL_MAX_DIST

        @pl.when(all_far)
        def _():
            attend(lambda h: (mask_ref[0].astype(F32), rb_ref[REL_BUCKETS - 1, h] * LOG2E))

        @pl.when(jnp.logical_not(all_far))
        def _():
            n_lane = lax.broadcasted_iota(I32, (1, REL_MAX_DIST), 1)
            bucket = _t5_bucket(n_lane)
            tabs = []
            for h in range(heads):
                tab = jnp.zeros((1, REL_MAX_DIST), F32)
                for j in range(REL_BUCKETS):
                    tab = jnp.where(bucket == j, rb_ref[j, h] * LOG2E, tab)
                tabs.append(jnp.broadcast_to(tab, (gran, REL_MAX_DIST)))
            k_lo = [jnp.min(pk[:, cj * gran:(cj + 1) * gran]) for cj in range(tk // gran)]
            k_hi = [jnp.max(pk[:, cj * gran:(cj + 1) * gran]) for cj in range(tk // gran)]
            for ri in range(tq // gran):
                rs = slice(ri * gran, (ri + 1) * gran)
                pq_g = pq[rs]
                q_lo = jnp.min(pq_g)
                q_hi = jnp.max(pq_g)
                for cj in range(tk // gran):
                    cs = slice(cj * gran, (cj + 1) * gran)
                    pk_g = pk[:, cs]
                    lo = q_lo - k_hi[cj]
                    hi = q_hi - k_lo[cj]
                    is_far = lo >= REL_MAX_DIST
                    is_zero = hi <= 0
                    maskf = mask_ref[0, rs, cs].astype(F32)

                    @pl.when(is_far | is_zero)
                    def _():
                        for h in range(heads):
                            c_h = jnp.where(is_far, rb_ref[REL_BUCKETS - 1, h], rb_ref[0, h]) * LOG2E
                            bias_ref[h, rs, cs] = maskf + c_h

                    @pl.when(jnp.logical_not(is_far | is_zero))
                    def _():
                        dist = jnp.clip(pq_g - pk_g, 0, REL_MAX_DIST - 1)
                        for h in range(heads):
                            bias_ref[h, rs, cs] = maskf + jnp.take_along_axis(tabs[h], dist, axis=1)
            attend(lambda h: (bias_ref[h], 0.0))

    @pl.when(kb == kmax)
    def _():
        for h in range(heads):
            sl = slice(h * dh, (h + 1) * dh)
            o_ref[0, :, sl] = (acc_ref[:, sl] / l_ref[h]).astype(o_ref.dtype)


def _dsa_attention(qn, kn, proj3, mask, pos_q, pos_k, rel_bias, mix_w):
    b, s, _ = qn.shape
    heads = ATT_HEADS
    dh = mix_w // heads
    tq = _tile(s, T_ATTN)
    tk = tq
    assert REL_MAX_DIST == LANES

    tiles = [(qb, kb) for qb in range(s // tq) for kb in range(((qb + 1) * tq - 1) // tk + 1)]
    qb_of = jnp.asarray([t[0] for t in tiles], I32)
    kb_of = jnp.asarray([t[1] for t in tiles], I32)

    return pl.pallas_call(
        functools.partial(_attn_kernel, tq=tq, tk=tk, heads=heads, dh=dh),
        out_shape=jax.ShapeDtypeStruct((b, s, mix_w), BF16),
        grid_spec=pltpu.PrefetchScalarGridSpec(
            num_scalar_prefetch=2,
            grid=(b, len(tiles)),
            in_specs=[pl.BlockSpec(memory_space=pltpu.SMEM),
                      pl.BlockSpec((1, tq, mix_w), lambda bi, st, qo, ko: (bi, qo[st], 0)),
                      pl.BlockSpec((1, tk, mix_w), lambda bi, st, qo, ko: (bi, ko[st], 0)),
                      pl.BlockSpec((1, tk, mix_w), lambda bi, st, qo, ko: (bi, ko[st], 9)),
                      pl.BlockSpec((1, tq, tk), lambda bi, st, qo, ko: (bi, qo[st], ko[st])),
                      pl.BlockSpec((1, tq, 1), lambda bi, st, qo, ko: (bi, qo[st], 0)),
                      pl.BlockSpec((1, 1, tk), lambda bi, st, qo, ko: (bi, 0, ko[st]))],
            out_specs=pl.BlockSpec((1, tq, mix_w), lambda bi, st, qo, ko: (bi, qo[st], 0)),
            scratch_shapes=[pltpu.VMEM((tq, mix_w), F32),
                            pltpu.VMEM((heads, tq, LANES), F32),
                            pltpu.VMEM((heads, tq, LANES), F32),
                            pltpu.VMEM((heads, tq, tk), F32),
                            pltpu.VMEM((2, tq, tk), F32),
                            pltpu.VMEM((2, tq, tk), BF16)]),
        compiler_params=_cparams(("parallel", "arbitrary")),
        name="dsa_attention",
    )(qb_of, kb_of, rel_bias, qn, kn, proj3, mask, pos_q, pos_k)


def _merge_kernel(y0_ref, y1_ref, y2_ref, w_ref, g0_ref, g1_ref, g2_ref, o_ref):
    acc = None
    for n, (y_ref, g_ref) in enumerate(((y0_ref, g0_ref), (y1_ref, g1_ref), (y2_ref, g2_ref))):
        br = jnp.dot(y_ref[...], w_ref[n], preferred_element_type=F32)
        term = _sigmoid(g_ref[...].astype(F32)) * br
        acc = term if acc is None else acc + term
    o_ref[...] = acc.astype(o_ref.dtype)


def _branch_merge(ys, w_br, proj2, d_model, mix_w, gate_col0):
    t = proj2.shape[0]
    tm = _tile(t, TILE_MERGE[0])
    tn = _tile(d_model, TILE_MERGE[1])
    yspec = pl.BlockSpec((tm, mix_w), lambda i, j: (i, 0))

    def gspec(n):
        base = (gate_col0 + n * d_model) // tn
        return pl.BlockSpec((tm, tn), lambda i, j, base=base: (i, base + j))

    return pl.pallas_call(
        _merge_kernel,
        out_shape=jax.ShapeDtypeStruct((t, d_model), BF16),
        grid=(t // tm, d_model // tn),
        in_specs=[yspec, yspec, yspec,
                  pl.BlockSpec((N_BRANCH, mix_w, tn), lambda i, j: (0, 0, j)),
                  gspec(0), gspec(1), gspec(2)],
        out_specs=pl.BlockSpec((tm, tn), lambda i, j: (i, j)),
        compiler_params=_cparams(("parallel", "parallel")),
        name="branch_merge",
    )(ys[0], ys[1], ys[2], w_br, proj2, proj2, proj2)


def _glu_kernel(x_ref, g_ref, w1_ref, w3_ref, o_ref, a_ref):
    @pl.when(pl.program_id(1) == 0)
    def _():
        x = x_ref[...]
        ms = jnp.mean(x * x, axis=-1, keepdims=True)
        a_ref[...] = (x * lax.rsqrt(ms + EPS) * g_ref[...]).astype(a_ref.dtype)

    a = a_ref[...]
    h1 = jnp.dot(a, w1_ref[...], preferred_element_type=F32)
    h3 = jnp.dot(a, w3_ref[...], preferred_element_type=F32)
    o_ref[...] = (h1 * _sigmoid(h1) * h3).astype(o_ref.dtype)


def _glu(x2, g, w1, w3):
    t, d = x2.shape
    ff = w1.shape[1]
    tm = _tile(t, TILE_FFN_UP[0])
    tn = _tile(ff, TILE_FFN_UP[1])
    return pl.pallas_call(
        _glu_kernel,
        out_shape=jax.ShapeDtypeStruct((t, ff), BF16),
        grid=(t // tm, ff // tn),
        in_specs=[pl.BlockSpec((tm, d), lambda i, j: (i, 0)),
                  pl.BlockSpec((1, d), lambda i, j: (0, 0)),
                  pl.BlockSpec((d, tn), lambda i, j: (0, j)),
                  pl.BlockSpec((d, tn), lambda i, j: (0, j))],
        out_specs=pl.BlockSpec((tm, tn), lambda i, j: (i, j)),
        scratch_shapes=[pltpu.VMEM((tm, d), BF16)],
        compiler_params=_cparams(("parallel", "arbitrary")),
        name="swiglu_up",
    )(x2, g.reshape(1, d), w1, w3)


def _router_kernel(x_ref, g_ref, r_ref, comb_ref, sel_ref, *, n_exp):
    x = x_ref[...]
    h = (x * lax.rsqrt(jnp.mean(x * x, axis=-1, keepdims=True) + EPS) * g_ref[...]).astype(BF16)
    logits = jnp.dot(h, r_ref[...], preferred_element_type=F32)
    lane = lax.broadcasted_iota(I32, logits.shape, 1)
    lg = jnp.where(lane < n_exp, logits, -jnp.inf)
    v1 = jnp.max(lg, axis=1, keepdims=True)
    i1 = jnp.min(jnp.where(lg == v1, lane, LANES), axis=1, keepdims=True)
    lg2 = jnp.where(lane == i1, -jnp.inf, lg)
    v2 = jnp.max(lg2, axis=1, keepdims=True)
    i2 = jnp.min(jnp.where(lg2 == v2, lane, LANES), axis=1, keepdims=True)
    e = jnp.exp(v2 - v1)
    w1 = 1.0 / (1.0 + e)
    w2 = e / (1.0 + e)
    comb_ref[...] = jnp.where(lane == i1, w1, jnp.where(lane == i2, w2, 0.0))
    sel_ref[...] = jnp.where((lane == i1) | (lane == i2), 1, 0)


def _router(x2, g, router_w):
    t, d = x2.shape
    n_exp = router_w.shape[1]
    tm = _tile(t, TM_ROWWISE)
    rpad = jnp.zeros((d, LANES), BF16).at[:, :n_exp].set(router_w.astype(BF16))
    comb, sel = pl.pallas_call(
        functools.partial(_router_kernel, n_exp=n_exp),
        out_shape=(jax.ShapeDtypeStruct((t, LANES), F32), jax.ShapeDtypeStruct((t, LANES), I32)),
        grid=(t // tm,),
        in_specs=[pl.BlockSpec((tm, d), lambda i: (i, 0)),
                  pl.BlockSpec((1, d), lambda i: (0, 0)),
                  pl.BlockSpec((d, LANES), lambda i: (0, 0))],
        out_specs=(pl.BlockSpec((tm, LANES), lambda i: (i, 0)),
                   pl.BlockSpec((tm, LANES), lambda i: (i, 0))),
        compiler_params=_cparams(("parallel",)),
        name="moe_router",
    )(x2, g.reshape(1, d), rpad)
    return comb[:, :n_exp], sel[:, :n_exp]


def _row_gather(idx_ref, src_hbm, buf_ref, sem, slot, rows, start):
    def body(r8, carry):
        for u in range(ROW_DMA_UNROLL):
            r = r8 * ROW_DMA_UNROLL + u
            cp = pltpu.make_async_copy(src_hbm.at[pl.ds(idx_ref[0, 0, r], 1)],
                                       buf_ref.at[slot, pl.ds(r, 1)], sem.at[slot])
            if start:
                cp.start(priority=u % 2)
            else:
                cp.wait()
        return carry

    lax.fori_loop(0, rows // ROW_DMA_UNROLL, body, 0)


def _dispatch_kernel(cur_ref, nxt_ref, x_hbm, g_ref, o_ref, buf_ref, sem, *, rows, nsteps):
    i = pl.program_id(0)
    slot = i % 2

    @pl.when(i == 0)
    def _():
        _row_gather(cur_ref, x_hbm, buf_ref, sem, 0, rows, start=True)

    @pl.when(i + 1 < nsteps)
    def _():
        _row_gather(nxt_ref, x_hbm, buf_ref, sem, 1 - slot, rows, start=True)

    _row_gather(cur_ref, x_hbm, buf_ref, sem, slot, rows, start=False)
    x = buf_ref[slot]
    o_ref[...] = (x * lax.rsqrt(jnp.mean(x * x, axis=-1, keepdims=True) + EPS) * g_ref[...]).astype(o_ref.dtype)


def _dispatch(x2, g, src_tok, rows):
    t, d = x2.shape
    p = src_tok.shape[0]
    assert p % rows == 0 and rows % ROW_DMA_UNROLL == 0
    nsteps = p // rows
    src3 = src_tok.reshape(nsteps, 1, rows)
    return pl.pallas_call(
        functools.partial(_dispatch_kernel, rows=rows, nsteps=nsteps),
        out_shape=jax.ShapeDtypeStruct((p, d), BF16),
        grid=(nsteps,),
        in_specs=[pl.BlockSpec((1, 1, rows), lambda i: (i, 0, 0), memory_space=pltpu.SMEM),
                  pl.BlockSpec((1, 1, rows), lambda i: (jnp.minimum(i + 1, nsteps - 1), 0, 0),
                               memory_space=pltpu.SMEM),
                  pl.BlockSpec(memory_space=pl.ANY),
                  pl.BlockSpec((1, d), lambda i: (0, 0))],
        out_specs=pl.BlockSpec((rows, d), lambda i: (i, 0)),
        scratch_shapes=[pltpu.VMEM((2, rows, d), F32), pltpu.SemaphoreType.DMA((2,))],
        compiler_params=_cparams(("arbitrary",)),
        name="moe_dispatch",
    )(src3, src3, x2, g.reshape(1, d))


def _moe_up_kernel(ti_ref, tj_ref, wj_ref, te_ref, first_ref, valid_ref, a_ref, w1_ref, w3_ref, o_ref,
                   w1b_ref, w3b_ref):
    s = pl.program_id(0)

    @pl.when(first_ref[s] == 1)
    def _():
        w1b_ref[...] = w1_ref[0].astype(BF16)
        w3b_ref[...] = w3_ref[0].astype(BF16)

    @pl.when(valid_ref[s] == 1)
    def _():
        a = a_ref[...]
        h1 = jnp.dot(a, w1b_ref[...], preferred_element_type=F32)
        h3 = jnp.dot(a, w3b_ref[...], preferred_element_type=F32)
        o_ref[...] = (h1 * _sigmoid(h1) * h3).astype(o_ref.dtype)

    @pl.when(valid_ref[s] == 0)
    def _():
        o_ref[...] = jnp.zeros_like(o_ref)


def _moe_up(hs, w1, w3, tile_expert, tile_valid, gfirst, gtiles, tm):
    p, d = hs.shape
    ff = w1.shape[2]
    tn = _tile(ff, TN_MOE)
    nj = ff // tn
    n_tiles = p // tm
    step = jnp.arange(n_tiles * nj, dtype=I32)
    t0 = step // nj
    e_s = tile_expert[t0]
    valid = tile_valid[t0]
    n_run = jnp.maximum(gtiles[e_s], 1)
    local = step - gfirst[e_s] * nj
    ti = jnp.where(valid == 1, gfirst[e_s] + local % n_run, t0).astype(I32)
    tj = jnp.where(valid == 1, local // n_run, step % nj).astype(I32)
    wj = jnp.where(valid == 1, local // n_run, nj - 1).astype(I32)
    first = jnp.where((valid == 1) & (local % n_run == 0), 1, 0).astype(I32)

    def amap(s, ti_r, tj_r, wj_r, te_r, f_r, v_r):
        return (ti_r[s], 0)

    def wmap(s, ti_r, tj_r, wj_r, te_r, f_r, v_r):
        return (te_r[s], 0, wj_r[s])

    def omap(s, ti_r, tj_r, wj_r, te_r, f_r, v_r):
        return (ti_r[s], tj_r[s])

    return pl.pallas_call(
        _moe_up_kernel,
        out_shape=jax.ShapeDtypeStruct((p, ff), BF16),
        grid_spec=pltpu.PrefetchScalarGridSpec(
            num_scalar_prefetch=6,
            grid=(n_tiles * nj,),
            in_specs=[pl.BlockSpec((tm, d), amap),
                      pl.BlockSpec((1, d, tn), wmap),
                      pl.BlockSpec((1, d, tn), wmap)],
            out_specs=pl.BlockSpec((tm, tn), omap),
            scratch_shapes=[pltpu.VMEM((d, tn), BF16), pltpu.VMEM((d, tn), BF16)]),
        compiler_params=_cparams(("arbitrary",)),
        name="moe_up",
    )(ti, tj, wj, e_s.astype(I32), first, valid.astype(I32), hs, w1, w3)


def _moe_down_kernel(te_ref, tv_ref, a_ref, w_ref, o_ref):
    i = pl.program_id(0)

    @pl.when(tv_ref[i] == 1)
    def _():
        o_ref[...] = jnp.dot(a_ref[...], w_ref[0], preferred_element_type=F32)

    @pl.when(tv_ref[i] == 0)
    def _():
        o_ref[...] = jnp.zeros_like(o_ref)


def _moe_down(us, w2, tile_expert, tile_valid, tm):
    p, ff = us.shape
    d = w2.shape[2]
    tn = _tile(d, TN_MOE)
    nj = d // tn

    def wmap(i, j, te, tv):
        return (te[i], 0, jnp.where(tv[i] == 1, j, nj - 1))

    return pl.pallas_call(
        _moe_down_kernel,
        out_shape=jax.ShapeDtypeStruct((p, d), F32),
        grid_spec=pltpu.PrefetchScalarGridSpec(
            num_scalar_prefetch=2,
            grid=(p // tm, nj),
            in_specs=[pl.BlockSpec((tm, ff), lambda i, j, te, tv: (i, 0)),
                      pl.BlockSpec((1, ff, tn), wmap)],
            out_specs=pl.BlockSpec((tm, tn), lambda i, j, te, tv: (i, j))),
        compiler_params=_cparams(("arbitrary", "arbitrary")),
        name="moe_down",
    )(tile_expert, tile_valid, us, w2)


def _combine_kernel(p0_ref, p1_ref, n0_ref, n1_ref, x_ref, w0_ref, w1_ref, ys_hbm, o_ref,
                    a_ref, b_ref, sem_a, sem_b, *, rows, nsteps):
    i = pl.program_id(0)
    slot = i % 2

    @pl.when(i == 0)
    def _():
        _row_gather(p0_ref, ys_hbm, a_ref, sem_a, 0, rows, start=True)
        _row_gather(p1_ref, ys_hbm, b_ref, sem_b, 0, rows, start=True)

    @pl.when(i + 1 < nsteps)
    def _():
        _row_gather(n0_ref, ys_hbm, a_ref, sem_a, 1 - slot, rows, start=True)
        _row_gather(n1_ref, ys_hbm, b_ref, sem_b, 1 - slot, rows, start=True)

    _row_gather(p0_ref, ys_hbm, a_ref, sem_a, slot, rows, start=False)
    _row_gather(p1_ref, ys_hbm, b_ref, sem_b, slot, rows, start=False)
    o_ref[...] = x_ref[...] + (w0_ref[...] * a_ref[slot] + w1_ref[...] * b_ref[slot])


def _combine(x2, ys, pos0, pos1, w0, w1, rows):
    t, d = x2.shape
    assert t % rows == 0 and rows % ROW_DMA_UNROLL == 0
    nsteps = t // rows
    cur = pl.BlockSpec((1, 1, rows), lambda i: (i, 0, 0), memory_space=pltpu.SMEM)
    nxt = pl.BlockSpec((1, 1, rows), lambda i: (jnp.minimum(i + 1, nsteps - 1), 0, 0),
                       memory_space=pltpu.SMEM)
    p0 = pos0.reshape(nsteps, 1, rows)
    p1 = pos1.reshape(nsteps, 1, rows)
    return pl.pallas_call(
        functools.partial(_combine_kernel, rows=rows, nsteps=nsteps),
        out_shape=jax.ShapeDtypeStruct((t, d), F32),
        grid=(nsteps,),
        in_specs=[cur, cur, nxt, nxt,
                  pl.BlockSpec((rows, d), lambda i: (i, 0)),
                  pl.BlockSpec((rows, 1), lambda i: (i, 0)),
                  pl.BlockSpec((rows, 1), lambda i: (i, 0)),
                  pl.BlockSpec(memory_space=pl.ANY)],
        out_specs=pl.BlockSpec((rows, d), lambda i: (i, 0)),
        scratch_shapes=[pltpu.VMEM((2, rows, d), F32), pltpu.VMEM((2, rows, d), F32),
                        pltpu.SemaphoreType.DMA((2,)), pltpu.SemaphoreType.DMA((2,))],
        compiler_params=_cparams(("arbitrary",)),
        name="moe_combine",
    )(p0, p1, p0, p1, x2, w0, w1, ys)


def _moe(x2, g, router_w, w1, w3, w2):
    t, d = x2.shape
    n_exp = router_w.shape[1]
    tm = _tile(t, TM_MOE)
    comb, sel = _router(x2, g, router_w)
    rank = jnp.cumsum(sel, axis=0) - sel
    counts = jnp.sum(sel, axis=0)
    gsize = ((counts + tm - 1) // tm) * tm
    gend = jnp.cumsum(gsize)
    goff = gend - gsize
    dest = goff[None, :] + rank
    e0 = jnp.argmax(sel, axis=1)
    e1 = jnp.max(sel * jnp.arange(n_exp, dtype=I32)[None, :], axis=1)
    pos0 = jnp.take_along_axis(dest, e0[:, None], axis=1)[:, 0].astype(I32)
    pos1 = jnp.take_along_axis(dest, e1[:, None], axis=1)[:, 0].astype(I32)
    wt0 = jnp.take_along_axis(comb, e0[:, None], axis=1)
    wt1 = jnp.take_along_axis(comb, e1[:, None], axis=1)
    p = t * TOP_K + n_exp * tm
    tok = jnp.arange(t, dtype=I32)
    src_tok = jnp.zeros((p,), I32).at[jnp.concatenate([pos0, pos1])].set(jnp.concatenate([tok, tok]))
    n_tiles = p // tm
    tstart = jnp.arange(n_tiles, dtype=I32) * tm
    tile_valid = (tstart < gend[-1]).astype(I32)
    te_raw = jnp.minimum(jnp.sum((tstart[:, None] >= gend[None, :]).astype(I32), axis=1), n_exp - 1)
    te_last = te_raw[jnp.maximum(gend[-1] // tm - 1, 0)]
    tile_expert = jnp.where(tile_valid == 1, te_raw, te_last).astype(I32)

    hs = _dispatch(x2, g, src_tok, rows=_tile(p, ROWS_GATHER))
    us = _moe_up(hs, w1, w3, tile_expert, tile_valid, (goff // tm).astype(I32), (gsize // tm).astype(I32), tm)
    ys = _moe_down(us, w2, tile_expert, tile_valid, tm)
    return _combine(x2, ys, pos0, pos1, wt0, wt1, rows=_tile(t, ROWS_GATHER))


def _pack_kernel(ia_ref, ib_ref, sh_ref, nv_ref, a_ref, b_ref, o_ref, *, shift):
    j = pl.program_id(0)
    a = a_ref[0]
    tn = a.shape[1]
    shifted = jnp.concatenate([a[:, shift:], b_ref[0][:, :shift]], axis=1)
    out = jnp.where(sh_ref[j] == 1, shifted, a)
    lane = lax.broadcasted_iota(I32, (1, tn), 1)
    o_ref[...] = jnp.where(lane < nv_ref[j], out, jnp.zeros_like(out))


def _pack_w_in(w_in_bf, layer, d_model, mix_w):
    iq_w = IDX_HEADS * IDX_DH
    tn = iq_w
    n_main = 10 * mix_w
    small = IDX_DH + IDX_HEADS
    gates0 = n_main + iq_w + small
    n_gates = N_BRANCH * d_model
    assert n_main % tn == 0 and n_gates % tn == 0 and small < tn
    shift = gates0 % tn
    ia = ([j for j in range(n_main // tn)]
          + [gates0 // tn + q for q in range(n_gates // tn)]
          + [n_main // tn]
          + [(n_main + iq_w) // tn])
    nblk = len(ia)
    sh = [0] * (n_main // tn) + [1] * (n_gates // tn) + [0, 0]
    nv = [tn] * (nblk - 1) + [small]
    ib = [a + 1 if s_ == 1 else a for a, s_ in zip(ia, sh)]
    assert (max(ib) + 1) * tn >= gates0 + n_gates and max(ib) * tn < w_in_bf.shape[2]

    def amap(j, ia_r, ib_r, sh_r, nv_r):
        return (layer, 0, ia_r[j])

    def bmap(j, ia_r, ib_r, sh_r, nv_r):
        return (layer, 0, ib_r[j])

    return pl.pallas_call(
        functools.partial(_pack_kernel, shift=shift),
        out_shape=jax.ShapeDtypeStruct((d_model, nblk * tn), BF16),
        grid_spec=pltpu.PrefetchScalarGridSpec(
            num_scalar_prefetch=4,
            grid=(nblk,),
            in_specs=[pl.BlockSpec((1, d_model, tn), amap),
                      pl.BlockSpec((1, d_model, tn), bmap)],
            out_specs=pl.BlockSpec((d_model, tn), lambda j, ia_r, ib_r, sh_r, nv_r: (0, j))),
        compiler_params=_cparams(("arbitrary",)),
        name="pack_w_in",
    )(jnp.asarray(ia, I32), jnp.asarray(ib, I32), jnp.asarray(sh, I32), jnp.asarray(nv, I32),
      w_in_bf, w_in_bf)


def _mixer(x2, b, s, positions, rel_bias, norm_g, w_in_bf, layer, w_br_l, w_o_l, gn_w, gn_b, conv_w,
           q_norm, k_norm):
    t, d_model = x2.shape
    mix_w = d_model // 2
    iq_w = IDX_HEADS * IDX_DH
    gate_col0 = 10 * mix_w
    iq_col0 = gate_col0 + N_BRANCH * d_model
    assert iq_col0 % iq_w == 0

    h = _rmsnorm(x2, norm_g)
    proj2 = _matmul(h, _pack_w_in(w_in_bf, layer, d_model, mix_w), BF16, *TILE_IN_PROJ, name="in_proj")
    proj3 = proj2.reshape(b, s, proj2.shape[1])
    pos_q = positions.reshape(b, s, 1)
    pos_k = positions.reshape(b, 1, s)

    y_ret = _retention_branch(proj3, pos_q, gn_w, gn_b, mix_w)
    y_conv = _conv_branch(proj3, conv_w, mix_w)

    ik0 = iq_col0 + iq_w
    kit = jnp.swapaxes(proj3[:, :, ik0:ik0 + IDX_DH], 1, 2)
    mask = _dsa_select(proj3, kit, s, min(TOPK_MAX, s // 4), iq_col0 // iq_w)
    qn, kn = _qk_norm(proj3, q_norm, k_norm, mix_w)
    y_att = _dsa_attention(qn, kn, proj3, mask, pos_q, pos_k, rel_bias, mix_w)

    ys = [y.reshape(t, mix_w) for y in (y_ret, y_conv, y_att)]
    merged = _branch_merge(ys, w_br_l.astype(BF16), proj2, d_model, mix_w, gate_col0)
    return _matmul(merged, w_o_l.astype(BF16), F32, *TILE_OUT_PROJ, res=x2, name="out_proj")


def kernel(x, positions, rel_bias, norm_mix, norm_ffn, w_in, w_br, w_o, ret_gn_w, ret_gn_b, conv_w,
           q_norm, k_norm, ffn_w1, ffn_w3, ffn_w2, moe_router, moe_w1, moe_w3, moe_w2):
    b, s, d_model = x.shape
    depth = w_in.shape[0]
    x2 = x.reshape(b * s, d_model)
    w_in_bf = w_in.astype(BF16)
    for l in range(depth):
        x2 = _mixer(x2, b, s, positions, rel_bias, norm_mix[l], w_in_bf, l, w_br[l], w_o[l],
                    ret_gn_w[l], ret_gn_b[l], conv_w[l], q_norm[l], k_norm[l])
        if l % 2 == 0:
            i = l // 2
            u = _glu(x2, norm_ffn[l], ffn_w1[i].astype(BF16), ffn_w3[i].astype(BF16))
            x2 = _matmul(u, ffn_w2[i].astype(BF16), F32, *TILE_FFN_DOWN, res=x2, name="ffn_down")
        else:
            i = l // 2
            x2 = _moe(x2, norm_ffn[l], moe_router[i], moe_w1[i], moe_w3[i], moe_w2[i].astype(BF16))
    return x2.reshape(b, s, d_model)
```

```python
import functools
import math

import numpy as np
import jax
import jax.numpy as jnp
from jax import lax
from jax.experimental import pallas as pl
from jax.experimental.pallas import tpu as pltpu

F32 = jnp.float32
BF16 = jnp.bfloat16
I32 = jnp.int32

EPS = 1e-6
N_BRANCH = 3
RET_HEADS = 8
RET_CHUNK = 128
RET_ROPE_BASE = 10000.0
CONV_K = 3
ATT_HEADS = 8
IDX_HEADS = 8
IDX_DH = 64
TOPK_MAX = 256
REL_BUCKETS = 32
REL_MAX_DIST = 128
TOP_K = 2

LANES = 128
SUBLANES = 8
VMEM_LIMIT_BYTES = 56 * 1024 * 1024
ROW_DMA_UNROLL = 8

TM_ROWWISE = 512
TILE_IN_PROJ = (2048, 1024)
TILE_OUT_PROJ = (2048, 512)
TILE_FFN_DOWN = (1024, 512)
TILE_FFN_UP = (1024, 512)
TILE_MERGE = (1024, 512)
TQ_SELECT, CK_SELECT = 256, 512
T_ATTN = 512
TM_MOE, TN_MOE = 512, 512
ROWS_GATHER = 256

NEG_MASK = -1e30
LOG2E = math.log2(math.e)
INT_MIN = -2 ** 31
KEY_BITS = 32


def _cparams(sem):
    return pltpu.CompilerParams(dimension_semantics=sem, vmem_limit_bytes=VMEM_LIMIT_BYTES)


def _sigmoid(x):
    return 1.0 / (1.0 + jnp.exp(-x))


def _tile(n, pref):
    t = min(n, pref)
    assert n % t == 0, (n, pref)
    return t


def _rmsnorm_kernel(x_ref, g_ref, o_ref):
    x = x_ref[...]
    ms = jnp.mean(x * x, axis=-1, keepdims=True)
    o_ref[...] = (x * lax.rsqrt(ms + EPS) * g_ref[...]).astype(o_ref.dtype)


def _rmsnorm(x2, g):
    t, d = x2.shape
    tm = _tile(t, TM_ROWWISE)
    return pl.pallas_call(
        _rmsnorm_kernel,
        out_shape=jax.ShapeDtypeStruct((t, d), BF16),
        grid=(t // tm,),
        in_specs=[pl.BlockSpec((tm, d), lambda i: (i, 0)),
                  pl.BlockSpec((1, d), lambda i: (0, 0))],
        out_specs=pl.BlockSpec((tm, d), lambda i: (i, 0)),
        compiler_params=_cparams(("parallel",)),
        name="rmsnorm",
    )(x2, g.reshape(1, d))


def _mm_kernel(a_ref, b_ref, o_ref):
    o_ref[...] = jnp.dot(a_ref[...], b_ref[...], preferred_element_type=F32).astype(o_ref.dtype)


def _mm_res_kernel(a_ref, b_ref, r_ref, o_ref):
    o_ref[...] = (r_ref[...] + jnp.dot(a_ref[...], b_ref[...], preferred_element_type=F32)).astype(o_ref.dtype)


def _matmul(a, b, out_dtype, tm, tn, res=None, name="matmul"):
    m, k = a.shape
    k2, n = b.shape
    assert k == k2
    tm = _tile(m, tm)
    tn = _tile(n, tn)
    in_specs = [pl.BlockSpec((tm, k), lambda i, j: (i, 0)),
                pl.BlockSpec((k, tn), lambda i, j: (0, j))]
    args = [a, b]
    kern = _mm_kernel
    if res is not None:
        in_specs.append(pl.BlockSpec((tm, tn), lambda i, j: (i, j)))
        args.append(res)
        kern = _mm_res_kernel
    return pl.pallas_call(
        kern,
        out_shape=jax.ShapeDtypeStruct((m, n), out_dtype),
        grid=(m // tm, n // tn),
        in_specs=in_specs,
        out_specs=pl.BlockSpec((tm, tn), lambda i, j: (i, j)),
        compiler_params=_cparams(("parallel", "parallel")),
        name=name,
    )(*args)


def _ret_kernel(pos_ref, invf_ref, q_ref, k_ref, v_ref, g_ref, dec_ref, kdec_ref, qdec_ref,
                gnw_ref, gnb_ref, o_ref, state_ref, *, heads, dh, chunk_decay):
    @pl.when(pl.program_id(1) == 0)
    def _():
        state_ref[...] = jnp.zeros_like(state_ref)

    c = q_ref.shape[1]
    ang = pos_ref[0].astype(F32) * invf_ref[...]
    cos = jnp.cos(ang)
    sin = jnp.sin(ang)
    lane = lax.broadcasted_iota(I32, (c, dh), 1)
    sin_signed = jnp.where(lane < dh // 2, -sin, sin)
    scale = dh ** -0.5
    for h in range(heads):
        sl = slice(h * dh, (h + 1) * dh)
        q = q_ref[0, :, sl].astype(F32)
        k = k_ref[0, :, sl].astype(F32)
        v = v_ref[0, :, sl]
        qr = (q * cos + pltpu.roll(q, dh // 2, 1) * sin_signed).astype(BF16)
        kr = (k * cos + pltpu.roll(k, dh // 2, 1) * sin_signed) * scale
        scores = lax.dot_general(qr, kr.astype(BF16), (((1,), (1,)), ((), ())),
                                 preferred_element_type=F32) * dec_ref[h]
        intra = jnp.dot(scores.astype(BF16), v, preferred_element_type=F32)
        prev = state_ref[h]
        cross = jnp.dot(qr, prev.astype(BF16), preferred_element_type=F32) * qdec_ref[h]
        kd = (kr * kdec_ref[h]).astype(BF16)
        kv = lax.dot_general(kd, v, (((0,), (0,)), ((), ())), preferred_element_type=F32)
        state_ref[h] = prev * chunk_decay[h] + kv
        ret = intra + cross
        mu = jnp.mean(ret, axis=-1, keepdims=True)
        cen = ret - mu
        var = jnp.mean(cen * cen, axis=-1, keepdims=True)
        gn = cen * lax.rsqrt(var + EPS) * gnw_ref[:, sl] + gnb_ref[:, sl]
        g = g_ref[0, :, sl].astype(F32)
        o_ref[0, :, sl] = (g * _sigmoid(g) * gn).astype(o_ref.dtype)


def _retention_branch(proj3, pos3, gn_w, gn_b, mix_w):
    b, s, _ = proj3.shape
    heads, c = RET_HEADS, RET_CHUNK
    dh = mix_w // heads
    assert dh == LANES and s % c == 0
    log_g = np.log1p(-np.exp2(-5.0 - np.arange(heads, dtype=np.float64)))
    pos = np.arange(c, dtype=np.float64)
    diff = pos[:, None] - pos[None, :]
    intra_decay = np.where(diff >= 0, np.exp(np.maximum(diff, 0.0)[None] * log_g[:, None, None]), 0.0)
    k_decay = np.exp((c - 1 - pos)[None, :] * log_g[:, None])
    q_decay = np.exp((pos + 1)[None, :] * log_g[:, None])
    chunk_decay = tuple(float(x) for x in np.exp(c * log_g))
    dec = jnp.asarray(intra_decay, F32)
    kdec = jnp.asarray(np.broadcast_to(k_decay[:, :, None], (heads, c, dh)), F32)
    qdec = jnp.asarray(np.broadcast_to(q_decay[:, :, None], (heads, c, dh)), F32)
    inv_freq = RET_ROPE_BASE ** (-jnp.arange(0, dh, 2, dtype=F32) / dh)
    invf = jnp.concatenate([inv_freq, inv_freq]).reshape(1, dh)

    def col(cb):
        return pl.BlockSpec((1, c, mix_w), lambda bi, n, cb=cb: (bi, n, cb))

    const3 = pl.BlockSpec((heads, c, dh), lambda bi, n: (0, 0, 0))
    return pl.pallas_call(
        functools.partial(_ret_kernel, heads=heads, dh=dh, chunk_decay=chunk_decay),
        out_shape=jax.ShapeDtypeStruct((b, s, mix_w), BF16),
        grid=(b, s // c),
        in_specs=[pl.BlockSpec((1, c, 1), lambda bi, n: (bi, n, 0)),
                  pl.BlockSpec((1, dh), lambda bi, n: (0, 0)),
                  col(0), col(1), col(2), col(3),
                  const3, const3, const3,
                  pl.BlockSpec((1, mix_w), lambda bi, n: (0, 0)),
                  pl.BlockSpec((1, mix_w), lambda bi, n: (0, 0))],
        out_specs=pl.BlockSpec((1, c, mix_w), lambda bi, n: (bi, n, 0)),
        scratch_shapes=[pltpu.VMEM((heads, dh, dh), F32)],
        compiler_params=_cparams(("arbitrary", "arbitrary")),
        name="retention",
    )(pos3, invf, proj3, proj3, proj3, proj3, dec, kdec, qdec,
      gn_w.reshape(1, mix_w), gn_b.reshape(1, mix_w))


def _conv_kernel(b_ref, c_ref, u_ref, w_ref, o_ref, ubuf_ref):
    ts = o_ref.shape[1]
    halo = SUBLANES

    @pl.when(pl.program_id(1) == 0)
    def _():
        ubuf_ref[0:halo, :] = jnp.zeros((halo, ubuf_ref.shape[1]), F32)

    u = c_ref[0].astype(F32) * u_ref[0].astype(F32)
    ubuf_ref[halo:halo + ts, :] = u
    u1 = ubuf_ref[halo - 1:halo - 1 + ts, :]
    u2 = ubuf_ref[halo - 2:halo - 2 + ts, :]
    conv = w_ref[0:1, :] * u2 + w_ref[1:2, :] * u1 + w_ref[2:3, :] * u
    o_ref[0] = (b_ref[0].astype(F32) * conv).astype(o_ref.dtype)
    ubuf_ref[0:halo, :] = ubuf_ref[ts:ts + halo, :]


def _conv_branch(proj3, conv_w, mix_w):
    b, s, _ = proj3.shape
    ts = _tile(s, TM_ROWWISE)

    def col(cb):
        return pl.BlockSpec((1, ts, mix_w), lambda bi, n, cb=cb: (bi, n, cb))

    return pl.pallas_call(
        _conv_kernel,
        out_shape=jax.ShapeDtypeStruct((b, s, mix_w), BF16),
        grid=(b, s // ts),
        in_specs=[col(4), col(5), col(6),
                  pl.BlockSpec((CONV_K, mix_w), lambda bi, n: (0, 0))],
        out_specs=pl.BlockSpec((1, ts, mix_w), lambda bi, n: (bi, n, 0)),
        scratch_shapes=[pltpu.VMEM((ts + SUBLANES, mix_w), F32)],
        compiler_params=_cparams(("arbitrary", "arbitrary")),
        name="short_conv",
    )(proj3, proj3, proj3, conv_w)


def _qknorm_kernel(q_ref, k_ref, qn_ref, kn_ref, qo_ref, ko_ref, *, heads, dh, q_scale):
    for h in range(heads):
        sl = slice(h * dh, (h + 1) * dh)
        q = q_ref[0, :, sl].astype(F32)
        k = k_ref[0, :, sl].astype(F32)
        qy = q * lax.rsqrt(jnp.mean(q * q, axis=-1, keepdims=True) + EPS) * qn_ref[...]
        ky = k * lax.rsqrt(jnp.mean(k * k, axis=-1, keepdims=True) + EPS) * kn_ref[...]
        qo_ref[0, :, sl] = (qy * q_scale).astype(qo_ref.dtype)
        ko_ref[0, :, sl] = ky.astype(ko_ref.dtype)


def _qk_norm(proj3, q_norm, k_norm, mix_w):
    b, s, _ = proj3.shape
    heads = ATT_HEADS
    dh = mix_w // heads
    ts = _tile(s, TM_ROWWISE)
    out = jax.ShapeDtypeStruct((b, s, mix_w), BF16)
    blk = pl.BlockSpec((1, ts, mix_w), lambda bi, n: (bi, n, 0))
    return pl.pallas_call(
        functools.partial(_qknorm_kernel, heads=heads, dh=dh, q_scale=dh ** -0.5 * LOG2E),
        out_shape=(out, out),
        grid=(b, s // ts),
        in_specs=[pl.BlockSpec((1, ts, mix_w), lambda bi, n: (bi, n, 7)),
                  pl.BlockSpec((1, ts, mix_w), lambda bi, n: (bi, n, 8)),
                  pl.BlockSpec((1, dh), lambda bi, n: (0, 0)),
                  pl.BlockSpec((1, dh), lambda bi, n: (0, 0))],
        out_specs=(blk, blk),
        compiler_params=_cparams(("parallel", "parallel")),
        name="qk_norm",
    )(proj3, proj3, q_norm.reshape(1, dh), k_norm.reshape(1, dh))


def _select_kernel(iq_ref, sm_ref, kit_ref, tri_ref, o_ref, qall_ref, keys_ref, planes_ref, eq_ref, *,
                   tq, ck, ksel, heads, d_idx):
    qb = pl.program_id(1)
    row0 = qb * tq
    nch = (row0 + tq + ck - 1) // ck
    grp = KEY_BITS * LANES
    ngrp = keys_ref.shape[1] // grp
    nlive = (row0 + tq + grp - 1) // grp
    for h in range(heads):
        qall_ref[h * tq:(h + 1) * tq, :] = iq_ref[0, :, h * d_idx:(h + 1) * d_idx]
    w = sm_ref[0, :, d_idx:d_idx + heads].astype(F32) * (heads ** -0.5 * d_idx ** -0.5)
    t_col = row0 + lax.broadcasted_iota(I32, (tq, 1), 0)
    lane_ck = lax.broadcasted_iota(I32, (1, ck), 1)

    def score_chunk(c, carry):
        off = pl.multiple_of(c * ck, ck)
        kt = kit_ref[0, :, pl.ds(off, ck)]
        sc = jnp.zeros((tq, ck), F32)
        for h in range(heads):
            d = jnp.dot(qall_ref[h * tq:(h + 1) * tq, :], kt, preferred_element_type=F32)
            sc = sc + w[:, h:h + 1] * jnp.maximum(d, 0.0)
        bits = pltpu.bitcast(sc, I32)
        key = bits ^ ((bits >> 31) & 0x7FFFFFFF)
        keys_ref[:, pl.ds(off, ck)] = jnp.where(off + lane_ck <= t_col, key, INT_MIN)
        return carry

    def clear_chunk(c, carry):
        keys_ref[:, pl.ds(pl.multiple_of(c * ck, ck), ck)] = jnp.full((tq, ck), INT_MIN, I32)
        return carry

    lax.fori_loop(0, nch, score_chunk, 0)
    lax.fori_loop(nch, nlive * (grp // ck), clear_chunk, 0)

    def transpose_group(g):
        def rows(r, carry):
            rs = pl.ds(pl.multiple_of(r * SUBLANES, SUBLANES), SUBLANES)
            a = [keys_ref[rs, (g * KEY_BITS + j) * LANES:(g * KEY_BITS + j + 1) * LANES] ^ INT_MIN
                 for j in range(KEY_BITS)]
            j, m = KEY_BITS // 2, (1 << (KEY_BITS // 2)) - 1
            while j:
                k = 0
                while k < KEY_BITS:
                    t = (a[k] ^ (a[k + j] >> j)) & m
                    a[k] = a[k] ^ t
                    a[k + j] = a[k + j] ^ (t << j)
                    k = (k + j + 1) & ~j
                j >>= 1
                m = m ^ (m << j)
            for p in range(KEY_BITS):
                planes_ref[p, g, rs, :] = a[p]
            return carry

        lax.fori_loop(0, tq // SUBLANES, rows, 0)

    @pl.when(qb == 0)
    def _():
        planes_ref[...] = jnp.zeros_like(planes_ref)

    for g in range(ngrp):
        @pl.when(g < nlive)
        def _(g=g):
            transpose_group(g)
            eq_ref[g] = jnp.full(eq_ref.shape[1:], -1, I32)

        @pl.when(g >= nlive)
        def _(g=g):
            eq_ref[g] = jnp.zeros(eq_ref.shape[1:], I32)

    kf = float(ksel)

    def bit_pass(p, carry):
        above, thr_u = carry
        ones = jnp.zeros((tq, LANES), I32)
        for g in range(ngrp):
            ones = ones + lax.population_count(eq_ref[g] & planes_ref[p, g])
        c1 = jnp.sum(ones.astype(F32), axis=1, keepdims=True)
        take = (above + c1) >= kf
        for g in range(ngrp):
            e = eq_ref[g]
            x = e & planes_ref[p, g]
            eq_ref[g] = jnp.where(take, x, e ^ x)
        bit = jnp.left_shift(jnp.int32(1), KEY_BITS - 1 - p)
        return jnp.where(take, above, above + c1), jnp.where(take, thr_u | bit, thr_u)

    cnt_gt, thr_u = lax.fori_loop(0, KEY_BITS, bit_pass,
                                  (jnp.zeros((tq, 1), F32), jnp.zeros((tq, 1), I32)))
    thr = thr_u ^ INT_MIN
    eq_cnt = jnp.zeros((tq, LANES), I32)
    for g in range(ngrp):
        eq_cnt = eq_cnt + lax.population_count(eq_ref[g])
    cnt_eq = jnp.sum(eq_cnt.astype(F32), axis=1, keepdims=True)

    need = kf - cnt_gt
    excess = jnp.where(thr == INT_MIN, 0.0, cnt_eq - need)
    o_ref[...] = jnp.full(o_ref.shape, NEG_MASK, o_ref.dtype)

    def write_chunk(c, eq_before, ties):
        off = pl.multiple_of(c * ck, ck)
        sl = pl.ds(off, ck)
        k = keys_ref[:, sl]
        if ties:
            eq = jnp.where(k == thr, 1.0, 0.0).astype(BF16)
            rank = eq_before + jnp.dot(eq, tri_ref[...], preferred_element_type=F32)
            on_eq = jnp.where(rank <= need, 0.0, NEG_MASK)
            eq_before = rank[:, ck - 1:ck]
        else:
            on_eq = 0.0
        val = jnp.where(k > thr, 0.0, jnp.where(k == thr, on_eq, NEG_MASK))
        o_ref[0, :, sl] = jnp.where(off + lane_ck <= t_col, val, NEG_MASK).astype(o_ref.dtype)
        return eq_before

    any_excess = jnp.max(excess) > 0.0

    @pl.when(any_excess)
    def _():
        lax.fori_loop(0, nch, lambda c, e: write_chunk(c, e, True), jnp.zeros((tq, 1), F32))

    @pl.when(jnp.logical_not(any_excess))
    def _():
        lax.fori_loop(0, nch, lambda c, e: write_chunk(c, e, False), 0)


def _dsa_select(proj3, kit, s, ksel, iq_block):
    b = proj3.shape[0]
    tq = _tile(s, TQ_SELECT)
    ck = _tile(s, CK_SELECT)
    heads, d_idx = IDX_HEADS, IDX_DH
    iq_w = heads * d_idx
    assert (s % (KEY_BITS * LANES) == 0 or s < KEY_BITS * LANES) and (KEY_BITS * LANES) % ck == 0
    sp = max(s, KEY_BITS * LANES)
    tri = (jnp.arange(ck)[:, None] <= jnp.arange(ck)[None, :]).astype(BF16)
    return pl.pallas_call(
        functools.partial(_select_kernel, tq=tq, ck=ck, ksel=ksel, heads=heads, d_idx=d_idx),
        out_shape=jax.ShapeDtypeStruct((b, s, s), BF16),
        grid=(b, s // tq),
        in_specs=[pl.BlockSpec((1, tq, iq_w), lambda bi, n: (bi, n, iq_block)),
                  pl.BlockSpec((1, tq, iq_w), lambda bi, n: (bi, n, iq_block + 1)),
                  pl.BlockSpec((1, d_idx, s), lambda bi, n: (bi, 0, 0)),
                  pl.BlockSpec((ck, ck), lambda bi, n: (0, 0))],
        out_specs=pl.BlockSpec((1, tq, s), lambda bi, n: (bi, n, 0)),
        scratch_shapes=[pltpu.VMEM((heads * tq, d_idx), BF16),
                        pltpu.VMEM((tq, sp), I32),
                        pltpu.VMEM((KEY_BITS, sp // (KEY_BITS * LANES), tq, LANES), I32),
                        pltpu.VMEM((sp // (KEY_BITS * LANES), tq, LANES), I32)],
        compiler_params=_cparams(("arbitrary", "arbitrary")),
        name="dsa_select",
    )(proj3, proj3, kit, tri)


def _t5_bucket(n):
    max_exact = REL_BUCKETS // 2
    nf = jnp.maximum(n, 1).astype(F32)
    large = max_exact + (jnp.log(nf / max_exact) / math.log(REL_MAX_DIST / max_exact)
                         * (REL_BUCKETS - max_exact)).astype(I32)
    large = jnp.minimum(large, REL_BUCKETS - 1)
    return jnp.where(n < max_exact, n, large)


def _attn_kernel(qb_ref, kb_ref, rb_ref, plo_ref, phi_ref, q_ref, k_ref, v_ref, mask_ref, pq_ref, pk_ref,
                 o_ref, acc_ref, m_ref, l_ref, bias_ref, s_ref, p_ref, *, tq, tk, heads, dh):
    qb = qb_ref[pl.program_id(1)]
    kb = kb_ref[pl.program_id(1)]
    kmax = ((qb + 1) * tq - 1) // tk
    nlt = tk // LANES
    gran = LANES

    @pl.when(kb == 0)
    def _():
        acc_ref[...] = jnp.zeros_like(acc_ref)
        m_ref[...] = jnp.full(m_ref.shape, -jnp.inf, F32)
        l_ref[...] = jnp.zeros_like(l_ref)

    def attend(bias_of):
        ones = jnp.ones((tk, LANES), BF16)
        for h in range(heads):
            sl = slice(h * dh, (h + 1) * dh)
            kh = k_ref[0, :, sl]
            v_aug = jnp.concatenate([v_ref[0, :, sl], ones], axis=1)
            add, const = bias_of(h)
            buf = h % 2
            s = lax.dot_general(q_ref[0, :, sl], kh, (((1,), (1,)), ((), ())),
                                preferred_element_type=F32) + add
            s_ref[buf] = s
            tile_max = s[:, 0:LANES]
            for c in range(1, nlt):
                tile_max = jnp.maximum(tile_max, s[:, c * LANES:(c + 1) * LANES])
            m_cur = jnp.broadcast_to(jnp.max(tile_max, axis=1, keepdims=True), (tq, LANES)) + const
            m_old = m_ref[h]
            m_new = jnp.maximum(m_old, m_cur)
            alpha = jnp.exp2(m_old - m_new)
            shift = m_new - const
            for c in range(nlt):
                cs = slice(c * LANES, (c + 1) * LANES)
                p_ref[buf, :, cs] = jnp.exp2(s_ref[buf, :, cs] - shift).astype(BF16)
            pv = jnp.dot(p_ref[buf], v_aug, preferred_element_type=F32)
            m_ref[h] = m_new
            l_ref[h] = alpha * l_ref[h] + pv[:, dh:]
            acc_ref[:, sl] = alpha * acc_ref[:, sl] + pv[:, :dh]

    @pl.when(kb <= kmax)
    def _():
        pq = pq_ref[0]
        pk = pk_ref[0]
        bi = pl.program_id(0)
        q_lo = [plo_ref[bi, qb * (tq // gran) + ri] for ri in range(tq // gran)]
        q_hi = [phi_ref[bi, qb * (tq // gran) + ri] for ri in range(tq // gran)]
        k_lo = [plo_ref[bi, kb * (tk // gran) + cj] for cj in range(tk // gran)]
        k_hi = [phi_ref[bi, kb * (tk // gran) + cj] for cj in range(tk // gran)]
        all_far = (functools.reduce(jnp.minimum, q_lo) - functools.reduce(jnp.maximum, k_hi)) >= REL_MAX_DIST

        @pl.when(all_far)
        def _():
            attend(lambda h: (mask_ref[0].astype(F32), rb_ref[REL_BUCKETS - 1, h] * LOG2E))

        @pl.when(jnp.logical_not(all_far))
        def _():
            n_lane = lax.broadcasted_iota(I32, (1, REL_MAX_DIST), 1)
            bucket = _t5_bucket(n_lane)
            tabs = []
            for h in range(heads):
                tab = jnp.zeros((1, REL_MAX_DIST), F32)
                for j in range(REL_BUCKETS):
                    tab = jnp.where(bucket == j, rb_ref[j, h] * LOG2E, tab)
                tabs.append(jnp.broadcast_to(tab, (gran, REL_MAX_DIST)))
            for ri in range(tq // gran):
                rs = slice(ri * gran, (ri + 1) * gran)
                pq_g = pq[rs]
                for cj in range(tk // gran):
                    cs = slice(cj * gran, (cj + 1) * gran)
                    pk_g = pk[:, cs]
                    lo = q_lo[ri] - k_hi[cj]
                    hi = q_hi[ri] - k_lo[cj]
                    is_far = lo >= REL_MAX_DIST
                    is_zero = hi <= 0
                    maskf = mask_ref[0, rs, cs].astype(F32)

                    @pl.when(is_far | is_zero)
                    def _():
                        for h in range(heads):
                            c_h = jnp.where(is_far, rb_ref[REL_BUCKETS - 1, h], rb_ref[0, h]) * LOG2E
                            bias_ref[h, rs, cs] = maskf + c_h

                    @pl.when(jnp.logical_not(is_far | is_zero))
                    def _():
                        dist = jnp.clip(pq_g - pk_g, 0, REL_MAX_DIST - 1)
                        for h in range(heads):
                            bias_ref[h, rs, cs] = maskf + jnp.take_along_axis(tabs[h], dist, axis=1)
            attend(lambda h: (bias_ref[h], 0.0))

    @pl.when(kb == kmax)
    def _():
        for h in range(heads):
            sl = slice(h * dh, (h + 1) * dh)
            o_ref[0, :, sl] = (acc_ref[:, sl] / l_ref[h]).astype(o_ref.dtype)


def _dsa_attention(qn, kn, proj3, mask, pos_q, pos_k, rel_bias, mix_w):
    b, s, _ = qn.shape
    heads = ATT_HEADS
    dh = mix_w // heads
    tq = _tile(s, T_ATTN)
    tk = tq
    assert REL_MAX_DIST == LANES

    tiles = [(qb, kb) for qb in range(s // tq) for kb in range(((qb + 1) * tq - 1) // tk + 1)]
    qb_of = jnp.asarray([t[0] for t in tiles], I32)
    kb_of = jnp.asarray([t[1] for t in tiles], I32)
    pos_g = pos_k.reshape(b, s // LANES, LANES)
    pos_lo = jnp.min(pos_g, axis=-1)
    pos_hi = jnp.max(pos_g, axis=-1)

    return pl.pallas_call(
        functools.partial(_attn_kernel, tq=tq, tk=tk, heads=heads, dh=dh),
        out_shape=jax.ShapeDtypeStruct((b, s, mix_w), BF16),
        grid_spec=pltpu.PrefetchScalarGridSpec(
            num_scalar_prefetch=2,
            grid=(b, len(tiles)),
            in_specs=[pl.BlockSpec(memory_space=pltpu.SMEM),
                      pl.BlockSpec(memory_space=pltpu.SMEM),
                      pl.BlockSpec(memory_space=pltpu.SMEM),
                      pl.BlockSpec((1, tq, mix_w), lambda bi, st, qo, ko: (bi, qo[st], 0)),
                      pl.BlockSpec((1, tk, mix_w), lambda bi, st, qo, ko: (bi, ko[st], 0)),
                      pl.BlockSpec((1, tk, mix_w), lambda bi, st, qo, ko: (bi, ko[st], 9)),
                      pl.BlockSpec((1, tq, tk), lambda bi, st, qo, ko: (bi, qo[st], ko[st])),
                      pl.BlockSpec((1, tq, 1), lambda bi, st, qo, ko: (bi, qo[st], 0)),
                      pl.BlockSpec((1, 1, tk), lambda bi, st, qo, ko: (bi, 0, ko[st]))],
            out_specs=pl.BlockSpec((1, tq, mix_w), lambda bi, st, qo, ko: (bi, qo[st], 0)),
            scratch_shapes=[pltpu.VMEM((tq, mix_w), F32),
                            pltpu.VMEM((heads, tq, LANES), F32),
                            pltpu.VMEM((heads, tq, LANES), F32),
                            pltpu.VMEM((heads, tq, tk), F32),
                            pltpu.VMEM((2, tq, tk), F32),
                            pltpu.VMEM((2, tq, tk), BF16)]),
        compiler_params=_cparams(("parallel", "arbitrary")),
        name="dsa_attention",
    )(qb_of, kb_of, rel_bias, pos_lo, pos_hi, qn, kn, proj3, mask, pos_q, pos_k)


def _merge_kernel(y0_ref, y1_ref, y2_ref, w_ref, g0_ref, g1_ref, g2_ref, o_ref):
    acc = None
    for n, (y_ref, g_ref) in enumerate(((y0_ref, g0_ref), (y1_ref, g1_ref), (y2_ref, g2_ref))):
        br = jnp.dot(y_ref[...], w_ref[n], preferred_element_type=F32)
        term = _sigmoid(g_ref[...].astype(F32)) * br
        acc = term if acc is None else acc + term
    o_ref[...] = acc.astype(o_ref.dtype)


def _branch_merge(ys, w_br, proj2, d_model, mix_w, gate_col0):
    t = proj2.shape[0]
    tm = _tile(t, TILE_MERGE[0])
    tn = _tile(d_model, TILE_MERGE[1])
    yspec = pl.BlockSpec((tm, mix_w), lambda i, j: (i, 0))

    def gspec(n):
        base = (gate_col0 + n * d_model) // tn
        return pl.BlockSpec((tm, tn), lambda i, j, base=base: (i, base + j))

    return pl.pallas_call(
        _merge_kernel,
        out_shape=jax.ShapeDtypeStruct((t, d_model), BF16),
        grid=(t // tm, d_model // tn),
        in_specs=[yspec, yspec, yspec,
                  pl.BlockSpec((N_BRANCH, mix_w, tn), lambda i, j: (0, 0, j)),
                  gspec(0), gspec(1), gspec(2)],
        out_specs=pl.BlockSpec((tm, tn), lambda i, j: (i, j)),
        compiler_params=_cparams(("parallel", "parallel")),
        name="branch_merge",
    )(ys[0], ys[1], ys[2], w_br, proj2, proj2, proj2)


def _glu_kernel(x_ref, g_ref, w1_ref, w3_ref, o_ref, a_ref):
    @pl.when(pl.program_id(1) == 0)
    def _():
        x = x_ref[...]
        ms = jnp.mean(x * x, axis=-1, keepdims=True)
        a_ref[...] = (x * lax.rsqrt(ms + EPS) * g_ref[...]).astype(a_ref.dtype)

    a = a_ref[...]
    h1 = jnp.dot(a, w1_ref[...], preferred_element_type=F32)
    h3 = jnp.dot(a, w3_ref[...], preferred_element_type=F32)
    o_ref[...] = (h1 * _sigmoid(h1) * h3).astype(o_ref.dtype)


def _glu(x2, g, w1, w3):
    t, d = x2.shape
    ff = w1.shape[1]
    tm = _tile(t, TILE_FFN_UP[0])
    tn = _tile(ff, TILE_FFN_UP[1])
    return pl.pallas_call(
        _glu_kernel,
        out_shape=jax.ShapeDtypeStruct((t, ff), BF16),
        grid=(t // tm, ff // tn),
        in_specs=[pl.BlockSpec((tm, d), lambda i, j: (i, 0)),
                  pl.BlockSpec((1, d), lambda i, j: (0, 0)),
                  pl.BlockSpec((d, tn), lambda i, j: (0, j)),
                  pl.BlockSpec((d, tn), lambda i, j: (0, j))],
        out_specs=pl.BlockSpec((tm, tn), lambda i, j: (i, j)),
        scratch_shapes=[pltpu.VMEM((tm, d), BF16)],
        compiler_params=_cparams(("parallel", "arbitrary")),
        name="swiglu_up",
    )(x2, g.reshape(1, d), w1, w3)


def _router_kernel(x_ref, g_ref, r_ref, comb_ref, sel_ref, *, n_exp):
    x = x_ref[...]
    h = (x * lax.rsqrt(jnp.mean(x * x, axis=-1, keepdims=True) + EPS) * g_ref[...]).astype(BF16)
    logits = jnp.dot(h, r_ref[...], preferred_element_type=F32)
    lane = lax.broadcasted_iota(I32, logits.shape, 1)
    lg = jnp.where(lane < n_exp, logits, -jnp.inf)
    v1 = jnp.max(lg, axis=1, keepdims=True)
    i1 = jnp.min(jnp.where(lg == v1, lane, LANES), axis=1, keepdims=True)
    lg2 = jnp.where(lane == i1, -jnp.inf, lg)
    v2 = jnp.max(lg2, axis=1, keepdims=True)
    i2 = jnp.min(jnp.where(lg2 == v2, lane, LANES), axis=1, keepdims=True)
    e = jnp.exp(v2 - v1)
    w1 = 1.0 / (1.0 + e)
    w2 = e / (1.0 + e)
    comb_ref[...] = jnp.where(lane == i1, w1, jnp.where(lane == i2, w2, 0.0))
    sel_ref[...] = jnp.where((lane == i1) | (lane == i2), 1, 0)


def _router(x2, g, router_w):
    t, d = x2.shape
    n_exp = router_w.shape[1]
    tm = _tile(t, TM_ROWWISE)
    rpad = jnp.zeros((d, LANES), BF16).at[:, :n_exp].set(router_w.astype(BF16))
    comb, sel = pl.pallas_call(
        functools.partial(_router_kernel, n_exp=n_exp),
        out_shape=(jax.ShapeDtypeStruct((t, LANES), F32), jax.ShapeDtypeStruct((t, LANES), I32)),
        grid=(t // tm,),
        in_specs=[pl.BlockSpec((tm, d), lambda i: (i, 0)),
                  pl.BlockSpec((1, d), lambda i: (0, 0)),
                  pl.BlockSpec((d, LANES), lambda i: (0, 0))],
        out_specs=(pl.BlockSpec((tm, LANES), lambda i: (i, 0)),
                   pl.BlockSpec((tm, LANES), lambda i: (i, 0))),
        compiler_params=_cparams(("parallel",)),
        name="moe_router",
    )(x2, g.reshape(1, d), rpad)
    return comb[:, :n_exp], sel[:, :n_exp]


def _row_gather(idx_ref, src_hbm, buf_ref, sem, slot, rows, start):
    def body(r8, carry):
        for u in range(ROW_DMA_UNROLL):
            r = r8 * ROW_DMA_UNROLL + u
            cp = pltpu.make_async_copy(src_hbm.at[pl.ds(idx_ref[0, 0, r], 1)],
                                       buf_ref.at[slot, pl.ds(r, 1)], sem.at[slot])
            if start:
                cp.start(priority=u % 2)
            else:
                cp.wait()
        return carry

    lax.fori_loop(0, rows // ROW_DMA_UNROLL, body, 0)


def _dispatch_kernel(cur_ref, nxt_ref, x_hbm, g_ref, o_ref, buf_ref, sem, *, rows, nsteps):
    i = pl.program_id(0)
    slot = i % 2

    @pl.when(i == 0)
    def _():
        _row_gather(cur_ref, x_hbm, buf_ref, sem, 0, rows, start=True)

    @pl.when(i + 1 < nsteps)
    def _():
        _row_gather(nxt_ref, x_hbm, buf_ref, sem, 1 - slot, rows, start=True)

    _row_gather(cur_ref, x_hbm, buf_ref, sem, slot, rows, start=False)
    x = buf_ref[slot]
    o_ref[...] = (x * lax.rsqrt(jnp.mean(x * x, axis=-1, keepdims=True) + EPS) * g_ref[...]).astype(o_ref.dtype)


def _dispatch(x2, g, src_tok, rows):
    t, d = x2.shape
    p = src_tok.shape[0]
    assert p % rows == 0 and rows % ROW_DMA_UNROLL == 0
    nsteps = p // rows
    src3 = src_tok.reshape(nsteps, 1, rows)
    return pl.pallas_call(
        functools.partial(_dispatch_kernel, rows=rows, nsteps=nsteps),
        out_shape=jax.ShapeDtypeStruct((p, d), BF16),
        grid=(nsteps,),
        in_specs=[pl.BlockSpec((1, 1, rows), lambda i: (i, 0, 0), memory_space=pltpu.SMEM),
                  pl.BlockSpec((1, 1, rows), lambda i: (jnp.minimum(i + 1, nsteps - 1), 0, 0),
                               memory_space=pltpu.SMEM),
                  pl.BlockSpec(memory_space=pl.ANY),
                  pl.BlockSpec((1, d), lambda i: (0, 0))],
        out_specs=pl.BlockSpec((rows, d), lambda i: (i, 0)),
        scratch_shapes=[pltpu.VMEM((2, rows, d), F32), pltpu.SemaphoreType.DMA((2,))],
        compiler_params=_cparams(("arbitrary",)),
        name="moe_dispatch",
    )(src3, src3, x2, g.reshape(1, d))


def _moe_up_kernel(ti_ref, tj_ref, wj_ref, te_ref, first_ref, valid_ref, a_ref, w1_ref, w3_ref, o_ref,
                   w1b_ref, w3b_ref):
    s = pl.program_id(0)

    @pl.when(first_ref[s] == 1)
    def _():
        w1b_ref[...] = w1_ref[0].astype(BF16)
        w3b_ref[...] = w3_ref[0].astype(BF16)

    @pl.when(valid_ref[s] == 1)
    def _():
        a = a_ref[...]
        h1 = jnp.dot(a, w1b_ref[...], preferred_element_type=F32)
        h3 = jnp.dot(a, w3b_ref[...], preferred_element_type=F32)
        o_ref[...] = (h1 * _sigmoid(h1) * h3).astype(o_ref.dtype)

    @pl.when(valid_ref[s] == 0)
    def _():
        o_ref[...] = jnp.zeros_like(o_ref)


def _moe_up(hs, w1, w3, tile_expert, tile_valid, gfirst, gtiles, tm):
    p, d = hs.shape
    ff = w1.shape[2]
    tn = _tile(ff, TN_MOE)
    nj = ff // tn
    n_tiles = p // tm
    step = jnp.arange(n_tiles * nj, dtype=I32)
    t0 = step // nj
    e_s = tile_expert[t0]
    valid = tile_valid[t0]
    n_run = jnp.maximum(gtiles[e_s], 1)
    local = step - gfirst[e_s] * nj
    ti = jnp.where(valid == 1, gfirst[e_s] + local % n_run, t0).astype(I32)
    tj = jnp.where(valid == 1, local // n_run, step % nj).astype(I32)
    wj = jnp.where(valid == 1, local // n_run, nj - 1).astype(I32)
    first = jnp.where((valid == 1) & (local % n_run == 0), 1, 0).astype(I32)

    def amap(s, ti_r, tj_r, wj_r, te_r, f_r, v_r):
        return (ti_r[s], 0)

    def wmap(s, ti_r, tj_r, wj_r, te_r, f_r, v_r):
        return (te_r[s], 0, wj_r[s])

    def omap(s, ti_r, tj_r, wj_r, te_r, f_r, v_r):
        return (ti_r[s], tj_r[s])

    return pl.pallas_call(
        _moe_up_kernel,
        out_shape=jax.ShapeDtypeStruct((p, ff), BF16),
        grid_spec=pltpu.PrefetchScalarGridSpec(
            num_scalar_prefetch=6,
            grid=(n_tiles * nj,),
            in_specs=[pl.BlockSpec((tm, d), amap),
                      pl.BlockSpec((1, d, tn), wmap),
                      pl.BlockSpec((1, d, tn), wmap)],
            out_specs=pl.BlockSpec((tm, tn), omap),
            scratch_shapes=[pltpu.VMEM((d, tn), BF16), pltpu.VMEM((d, tn), BF16)]),
        compiler_params=_cparams(("arbitrary",)),
        name="moe_up",
    )(ti, tj, wj, e_s.astype(I32), first, valid.astype(I32), hs, w1, w3)


def _moe_down_kernel(te_ref, tv_ref, a_ref, w_ref, o_ref):
    i = pl.program_id(0)

    @pl.when(tv_ref[i] == 1)
    def _():
        o_ref[...] = jnp.dot(a_ref[...], w_ref[0], preferred_element_type=F32)

    @pl.when(tv_ref[i] == 0)
    def _():
        o_ref[...] = jnp.zeros_like(o_ref)


def _moe_down(us, w2, tile_expert, tile_valid, tm):
    p, ff = us.shape
    d = w2.shape[2]
    tn = _tile(d, TN_MOE)
    nj = d // tn

    def wmap(i, j, te, tv):
        return (te[i], 0, jnp.where(tv[i] == 1, j, nj - 1))

    return pl.pallas_call(
        _moe_down_kernel,
        out_shape=jax.ShapeDtypeStruct((p, d), F32),
        grid_spec=pltpu.PrefetchScalarGridSpec(
            num_scalar_prefetch=2,
            grid=(p // tm, nj),
            in_specs=[pl.BlockSpec((tm, ff), lambda i, j, te, tv: (i, 0)),
                      pl.BlockSpec((1, ff, tn), wmap)],
            out_specs=pl.BlockSpec((tm, tn), lambda i, j, te, tv: (i, j))),
        compiler_params=_cparams(("arbitrary", "arbitrary")),
        name="moe_down",
    )(tile_expert, tile_valid, us, w2)


def _combine_kernel(p0_ref, p1_ref, n0_ref, n1_ref, x_ref, w0_ref, w1_ref, ys_hbm, o_ref,
                    a_ref, b_ref, sem_a, sem_b, *, rows, nsteps):
    i = pl.program_id(0)
    slot = i % 2

    @pl.when(i == 0)
    def _():
        _row_gather(p0_ref, ys_hbm, a_ref, sem_a, 0, rows, start=True)
        _row_gather(p1_ref, ys_hbm, b_ref, sem_b, 0, rows, start=True)

    @pl.when(i + 1 < nsteps)
    def _():
        _row_gather(n0_ref, ys_hbm, a_ref, sem_a, 1 - slot, rows, start=True)
        _row_gather(n1_ref, ys_hbm, b_ref, sem_b, 1 - slot, rows, start=True)

    _row_gather(p0_ref, ys_hbm, a_ref, sem_a, slot, rows, start=False)
    _row_gather(p1_ref, ys_hbm, b_ref, sem_b, slot, rows, start=False)
    o_ref[...] = x_ref[...] + (w0_ref[...] * a_ref[slot] + w1_ref[...] * b_ref[slot])


def _combine(x2, ys, pos0, pos1, w0, w1, rows):
    t, d = x2.shape
    assert t % rows == 0 and rows % ROW_DMA_UNROLL == 0
    nsteps = t // rows
    cur = pl.BlockSpec((1, 1, rows), lambda i: (i, 0, 0), memory_space=pltpu.SMEM)
    nxt = pl.BlockSpec((1, 1, rows), lambda i: (jnp.minimum(i + 1, nsteps - 1), 0, 0),
                       memory_space=pltpu.SMEM)
    p0 = pos0.reshape(nsteps, 1, rows)
    p1 = pos1.reshape(nsteps, 1, rows)
    return pl.pallas_call(
        functools.partial(_combine_kernel, rows=rows, nsteps=nsteps),
        out_shape=jax.ShapeDtypeStruct((t, d), F32),
        grid=(nsteps,),
        in_specs=[cur, cur, nxt, nxt,
                  pl.BlockSpec((rows, d), lambda i: (i, 0)),
                  pl.BlockSpec((rows, 1), lambda i: (i, 0)),
                  pl.BlockSpec((rows, 1), lambda i: (i, 0)),
                  pl.BlockSpec(memory_space=pl.ANY)],
        out_specs=pl.BlockSpec((rows, d), lambda i: (i, 0)),
        scratch_shapes=[pltpu.VMEM((2, rows, d), F32), pltpu.VMEM((2, rows, d), F32),
                        pltpu.SemaphoreType.DMA((2,)), pltpu.SemaphoreType.DMA((2,))],
        compiler_params=_cparams(("arbitrary",)),
        name="moe_combine",
    )(p0, p1, p0, p1, x2, w0, w1, ys)


def _moe(x2, g, router_w, w1, w3, w2):
    t, d = x2.shape
    n_exp = router_w.shape[1]
    tm = _tile(t, TM_MOE)
    comb, sel = _router(x2, g, router_w)
    rank = jnp.cumsum(sel, axis=0) - sel
    counts = jnp.sum(sel, axis=0)
    gsize = ((counts + tm - 1) // tm) * tm
    gend = jnp.cumsum(gsize)
    goff = gend - gsize
    dest = goff[None, :] + rank
    e0 = jnp.argmax(sel, axis=1)
    e1 = jnp.max(sel * jnp.arange(n_exp, dtype=I32)[None, :], axis=1)
    pos0 = jnp.take_along_axis(dest, e0[:, None], axis=1)[:, 0].astype(I32)
    pos1 = jnp.take_along_axis(dest, e1[:, None], axis=1)[:, 0].astype(I32)
    wt0 = jnp.take_along_axis(comb, e0[:, None], axis=1)
    wt1 = jnp.take_along_axis(comb, e1[:, None], axis=1)
    p = t * TOP_K + n_exp * tm
    tok = jnp.arange(t, dtype=I32)
    src_tok = jnp.zeros((p,), I32).at[jnp.concatenate([pos0, pos1])].set(jnp.concatenate([tok, tok]))
    n_tiles = p // tm
    tstart = jnp.arange(n_tiles, dtype=I32) * tm
    tile_valid = (tstart < gend[-1]).astype(I32)
    te_raw = jnp.minimum(jnp.sum((tstart[:, None] >= gend[None, :]).astype(I32), axis=1), n_exp - 1)
    te_last = te_raw[jnp.maximum(gend[-1] // tm - 1, 0)]
    tile_expert = jnp.where(tile_valid == 1, te_raw, te_last).astype(I32)

    hs = _dispatch(x2, g, src_tok, rows=_tile(p, ROWS_GATHER))
    us = _moe_up(hs, w1, w3, tile_expert, tile_valid, (goff // tm).astype(I32), (gsize // tm).astype(I32), tm)
    ys = _moe_down(us, w2, tile_expert, tile_valid, tm)
    return _combine(x2, ys, pos0, pos1, wt0, wt1, rows=_tile(t, ROWS_GATHER))


def _pack_kernel(ia_ref, ib_ref, sh_ref, nv_ref, a_ref, b_ref, o_ref, *, shift):
    j = pl.program_id(0)
    a = a_ref[0]
    tn = a.shape[1]
    shifted = jnp.concatenate([a[:, shift:], b_ref[0][:, :shift]], axis=1)
    out = jnp.where(sh_ref[j] == 1, shifted, a)
    lane = lax.broadcasted_iota(I32, (1, tn), 1)
    o_ref[...] = jnp.where(lane < nv_ref[j], out, jnp.zeros_like(out))


def _pack_w_in(w_in_bf, layer, d_model, mix_w):
    iq_w = IDX_HEADS * IDX_DH
    tn = iq_w
    n_main = 10 * mix_w
    small = IDX_DH + IDX_HEADS
    gates0 = n_main + iq_w + small
    n_gates = N_BRANCH * d_model
    assert n_main % tn == 0 and n_gates % tn == 0 and small < tn
    shift = gates0 % tn
    ia = ([j for j in range(n_main // tn)]
          + [gates0 // tn + q for q in range(n_gates // tn)]
          + [n_main // tn]
          + [(n_main + iq_w) // tn])
    nblk = len(ia)
    sh = [0] * (n_main // tn) + [1] * (n_gates // tn) + [0, 0]
    nv = [tn] * (nblk - 1) + [small]
    ib = [a + 1 if s_ == 1 else a for a, s_ in zip(ia, sh)]
    assert (max(ib) + 1) * tn >= gates0 + n_gates and max(ib) * tn < w_in_bf.shape[2]

    def amap(j, ia_r, ib_r, sh_r, nv_r):
        return (layer, 0, ia_r[j])

    def bmap(j, ia_r, ib_r, sh_r, nv_r):
        return (layer, 0, ib_r[j])

    return pl.pallas_call(
        functools.partial(_pack_kernel, shift=shift),
        out_shape=jax.ShapeDtypeStruct((d_model, nblk * tn), BF16),
        grid_spec=pltpu.PrefetchScalarGridSpec(
            num_scalar_prefetch=4,
            grid=(nblk,),
            in_specs=[pl.BlockSpec((1, d_model, tn), amap),
                      pl.BlockSpec((1, d_model, tn), bmap)],
            out_specs=pl.BlockSpec((d_model, tn), lambda j, ia_r, ib_r, sh_r, nv_r: (0, j))),
        compiler_params=_cparams(("arbitrary",)),
        name="pack_w_in",
    )(jnp.asarray(ia, I32), jnp.asarray(ib, I32), jnp.asarray(sh, I32), jnp.asarray(nv, I32),
      w_in_bf, w_in_bf)


def _mixer(x2, b, s, positions, rel_bias, norm_g, w_in_bf, layer, w_br_l, w_o_l, gn_w, gn_b, conv_w,
           q_norm, k_norm):
    t, d_model = x2.shape
    mix_w = d_model // 2
    iq_w = IDX_HEADS * IDX_DH
    gate_col0 = 10 * mix_w
    iq_col0 = gate_col0 + N_BRANCH * d_model
    assert iq_col0 % iq_w == 0

    h = _rmsnorm(x2, norm_g)
    proj2 = _matmul(h, _pack_w_in(w_in_bf, layer, d_model, mix_w), BF16, *TILE_IN_PROJ, name="in_proj")
    proj3 = proj2.reshape(b, s, proj2.shape[1])
    pos_q = positions.reshape(b, s, 1)
    pos_k = positions.reshape(b, 1, s)

    y_ret = _retention_branch(proj3, pos_q, gn_w, gn_b, mix_w)
    y_conv = _conv_branch(proj3, conv_w, mix_w)

    ik0 = iq_col0 + iq_w
    kit = jnp.swapaxes(proj3[:, :, ik0:ik0 + IDX_DH], 1, 2)
    mask = _dsa_select(proj3, kit, s, min(TOPK_MAX, s // 4), iq_col0 // iq_w)
    qn, kn = _qk_norm(proj3, q_norm, k_norm, mix_w)
    y_att = _dsa_attention(qn, kn, proj3, mask, pos_q, pos_k, rel_bias, mix_w)

    ys = [y.reshape(t, mix_w) for y in (y_ret, y_conv, y_att)]
    merged = _branch_merge(ys, w_br_l.astype(BF16), proj2, d_model, mix_w, gate_col0)
    return _matmul(merged, w_o_l.astype(BF16), F32, *TILE_OUT_PROJ, res=x2, name="out_proj")


def kernel(x, positions, rel_bias, norm_mix, norm_ffn, w_in, w_br, w_o, ret_gn_w, ret_gn_b, conv_w,
           q_norm, k_norm, ffn_w1, ffn_w3, ffn_w2, moe_router, moe_w1, moe_w3, moe_w2):
    b, s, d_model = x.shape
    depth = w_in.shape[0]
    x2 = x.reshape(b * s, d_model)
    w_in_bf = w_in.astype(BF16)
    for l in range(depth):
        x2 = _mixer(x2, b, s, positions, rel_bias, norm_mix[l], w_in_bf, l, w_br[l], w_o[l],
                    ret_gn_w[l], ret_gn_b[l], conv_w[l], q_norm[l], k_norm[l])
        if l % 2 == 0:
            i = l // 2
            u = _glu(x2, norm_ffn[l], ffn_w1[i].astype(BF16), ffn_w3[i].astype(BF16))
            x2 = _matmul(u, ffn_w2[i].astype(BF16), F32, *TILE_FFN_DOWN, res=x2, name="ffn_down")
        else:
            i = l // 2
            x2 = _moe(x2, norm_ffn[l], moe_router[i], moe_w1[i], moe_w3[i], moe_w2[i].astype(BF16))
    return x2.reshape(b, s, d_model)
```

```python
import functools
import math

import numpy as np
import jax
import jax.numpy as jnp
from jax import lax
from jax.experimental import pallas as pl
from jax.experimental.pallas import tpu as pltpu

F32 = jnp.float32
BF16 = jnp.bfloat16
I32 = jnp.int32

EPS = 1e-6
N_BRANCH = 3
RET_HEADS = 8
RET_CHUNK = 128
RET_ROPE_BASE = 10000.0
CONV_K = 3
ATT_HEADS = 8
IDX_HEADS = 8
IDX_DH = 64
TOPK_MAX = 256
REL_BUCKETS = 32
REL_MAX_DIST = 128
TOP_K = 2

LANES = 128
SUBLANES = 8
VMEM_LIMIT_BYTES = 56 * 1024 * 1024
ROW_DMA_UNROLL = 8

TM_ROWWISE = 512
TILE_IN_PROJ = (2048, 1024)
TILE_OUT_PROJ = (2048, 512)
TILE_FFN_DOWN = (1024, 512)
TILE_FFN_UP = (1024, 512)
TILE_MERGE = (1024, 512)
TQ_SELECT, CK_SELECT = 256, 512
T_ATTN = 512
TM_MOE, TN_MOE = 512, 512
ROWS_GATHER = 256

NEG_MASK = -1e30
LOG2E = math.log2(math.e)
INT_MIN = -2 ** 31
KEY_BITS = 32


def _cparams(sem):
    return pltpu.CompilerParams(dimension_semantics=sem, vmem_limit_bytes=VMEM_LIMIT_BYTES)


def _sigmoid(x):
    return 1.0 / (1.0 + jnp.exp(-x))


def _tile(n, pref):
    t = min(n, pref)
    assert n % t == 0, (n, pref)
    return t


def _rmsnorm_kernel(x_ref, g_ref, o_ref):
    x = x_ref[...]
    ms = jnp.mean(x * x, axis=-1, keepdims=True)
    o_ref[...] = (x * lax.rsqrt(ms + EPS) * g_ref[...]).astype(o_ref.dtype)


def _rmsnorm(x2, g):
    t, d = x2.shape
    tm = _tile(t, TM_ROWWISE)
    return pl.pallas_call(
        _rmsnorm_kernel,
        out_shape=jax.ShapeDtypeStruct((t, d), BF16),
        grid=(t // tm,),
        in_specs=[pl.BlockSpec((tm, d), lambda i: (i, 0)),
                  pl.BlockSpec((1, d), lambda i: (0, 0))],
        out_specs=pl.BlockSpec((tm, d), lambda i: (i, 0)),
        compiler_params=_cparams(("parallel",)),
        name="rmsnorm",
    )(x2, g.reshape(1, d))


def _mm_kernel(a_ref, b_ref, o_ref):
    o_ref[...] = jnp.dot(a_ref[...], b_ref[...], preferred_element_type=F32).astype(o_ref.dtype)


def _mm_res_kernel(a_ref, b_ref, r_ref, o_ref):
    o_ref[...] = (r_ref[...] + jnp.dot(a_ref[...], b_ref[...], preferred_element_type=F32)).astype(o_ref.dtype)


def _matmul(a, b, out_dtype, tm, tn, res=None, name="matmul"):
    m, k = a.shape
    k2, n = b.shape
    assert k == k2
    tm = _tile(m, tm)
    tn = _tile(n, tn)
    in_specs = [pl.BlockSpec((tm, k), lambda i, j: (i, 0)),
                pl.BlockSpec((k, tn), lambda i, j: (0, j))]
    args = [a, b]
    kern = _mm_kernel
    if res is not None:
        in_specs.append(pl.BlockSpec((tm, tn), lambda i, j: (i, j)))
        args.append(res)
        kern = _mm_res_kernel
    return pl.pallas_call(
        kern,
        out_shape=jax.ShapeDtypeStruct((m, n), out_dtype),
        grid=(m // tm, n // tn),
        in_specs=in_specs,
        out_specs=pl.BlockSpec((tm, tn), lambda i, j: (i, j)),
        compiler_params=_cparams(("parallel", "parallel")),
        name=name,
    )(*args)


def _rope_kernel(pos_ref, invf_ref, cos_ref, sin_ref):
    ang = pos_ref[0].astype(F32) * invf_ref[...]
    sin = jnp.sin(ang)
    dh = ang.shape[1]
    lane = lax.broadcasted_iota(I32, ang.shape, 1)
    cos_ref[0] = jnp.cos(ang)
    sin_ref[0] = jnp.where(lane < dh // 2, -sin, sin)


def _rope_tables(pos3, dh):
    b, s, _ = pos3.shape
    ts = _tile(s, TM_ROWWISE)
    inv_freq = RET_ROPE_BASE ** (-jnp.arange(0, dh, 2, dtype=F32) / dh)
    invf = jnp.concatenate([inv_freq, inv_freq]).reshape(1, dh)
    out = jax.ShapeDtypeStruct((b, s, dh), F32)
    blk = pl.BlockSpec((1, ts, dh), lambda bi, n: (bi, n, 0))
    return pl.pallas_call(
        _rope_kernel,
        out_shape=(out, out),
        grid=(b, s // ts),
        in_specs=[pl.BlockSpec((1, ts, 1), lambda bi, n: (bi, n, 0)),
                  pl.BlockSpec((1, dh), lambda bi, n: (0, 0))],
        out_specs=(blk, blk),
        compiler_params=_cparams(("parallel", "parallel")),
        name="rope_tables",
    )(pos3, invf)


def _ret_kernel(cos_ref, sin_ref, q_ref, k_ref, v_ref, g_ref, dec_ref, kdec_ref, qdec_ref,
                gnw_ref, gnb_ref, o_ref, state_ref, *, heads, dh, chunk_decay):
    @pl.when(pl.program_id(1) == 0)
    def _():
        state_ref[...] = jnp.zeros_like(state_ref)

    cos = cos_ref[0]
    sin_signed = sin_ref[0]
    scale = dh ** -0.5
    for h in range(heads):
        sl = slice(h * dh, (h + 1) * dh)
        q = q_ref[0, :, sl].astype(F32)
        k = k_ref[0, :, sl].astype(F32)
        v = v_ref[0, :, sl]
        qr = (q * cos + pltpu.roll(q, dh // 2, 1) * sin_signed).astype(BF16)
        kr = (k * cos + pltpu.roll(k, dh // 2, 1) * sin_signed) * scale
        scores = lax.dot_general(qr, kr.astype(BF16), (((1,), (1,)), ((), ())),
                                 preferred_element_type=F32) * dec_ref[h]
        intra = jnp.dot(scores.astype(BF16), v, preferred_element_type=F32)
        prev = state_ref[h]
        cross = jnp.dot(qr, prev.astype(BF16), preferred_element_type=F32) * qdec_ref[h]
        kd = (kr * kdec_ref[h]).astype(BF16)
        kv = lax.dot_general(kd, v, (((0,), (0,)), ((), ())), preferred_element_type=F32)
        state_ref[h] = prev * chunk_decay[h] + kv
        ret = intra + cross
        mu = jnp.mean(ret, axis=-1, keepdims=True)
        cen = ret - mu
        var = jnp.mean(cen * cen, axis=-1, keepdims=True)
        gn = cen * lax.rsqrt(var + EPS) * gnw_ref[:, sl] + gnb_ref[:, sl]
        g = g_ref[0, :, sl].astype(F32)
        o_ref[0, :, sl] = (g * _sigmoid(g) * gn).astype(o_ref.dtype)


def _retention_branch(proj3, rope, gn_w, gn_b, mix_w):
    b, s, _ = proj3.shape
    rope_cos, rope_sin = rope
    heads, c = RET_HEADS, RET_CHUNK
    dh = mix_w // heads
    assert dh == LANES and s % c == 0
    log_g = np.log1p(-np.exp2(-5.0 - np.arange(heads, dtype=np.float64)))
    pos = np.arange(c, dtype=np.float64)
    diff = pos[:, None] - pos[None, :]
    intra_decay = np.where(diff >= 0, np.exp(np.maximum(diff, 0.0)[None] * log_g[:, None, None]), 0.0)
    k_decay = np.exp((c - 1 - pos)[None, :] * log_g[:, None])
    q_decay = np.exp((pos + 1)[None, :] * log_g[:, None])
    chunk_decay = tuple(float(x) for x in np.exp(c * log_g))
    dec = jnp.asarray(intra_decay, F32)
    kdec = jnp.asarray(np.broadcast_to(k_decay[:, :, None], (heads, c, dh)), F32)
    qdec = jnp.asarray(np.broadcast_to(q_decay[:, :, None], (heads, c, dh)), F32)

    def col(cb):
        return pl.BlockSpec((1, c, mix_w), lambda bi, n, cb=cb: (bi, n, cb))

    const3 = pl.BlockSpec((heads, c, dh), lambda bi, n: (0, 0, 0))
    return pl.pallas_call(
        functools.partial(_ret_kernel, heads=heads, dh=dh, chunk_decay=chunk_decay),
        out_shape=jax.ShapeDtypeStruct((b, s, mix_w), BF16),
        grid=(b, s // c),
        in_specs=[pl.BlockSpec((1, c, dh), lambda bi, n: (bi, n, 0)),
                  pl.BlockSpec((1, c, dh), lambda bi, n: (bi, n, 0)),
                  col(0), col(1), col(2), col(3),
                  const3, const3, const3,
                  pl.BlockSpec((1, mix_w), lambda bi, n: (0, 0)),
                  pl.BlockSpec((1, mix_w), lambda bi, n: (0, 0))],
        out_specs=pl.BlockSpec((1, c, mix_w), lambda bi, n: (bi, n, 0)),
        scratch_shapes=[pltpu.VMEM((heads, dh, dh), F32)],
        compiler_params=_cparams(("arbitrary", "arbitrary")),
        name="retention",
    )(rope_cos, rope_sin, proj3, proj3, proj3, proj3, dec, kdec, qdec,
      gn_w.reshape(1, mix_w), gn_b.reshape(1, mix_w))


def _conv_kernel(b_ref, c_ref, u_ref, w_ref, o_ref, ubuf_ref):
    ts = o_ref.shape[1]
    halo = SUBLANES

    @pl.when(pl.program_id(1) == 0)
    def _():
        ubuf_ref[0:halo, :] = jnp.zeros((halo, ubuf_ref.shape[1]), F32)

    u = c_ref[0].astype(F32) * u_ref[0].astype(F32)
    ubuf_ref[halo:halo + ts, :] = u
    u1 = ubuf_ref[halo - 1:halo - 1 + ts, :]
    u2 = ubuf_ref[halo - 2:halo - 2 + ts, :]
    conv = w_ref[0:1, :] * u2 + w_ref[1:2, :] * u1 + w_ref[2:3, :] * u
    o_ref[0] = (b_ref[0].astype(F32) * conv).astype(o_ref.dtype)
    ubuf_ref[0:halo, :] = ubuf_ref[ts:ts + halo, :]


def _conv_branch(proj3, conv_w, mix_w):
    b, s, _ = proj3.shape
    ts = _tile(s, TM_ROWWISE)

    def col(cb):
        return pl.BlockSpec((1, ts, mix_w), lambda bi, n, cb=cb: (bi, n, cb))

    return pl.pallas_call(
        _conv_kernel,
        out_shape=jax.ShapeDtypeStruct((b, s, mix_w), BF16),
        grid=(b, s // ts),
        in_specs=[col(4), col(5), col(6),
                  pl.BlockSpec((CONV_K, mix_w), lambda bi, n: (0, 0))],
        out_specs=pl.BlockSpec((1, ts, mix_w), lambda bi, n: (bi, n, 0)),
        scratch_shapes=[pltpu.VMEM((ts + SUBLANES, mix_w), F32)],
        compiler_params=_cparams(("arbitrary", "arbitrary")),
        name="short_conv",
    )(proj3, proj3, proj3, conv_w)


def _qknorm_kernel(q_ref, k_ref, qn_ref, kn_ref, qo_ref, ko_ref, *, heads, dh, q_scale):
    for h in range(heads):
        sl = slice(h * dh, (h + 1) * dh)
        q = q_ref[0, :, sl].astype(F32)
        k = k_ref[0, :, sl].astype(F32)
        qy = q * lax.rsqrt(jnp.mean(q * q, axis=-1, keepdims=True) + EPS) * qn_ref[...]
        ky = k * lax.rsqrt(jnp.mean(k * k, axis=-1, keepdims=True) + EPS) * kn_ref[...]
        qo_ref[0, :, sl] = (qy * q_scale).astype(qo_ref.dtype)
        ko_ref[0, :, sl] = ky.astype(ko_ref.dtype)


def _qk_norm(proj3, q_norm, k_norm, mix_w):
    b, s, _ = proj3.shape
    heads = ATT_HEADS
    dh = mix_w // heads
    ts = _tile(s, TM_ROWWISE)
    out = jax.ShapeDtypeStruct((b, s, mix_w), BF16)
    blk = pl.BlockSpec((1, ts, mix_w), lambda bi, n: (bi, n, 0))
    return pl.pallas_call(
        functools.partial(_qknorm_kernel, heads=heads, dh=dh, q_scale=dh ** -0.5 * LOG2E),
        out_shape=(out, out),
        grid=(b, s // ts),
        in_specs=[pl.BlockSpec((1, ts, mix_w), lambda bi, n: (bi, n, 7)),
                  pl.BlockSpec((1, ts, mix_w), lambda bi, n: (bi, n, 8)),
                  pl.BlockSpec((1, dh), lambda bi, n: (0, 0)),
                  pl.BlockSpec((1, dh), lambda bi, n: (0, 0))],
        out_specs=(blk, blk),
        compiler_params=_cparams(("parallel", "parallel")),
        name="qk_norm",
    )(proj3, proj3, q_norm.reshape(1, dh), k_norm.reshape(1, dh))


def _select_kernel(iq_ref, sm_ref, kit_ref, tri_ref, o_ref, qall_ref, keys_ref, planes_ref, eq_ref, *,
                   tq, ck, ksel, heads, d_idx):
    qb = pl.program_id(1)
    row0 = qb * tq
    nch = (row0 + tq + ck - 1) // ck
    grp = KEY_BITS * LANES
    ngrp = keys_ref.shape[1] // grp
    nlive = (row0 + tq + grp - 1) // grp
    for h in range(heads):
        qall_ref[h * tq:(h + 1) * tq, :] = iq_ref[0, :, h * d_idx:(h + 1) * d_idx]
    w = sm_ref[0, :, d_idx:d_idx + heads].astype(F32) * (heads ** -0.5 * d_idx ** -0.5)
    t_col = row0 + lax.broadcasted_iota(I32, (tq, 1), 0)
    lane_ck = lax.broadcasted_iota(I32, (1, ck), 1)

    def score_chunk(c, carry):
        off = pl.multiple_of(c * ck, ck)
        kt = kit_ref[0, :, pl.ds(off, ck)]
        sc = jnp.zeros((tq, ck), F32)
        for h in range(heads):
            d = jnp.dot(qall_ref[h * tq:(h + 1) * tq, :], kt, preferred_element_type=F32)
            sc = sc + w[:, h:h + 1] * jnp.maximum(d, 0.0)
        bits = pltpu.bitcast(sc, I32)
        key = bits ^ ((bits >> 31) & 0x7FFFFFFF)
        keys_ref[:, pl.ds(off, ck)] = jnp.where(off + lane_ck <= t_col, key, INT_MIN)
        return carry

    def clear_chunk(c, carry):
        keys_ref[:, pl.ds(pl.multiple_of(c * ck, ck), ck)] = jnp.full((tq, ck), INT_MIN, I32)
        return carry

    lax.fori_loop(0, nch, score_chunk, 0)
    lax.fori_loop(nch, nlive * (grp // ck), clear_chunk, 0)

    def transpose_group(g):
        def rows(r, carry):
            rs = pl.ds(pl.multiple_of(r * SUBLANES, SUBLANES), SUBLANES)
            a = [keys_ref[rs, (g * KEY_BITS + j) * LANES:(g * KEY_BITS + j + 1) * LANES] ^ INT_MIN
                 for j in range(KEY_BITS)]
            j, m = KEY_BITS // 2, (1 << (KEY_BITS // 2)) - 1
            while j:
                k = 0
                while k < KEY_BITS:
                    t = (a[k] ^ (a[k + j] >> j)) & m
                    a[k] = a[k] ^ t
                    a[k + j] = a[k + j] ^ (t << j)
                    k = (k + j + 1) & ~j
                j >>= 1
                m = m ^ (m << j)
            for p in range(KEY_BITS):
                planes_ref[p, g, rs, :] = a[p]
            return carry

        lax.fori_loop(0, tq // SUBLANES, rows, 0)

    @pl.when(qb == 0)
    def _():
        planes_ref[...] = jnp.zeros_like(planes_ref)

    for g in range(ngrp):
        @pl.when(g < nlive)
        def _(g=g):
            transpose_group(g)
            eq_ref[g] = jnp.full(eq_ref.shape[1:], -1, I32)

        @pl.when(g >= nlive)
        def _(g=g):
            eq_ref[g] = jnp.zeros(eq_ref.shape[1:], I32)

    kf = float(ksel)

    def bit_pass(p, carry):
        above, thr_u = carry
        ones = jnp.zeros((tq, LANES), I32)
        for g in range(ngrp):
            ones = ones + lax.population_count(eq_ref[g] & planes_ref[p, g])
        c1 = jnp.sum(ones.astype(F32), axis=1, keepdims=True)
        take = (above + c1) >= kf
        for g in range(ngrp):
            e = eq_ref[g]
            x = e & planes_ref[p, g]
            eq_ref[g] = jnp.where(take, x, e ^ x)
        bit = jnp.left_shift(jnp.int32(1), KEY_BITS - 1 - p)
        return jnp.where(take, above, above + c1), jnp.where(take, thr_u | bit, thr_u)

    cnt_gt, thr_u = lax.fori_loop(0, KEY_BITS, bit_pass,
                                  (jnp.zeros((tq, 1), F32), jnp.zeros((tq, 1), I32)))
    thr = thr_u ^ INT_MIN
    eq_cnt = jnp.zeros((tq, LANES), I32)
    for g in range(ngrp):
        eq_cnt = eq_cnt + lax.population_count(eq_ref[g])
    cnt_eq = jnp.sum(eq_cnt.astype(F32), axis=1, keepdims=True)

    need = kf - cnt_gt
    excess = jnp.where(thr == INT_MIN, 0.0, cnt_eq - need)
    o_ref[...] = jnp.full(o_ref.shape, NEG_MASK, o_ref.dtype)

    def write_chunk(c, eq_before, ties):
        off = pl.multiple_of(c * ck, ck)
        sl = pl.ds(off, ck)
        k = keys_ref[:, sl]
        if ties:
            eq = jnp.where(k == thr, 1.0, 0.0).astype(BF16)
            rank = eq_before + jnp.dot(eq, tri_ref[...], preferred_element_type=F32)
            on_eq = jnp.where(rank <= need, 0.0, NEG_MASK)
            eq_before = rank[:, ck - 1:ck]
        else:
            on_eq = 0.0
        val = jnp.where(k > thr, 0.0, jnp.where(k == thr, on_eq, NEG_MASK))
        o_ref[0, :, sl] = jnp.where(off + lane_ck <= t_col, val, NEG_MASK).astype(o_ref.dtype)
        return eq_before

    any_excess = jnp.max(excess) > 0.0

    @pl.when(any_excess)
    def _():
        lax.fori_loop(0, nch, lambda c, e: write_chunk(c, e, True), jnp.zeros((tq, 1), F32))

    @pl.when(jnp.logical_not(any_excess))
    def _():
        lax.fori_loop(0, nch, lambda c, e: write_chunk(c, e, False), 0)


def _dsa_select(proj3, kit, s, ksel, iq_block):
    b = proj3.shape[0]
    tq = _tile(s, TQ_SELECT)
    ck = _tile(s, CK_SELECT)
    heads, d_idx = IDX_HEADS, IDX_DH
    iq_w = heads * d_idx
    assert (s % (KEY_BITS * LANES) == 0 or s < KEY_BITS * LANES) and (KEY_BITS * LANES) % ck == 0
    sp = max(s, KEY_BITS * LANES)
    tri = (jnp.arange(ck)[:, None] <= jnp.arange(ck)[None, :]).astype(BF16)
    return pl.pallas_call(
        functools.partial(_select_kernel, tq=tq, ck=ck, ksel=ksel, heads=heads, d_idx=d_idx),
        out_shape=jax.ShapeDtypeStruct((b, s, s), BF16),
        grid=(b, s // tq),
        in_specs=[pl.BlockSpec((1, tq, iq_w), lambda bi, n: (bi, n, iq_block)),
                  pl.BlockSpec((1, tq, iq_w), lambda bi, n: (bi, n, iq_block + 1)),
                  pl.BlockSpec((1, d_idx, s), lambda bi, n: (bi, 0, 0)),
                  pl.BlockSpec((ck, ck), lambda bi, n: (0, 0))],
        out_specs=pl.BlockSpec((1, tq, s), lambda bi, n: (bi, n, 0)),
        scratch_shapes=[pltpu.VMEM((heads * tq, d_idx), BF16),
                        pltpu.VMEM((tq, sp), I32),
                        pltpu.VMEM((KEY_BITS, sp // (KEY_BITS * LANES), tq, LANES), I32),
                        pltpu.VMEM((sp // (KEY_BITS * LANES), tq, LANES), I32)],
        compiler_params=_cparams(("arbitrary", "arbitrary")),
        name="dsa_select",
    )(proj3, proj3, kit, tri)


def _t5_bucket(n):
    max_exact = REL_BUCKETS // 2
    nf = jnp.maximum(n, 1).astype(F32)
    large = max_exact + (jnp.log(nf / max_exact) / math.log(REL_MAX_DIST / max_exact)
                         * (REL_BUCKETS - max_exact)).astype(I32)
    large = jnp.minimum(large, REL_BUCKETS - 1)
    return jnp.where(n < max_exact, n, large)


def _attn_kernel(qb_ref, kb_ref, rb_ref, plo_ref, phi_ref, q_ref, k_ref, v_ref, mask_ref, pq_ref, pk_ref,
                 o_ref, acc_ref, m_ref, l_ref, bias_ref, s_ref, p_ref, *, tq, tk, heads, dh):
    qb = qb_ref[pl.program_id(1)]
    kb = kb_ref[pl.program_id(1)]
    kmax = ((qb + 1) * tq - 1) // tk
    nlt = tk // LANES
    gran = LANES

    @pl.when(kb == 0)
    def _():
        acc_ref[...] = jnp.zeros_like(acc_ref)
        m_ref[...] = jnp.full(m_ref.shape, -jnp.inf, F32)
        l_ref[...] = jnp.zeros_like(l_ref)

    def attend(bias_of):
        ones = jnp.ones((tk, LANES), BF16)
        for h in range(heads):
            sl = slice(h * dh, (h + 1) * dh)
            kh = k_ref[0, :, sl]
            v_aug = jnp.concatenate([v_ref[0, :, sl], ones], axis=1)
            add, const = bias_of(h)
            buf = h % 2
            s = lax.dot_general(q_ref[0, :, sl], kh, (((1,), (1,)), ((), ())),
                                preferred_element_type=F32) + add
            s_ref[buf] = s
            tile_max = s[:, 0:LANES]
            for c in range(1, nlt):
                tile_max = jnp.maximum(tile_max, s[:, c * LANES:(c + 1) * LANES])
            m_cur = jnp.broadcast_to(jnp.max(tile_max, axis=1, keepdims=True), (tq, LANES)) + const
            m_old = m_ref[h]
            m_new = jnp.maximum(m_old, m_cur)
            alpha = jnp.exp2(m_old - m_new)
            shift = m_new - const
            for c in range(nlt):
                cs = slice(c * LANES, (c + 1) * LANES)
                p_ref[buf, :, cs] = jnp.exp2(s_ref[buf, :, cs] - shift).astype(BF16)
            pv = jnp.dot(p_ref[buf], v_aug, preferred_element_type=F32)
            m_ref[h] = m_new
            l_ref[h] = alpha * l_ref[h] + pv[:, dh:]
            acc_ref[:, sl] = alpha * acc_ref[:, sl] + pv[:, :dh]

    @pl.when(kb <= kmax)
    def _():
        pq = pq_ref[0]
        pk = pk_ref[0]
        bi = pl.program_id(0)
        q_lo = [plo_ref[bi, qb * (tq // gran) + ri] for ri in range(tq // gran)]
        q_hi = [phi_ref[bi, qb * (tq // gran) + ri] for ri in range(tq // gran)]
        k_lo = [plo_ref[bi, kb * (tk // gran) + cj] for cj in range(tk // gran)]
        k_hi = [phi_ref[bi, kb * (tk // gran) + cj] for cj in range(tk // gran)]
        all_far = (functools.reduce(jnp.minimum, q_lo) - functools.reduce(jnp.maximum, k_hi)) >= REL_MAX_DIST

        @pl.when(all_far)
        def _():
            attend(lambda h: (mask_ref[0].astype(F32), rb_ref[REL_BUCKETS - 1, h] * LOG2E))

        @pl.when(jnp.logical_not(all_far))
        def _():
            n_lane = lax.broadcasted_iota(I32, (1, REL_MAX_DIST), 1)
            bucket = _t5_bucket(n_lane)
            tabs = []
            for h in range(heads):
                tab = jnp.zeros((1, REL_MAX_DIST), F32)
                for j in range(REL_BUCKETS):
                    tab = jnp.where(bucket == j, rb_ref[j, h] * LOG2E, tab)
                tabs.append(jnp.broadcast_to(tab, (gran, REL_MAX_DIST)))
            for ri in range(tq // gran):
                rs = slice(ri * gran, (ri + 1) * gran)
                pq_g = pq[rs]
                for cj in range(tk // gran):
                    cs = slice(cj * gran, (cj + 1) * gran)
                    pk_g = pk[:, cs]
                    lo = q_lo[ri] - k_hi[cj]
                    hi = q_hi[ri] - k_lo[cj]
                    is_far = lo >= REL_MAX_DIST
                    is_zero = hi <= 0
                    maskf = mask_ref[0, rs, cs].astype(F32)

                    @pl.when(is_far | is_zero)
                    def _():
                        for h in range(heads):
                            c_h = jnp.where(is_far, rb_ref[REL_BUCKETS - 1, h], rb_ref[0, h]) * LOG2E
                            bias_ref[h, rs, cs] = maskf + c_h

                    @pl.when(jnp.logical_not(is_far | is_zero))
                    def _():
                        dist = jnp.clip(pq_g - pk_g, 0, REL_MAX_DIST - 1)
                        for h in range(heads):
                            bias_ref[h, rs, cs] = maskf + jnp.take_along_axis(tabs[h], dist, axis=1)
            attend(lambda h: (bias_ref[h], 0.0))

    @pl.when(kb == kmax)
    def _():
        for h in range(heads):
            sl = slice(h * dh, (h + 1) * dh)
            o_ref[0, :, sl] = (acc_ref[:, sl] / l_ref[h]).astype(o_ref.dtype)


def _dsa_attention(qn, kn, proj3, mask, pos_q, pos_k, rel_bias, mix_w):
    b, s, _ = qn.shape
    heads = ATT_HEADS
    dh = mix_w // heads
    tq = _tile(s, T_ATTN)
    tk = tq
    assert REL_MAX_DIST == LANES

    tiles = [(qb, kb) for qb in range(s // tq) for kb in range(((qb + 1) * tq - 1) // tk + 1)]
    qb_of = jnp.asarray([t[0] for t in tiles], I32)
    kb_of = jnp.asarray([t[1] for t in tiles], I32)
    pos_g = pos_k.reshape(b, s // LANES, LANES)
    pos_lo = jnp.min(pos_g, axis=-1)
    pos_hi = jnp.max(pos_g, axis=-1)

    return pl.pallas_call(
        functools.partial(_attn_kernel, tq=tq, tk=tk, heads=heads, dh=dh),
        out_shape=jax.ShapeDtypeStruct((b, s, mix_w), BF16),
        grid_spec=pltpu.PrefetchScalarGridSpec(
            num_scalar_prefetch=2,
            grid=(b, len(tiles)),
            in_specs=[pl.BlockSpec(memory_space=pltpu.SMEM),
                      pl.BlockSpec(memory_space=pltpu.SMEM),
                      pl.BlockSpec(memory_space=pltpu.SMEM),
                      pl.BlockSpec((1, tq, mix_w), lambda bi, st, qo, ko: (bi, qo[st], 0)),
                      pl.BlockSpec((1, tk, mix_w), lambda bi, st, qo, ko: (bi, ko[st], 0)),
                      pl.BlockSpec((1, tk, mix_w), lambda bi, st, qo, ko: (bi, ko[st], 9)),
                      pl.BlockSpec((1, tq, tk), lambda bi, st, qo, ko: (bi, qo[st], ko[st])),
                      pl.BlockSpec((1, tq, 1), lambda bi, st, qo, ko: (bi, qo[st], 0)),
                      pl.BlockSpec((1, 1, tk), lambda bi, st, qo, ko: (bi, 0, ko[st]))],
            out_specs=pl.BlockSpec((1, tq, mix_w), lambda bi, st, qo, ko: (bi, qo[st], 0)),
            scratch_shapes=[pltpu.VMEM((tq, mix_w), F32),
                            pltpu.VMEM((heads, tq, LANES), F32),
                            pltpu.VMEM((heads, tq, LANES), F32),
                            pltpu.VMEM((heads, tq, tk), F32),
                            pltpu.VMEM((2, tq, tk), F32),
                            pltpu.VMEM((2, tq, tk), BF16)]),
        compiler_params=_cparams(("parallel", "arbitrary")),
        name="dsa_attention",
    )(qb_of, kb_of, rel_bias, pos_lo, pos_hi, qn, kn, proj3, mask, pos_q, pos_k)


def _merge_kernel(y0_ref, y1_ref, y2_ref, w_ref, g0_ref, g1_ref, g2_ref, o_ref):
    acc = None
    for n, (y_ref, g_ref) in enumerate(((y0_ref, g0_ref), (y1_ref, g1_ref), (y2_ref, g2_ref))):
        br = jnp.dot(y_ref[...], w_ref[n], preferred_element_type=F32)
        term = _sigmoid(g_ref[...].astype(F32)) * br
        acc = term if acc is None else acc + term
    o_ref[...] = acc.astype(o_ref.dtype)


def _branch_merge(ys, w_br, proj2, d_model, mix_w, gate_col0):
    t = proj2.shape[0]
    tm = _tile(t, TILE_MERGE[0])
    tn = _tile(d_model, TILE_MERGE[1])
    yspec = pl.BlockSpec((tm, mix_w), lambda i, j: (i, 0))

    def gspec(n):
        base = (gate_col0 + n * d_model) // tn
        return pl.BlockSpec((tm, tn), lambda i, j, base=base: (i, base + j))

    return pl.pallas_call(
        _merge_kernel,
        out_shape=jax.ShapeDtypeStruct((t, d_model), BF16),
        grid=(t // tm, d_model // tn),
        in_specs=[yspec, yspec, yspec,
                  pl.BlockSpec((N_BRANCH, mix_w, tn), lambda i, j: (0, 0, j)),
                  gspec(0), gspec(1), gspec(2)],
        out_specs=pl.BlockSpec((tm, tn), lambda i, j: (i, j)),
        compiler_params=_cparams(("parallel", "parallel")),
        name="branch_merge",
    )(ys[0], ys[1], ys[2], w_br, proj2, proj2, proj2)


def _glu_kernel(x_ref, g_ref, w1_ref, w3_ref, o_ref, a_ref):
    @pl.when(pl.program_id(1) == 0)
    def _():
        x = x_ref[...]
        ms = jnp.mean(x * x, axis=-1, keepdims=True)
        a_ref[...] = (x * lax.rsqrt(ms + EPS) * g_ref[...]).astype(a_ref.dtype)

    a = a_ref[...]
    h1 = jnp.dot(a, w1_ref[...], preferred_element_type=F32)
    h3 = jnp.dot(a, w3_ref[...], preferred_element_type=F32)
    o_ref[...] = (h1 * _sigmoid(h1) * h3).astype(o_ref.dtype)


def _glu(x2, g, w1, w3):
    t, d = x2.shape
    ff = w1.shape[1]
    tm = _tile(t, TILE_FFN_UP[0])
    tn = _tile(ff, TILE_FFN_UP[1])
    return pl.pallas_call(
        _glu_kernel,
        out_shape=jax.ShapeDtypeStruct((t, ff), BF16),
        grid=(t // tm, ff // tn),
        in_specs=[pl.BlockSpec((tm, d), lambda i, j: (i, 0)),
                  pl.BlockSpec((1, d), lambda i, j: (0, 0)),
                  pl.BlockSpec((d, tn), lambda i, j: (0, j)),
                  pl.BlockSpec((d, tn), lambda i, j: (0, j))],
        out_specs=pl.BlockSpec((tm, tn), lambda i, j: (i, j)),
        scratch_shapes=[pltpu.VMEM((tm, d), BF16)],
        compiler_params=_cparams(("parallel", "arbitrary")),
        name="swiglu_up",
    )(x2, g.reshape(1, d), w1, w3)


def _router_kernel(x_ref, g_ref, r_ref, comb_ref, sel_ref, *, n_exp):
    x = x_ref[...]
    h = (x * lax.rsqrt(jnp.mean(x * x, axis=-1, keepdims=True) + EPS) * g_ref[...]).astype(BF16)
    logits = jnp.dot(h, r_ref[...], preferred_element_type=F32)
    lane = lax.broadcasted_iota(I32, logits.shape, 1)
    lg = jnp.where(lane < n_exp, logits, -jnp.inf)
    v1 = jnp.max(lg, axis=1, keepdims=True)
    i1 = jnp.min(jnp.where(lg == v1, lane, LANES), axis=1, keepdims=True)
    lg2 = jnp.where(lane == i1, -jnp.inf, lg)
    v2 = jnp.max(lg2, axis=1, keepdims=True)
    i2 = jnp.min(jnp.where(lg2 == v2, lane, LANES), axis=1, keepdims=True)
    e = jnp.exp(v2 - v1)
    w1 = 1.0 / (1.0 + e)
    w2 = e / (1.0 + e)
    comb_ref[...] = jnp.where(lane == i1, w1, jnp.where(lane == i2, w2, 0.0))
    sel_ref[...] = jnp.where((lane == i1) | (lane == i2), 1, 0)


def _router(x2, g, router_w):
    t, d = x2.shape
    n_exp = router_w.shape[1]
    tm = _tile(t, TM_ROWWISE)
    rpad = jnp.zeros((d, LANES), BF16).at[:, :n_exp].set(router_w.astype(BF16))
    comb, sel = pl.pallas_call(
        functools.partial(_router_kernel, n_exp=n_exp),
        out_shape=(jax.ShapeDtypeStruct((t, LANES), F32), jax.ShapeDtypeStruct((t, LANES), I32)),
        grid=(t // tm,),
        in_specs=[pl.BlockSpec((tm, d), lambda i: (i, 0)),
                  pl.BlockSpec((1, d), lambda i: (0, 0)),
                  pl.BlockSpec((d, LANES), lambda i: (0, 0))],
        out_specs=(pl.BlockSpec((tm, LANES), lambda i: (i, 0)),
                   pl.BlockSpec((tm, LANES), lambda i: (i, 0))),
        compiler_params=_cparams(("parallel",)),
        name="moe_router",
    )(x2, g.reshape(1, d), rpad)
    return comb[:, :n_exp], sel[:, :n_exp]


def _row_gather(idx_ref, src_hbm, buf_ref, sem, slot, rows, start):
    def body(r8, carry):
        for u in range(ROW_DMA_UNROLL):
            r = r8 * ROW_DMA_UNROLL + u
            cp = pltpu.make_async_copy(src_hbm.at[pl.ds(idx_ref[0, 0, r], 1)],
                                       buf_ref.at[slot, pl.ds(r, 1)], sem.at[slot])
            if start:
                cp.start(priority=u % 2)
            else:
                cp.wait()
        return carry

    lax.fori_loop(0, rows // ROW_DMA_UNROLL, body, 0)


def _dispatch_kernel(cur_ref, nxt_ref, x_hbm, g_ref, o_ref, buf_ref, sem, *, rows, nsteps):
    i = pl.program_id(0)
    slot = i % 2

    @pl.when(i == 0)
    def _():
        _row_gather(cur_ref, x_hbm, buf_ref, sem, 0, rows, start=True)

    @pl.when(i + 1 < nsteps)
    def _():
        _row_gather(nxt_ref, x_hbm, buf_ref, sem, 1 - slot, rows, start=True)

    _row_gather(cur_ref, x_hbm, buf_ref, sem, slot, rows, start=False)
    x = buf_ref[slot]
    o_ref[...] = (x * lax.rsqrt(jnp.mean(x * x, axis=-1, keepdims=True) + EPS) * g_ref[...]).astype(o_ref.dtype)


def _dispatch(x2, g, src_tok, rows):
    t, d = x2.shape
    p = src_tok.shape[0]
    assert p % rows == 0 and rows % ROW_DMA_UNROLL == 0
    nsteps = p // rows
    src3 = src_tok.reshape(nsteps, 1, rows)
    return pl.pallas_call(
        functools.partial(_dispatch_kernel, rows=rows, nsteps=nsteps),
        out_shape=jax.ShapeDtypeStruct((p, d), BF16),
        grid=(nsteps,),
        in_specs=[pl.BlockSpec((1, 1, rows), lambda i: (i, 0, 0), memory_space=pltpu.SMEM),
                  pl.BlockSpec((1, 1, rows), lambda i: (jnp.minimum(i + 1, nsteps - 1), 0, 0),
                               memory_space=pltpu.SMEM),
                  pl.BlockSpec(memory_space=pl.ANY),
                  pl.BlockSpec((1, d), lambda i: (0, 0))],
        out_specs=pl.BlockSpec((rows, d), lambda i: (i, 0)),
        scratch_shapes=[pltpu.VMEM((2, rows, d), F32), pltpu.SemaphoreType.DMA((2,))],
        compiler_params=_cparams(("arbitrary",)),
        name="moe_dispatch",
    )(src3, src3, x2, g.reshape(1, d))


def _moe_up_kernel(ti_ref, tj_ref, wj_ref, te_ref, first_ref, valid_ref, a_ref, w1_ref, w3_ref, o_ref,
                   w1b_ref, w3b_ref):
    s = pl.program_id(0)

    @pl.when(first_ref[s] == 1)
    def _():
        w1b_ref[...] = w1_ref[0].astype(BF16)
        w3b_ref[...] = w3_ref[0].astype(BF16)

    @pl.when(valid_ref[s] == 1)
    def _():
        a = a_ref[...]
        h1 = jnp.dot(a, w1b_ref[...], preferred_element_type=F32)
        h3 = jnp.dot(a, w3b_ref[...], preferred_element_type=F32)
        o_ref[...] = (h1 * _sigmoid(h1) * h3).astype(o_ref.dtype)

    @pl.when(valid_ref[s] == 0)
    def _():
        o_ref[...] = jnp.zeros_like(o_ref)


def _moe_up(hs, w1, w3, tile_expert, tile_valid, gfirst, gtiles, tm):
    p, d = hs.shape
    ff = w1.shape[2]
    tn = _tile(ff, TN_MOE)
    nj = ff // tn
    n_tiles = p // tm
    step = jnp.arange(n_tiles * nj, dtype=I32)
    t0 = step // nj
    e_s = tile_expert[t0]
    valid = tile_valid[t0]
    n_run = jnp.maximum(gtiles[e_s], 1)
    local = step - gfirst[e_s] * nj
    ti = jnp.where(valid == 1, gfirst[e_s] + local % n_run, t0).astype(I32)
    tj = jnp.where(valid == 1, local // n_run, step % nj).astype(I32)
    wj = jnp.where(valid == 1, local // n_run, nj - 1).astype(I32)
    first = jnp.where((valid == 1) & (local % n_run == 0), 1, 0).astype(I32)

    def amap(s, ti_r, tj_r, wj_r, te_r, f_r, v_r):
        return (ti_r[s], 0)

    def wmap(s, ti_r, tj_r, wj_r, te_r, f_r, v_r):
        return (te_r[s], 0, wj_r[s])

    def omap(s, ti_r, tj_r, wj_r, te_r, f_r, v_r):
        return (ti_r[s], tj_r[s])

    return pl.pallas_call(
        _moe_up_kernel,
        out_shape=jax.ShapeDtypeStruct((p, ff), BF16),
        grid_spec=pltpu.PrefetchScalarGridSpec(
            num_scalar_prefetch=6,
            grid=(n_tiles * nj,),
            in_specs=[pl.BlockSpec((tm, d), amap),
                      pl.BlockSpec((1, d, tn), wmap),
                      pl.BlockSpec((1, d, tn), wmap)],
            out_specs=pl.BlockSpec((tm, tn), omap),
            scratch_shapes=[pltpu.VMEM((d, tn), BF16), pltpu.VMEM((d, tn), BF16)]),
        compiler_params=_cparams(("arbitrary",)),
        name="moe_up",
    )(ti, tj, wj, e_s.astype(I32), first, valid.astype(I32), hs, w1, w3)


def _moe_down_kernel(te_ref, tv_ref, a_ref, w_ref, o_ref):
    i = pl.program_id(0)

    @pl.when(tv_ref[i] == 1)
    def _():
        o_ref[...] = jnp.dot(a_ref[...], w_ref[0], preferred_element_type=F32)

    @pl.when(tv_ref[i] == 0)
    def _():
        o_ref[...] = jnp.zeros_like(o_ref)


def _moe_down(us, w2, tile_expert, tile_valid, tm):
    p, ff = us.shape
    d = w2.shape[2]
    tn = _tile(d, TN_MOE)
    nj = d // tn

    def wmap(i, j, te, tv):
        return (te[i], 0, jnp.where(tv[i] == 1, j, nj - 1))

    return pl.pallas_call(
        _moe_down_kernel,
        out_shape=jax.ShapeDtypeStruct((p, d), F32),
        grid_spec=pltpu.PrefetchScalarGridSpec(
            num_scalar_prefetch=2,
            grid=(p // tm, nj),
            in_specs=[pl.BlockSpec((tm, ff), lambda i, j, te, tv: (i, 0)),
                      pl.BlockSpec((1, ff, tn), wmap)],
            out_specs=pl.BlockSpec((tm, tn), lambda i, j, te, tv: (i, j))),
        compiler_params=_cparams(("arbitrary", "arbitrary")),
        name="moe_down",
    )(tile_expert, tile_valid, us, w2)


def _combine_kernel(p0_ref, p1_ref, n0_ref, n1_ref, x_ref, w0_ref, w1_ref, ys_hbm, o_ref,
                    a_ref, b_ref, sem_a, sem_b, *, rows, nsteps):
    i = pl.program_id(0)
    slot = i % 2

    @pl.when(i == 0)
    def _():
        _row_gather(p0_ref, ys_hbm, a_ref, sem_a, 0, rows, start=True)
        _row_gather(p1_ref, ys_hbm, b_ref, sem_b, 0, rows, start=True)

    @pl.when(i + 1 < nsteps)
    def _():
        _row_gather(n0_ref, ys_hbm, a_ref, sem_a, 1 - slot, rows, start=True)
        _row_gather(n1_ref, ys_hbm, b_ref, sem_b, 1 - slot, rows, start=True)

    _row_gather(p0_ref, ys_hbm, a_ref, sem_a, slot, rows, start=False)
    _row_gather(p1_ref, ys_hbm, b_ref, sem_b, slot, rows, start=False)
    o_ref[...] = x_ref[...] + (w0_ref[...] * a_ref[slot] + w1_ref[...] * b_ref[slot])


def _combine(x2, ys, pos0, pos1, w0, w1, rows):
    t, d = x2.shape
    assert t % rows == 0 and rows % ROW_DMA_UNROLL == 0
    nsteps = t // rows
    cur = pl.BlockSpec((1, 1, rows), lambda i: (i, 0, 0), memory_space=pltpu.SMEM)
    nxt = pl.BlockSpec((1, 1, rows), lambda i: (jnp.minimum(i + 1, nsteps - 1), 0, 0),
                       memory_space=pltpu.SMEM)
    p0 = pos0.reshape(nsteps, 1, rows)
    p1 = pos1.reshape(nsteps, 1, rows)
    return pl.pallas_call(
        functools.partial(_combine_kernel, rows=rows, nsteps=nsteps),
        out_shape=jax.ShapeDtypeStruct((t, d), F32),
        grid=(nsteps,),
        in_specs=[cur, cur, nxt, nxt,
                  pl.BlockSpec((rows, d), lambda i: (i, 0)),
                  pl.BlockSpec((rows, 1), lambda i: (i, 0)),
                  pl.BlockSpec((rows, 1), lambda i: (i, 0)),
                  pl.BlockSpec(memory_space=pl.ANY)],
        out_specs=pl.BlockSpec((rows, d), lambda i: (i, 0)),
        scratch_shapes=[pltpu.VMEM((2, rows, d), F32), pltpu.VMEM((2, rows, d), F32),
                        pltpu.SemaphoreType.DMA((2,)), pltpu.SemaphoreType.DMA((2,))],
        compiler_params=_cparams(("arbitrary",)),
        name="moe_combine",
    )(p0, p1, p0, p1, x2, w0, w1, ys)


def _moe(x2, g, router_w, w1, w3, w2):
    t, d = x2.shape
    n_exp = router_w.shape[1]
    tm = _tile(t, TM_MOE)
    comb, sel = _router(x2, g, router_w)
    rank = jnp.cumsum(sel, axis=0) - sel
    counts = jnp.sum(sel, axis=0)
    gsize = ((counts + tm - 1) // tm) * tm
    gend = jnp.cumsum(gsize)
    goff = gend - gsize
    dest = goff[None, :] + rank
    e0 = jnp.argmax(sel, axis=1)
    e1 = jnp.max(sel * jnp.arange(n_exp, dtype=I32)[None, :], axis=1)
    pos0 = jnp.take_along_axis(dest, e0[:, None], axis=1)[:, 0].astype(I32)
    pos1 = jnp.take_along_axis(dest, e1[:, None], axis=1)[:, 0].astype(I32)
    wt0 = jnp.take_along_axis(comb, e0[:, None], axis=1)
    wt1 = jnp.take_along_axis(comb, e1[:, None], axis=1)
    p = t * TOP_K + n_exp * tm
    tok = jnp.arange(t, dtype=I32)
    src_tok = jnp.zeros((p,), I32).at[jnp.concatenate([pos0, pos1])].set(jnp.concatenate([tok, tok]))
    n_tiles = p // tm
    tstart = jnp.arange(n_tiles, dtype=I32) * tm
    tile_valid = (tstart < gend[-1]).astype(I32)
    te_raw = jnp.minimum(jnp.sum((tstart[:, None] >= gend[None, :]).astype(I32), axis=1), n_exp - 1)
    te_last = te_raw[jnp.maximum(gend[-1] // tm - 1, 0)]
    tile_expert = jnp.where(tile_valid == 1, te_raw, te_last).astype(I32)

    hs = _dispatch(x2, g, src_tok, rows=_tile(p, ROWS_GATHER))
    us = _moe_up(hs, w1, w3, tile_expert, tile_valid, (goff // tm).astype(I32), (gsize // tm).astype(I32), tm)
    ys = _moe_down(us, w2, tile_expert, tile_valid, tm)
    return _combine(x2, ys, pos0, pos1, wt0, wt1, rows=_tile(t, ROWS_GATHER))


def _pack_kernel(ia_ref, ib_ref, sh_ref, nv_ref, a_ref, b_ref, o_ref, *, shift):
    j = pl.program_id(0)
    a = a_ref[0]
    tn = a.shape[1]
    shifted = jnp.concatenate([a[:, shift:], b_ref[0][:, :shift]], axis=1)
    out = jnp.where(sh_ref[j] == 1, shifted, a)
    lane = lax.broadcasted_iota(I32, (1, tn), 1)
    o_ref[...] = jnp.where(lane < nv_ref[j], out, jnp.zeros_like(out))


def _pack_w_in(w_in_bf, layer, d_model, mix_w):
    iq_w = IDX_HEADS * IDX_DH
    tn = iq_w
    n_main = 10 * mix_w
    small = IDX_DH + IDX_HEADS
    gates0 = n_main + iq_w + small
    n_gates = N_BRANCH * d_model
    assert n_main % tn == 0 and n_gates % tn == 0 and small < tn
    shift = gates0 % tn
    ia = ([j for j in range(n_main // tn)]
          + [gates0 // tn + q for q in range(n_gates // tn)]
          + [n_main // tn]
          + [(n_main + iq_w) // tn])
    nblk = len(ia)
    sh = [0] * (n_main // tn) + [1] * (n_gates // tn) + [0, 0]
    nv = [tn] * (nblk - 1) + [small]
    ib = [a + 1 if s_ == 1 else a for a, s_ in zip(ia, sh)]
    assert (max(ib) + 1) * tn >= gates0 + n_gates and max(ib) * tn < w_in_bf.shape[2]

    def amap(j, ia_r, ib_r, sh_r, nv_r):
        return (layer, 0, ia_r[j])

    def bmap(j, ia_r, ib_r, sh_r, nv_r):
        return (layer, 0, ib_r[j])

    return pl.pallas_call(
        functools.partial(_pack_kernel, shift=shift),
        out_shape=jax.ShapeDtypeStruct((d_model, nblk * tn), BF16),
        grid_spec=pltpu.PrefetchScalarGridSpec(
            num_scalar_prefetch=4,
            grid=(nblk,),
            in_specs=[pl.BlockSpec((1, d_model, tn), amap),
                      pl.BlockSpec((1, d_model, tn), bmap)],
            out_specs=pl.BlockSpec((d_model, tn), lambda j, ia_r, ib_r, sh_r, nv_r: (0, j))),
        compiler_params=_cparams(("arbitrary",)),
        name="pack_w_in",
    )(jnp.asarray(ia, I32), jnp.asarray(ib, I32), jnp.asarray(sh, I32), jnp.asarray(nv, I32),
      w_in_bf, w_in_bf)


def _mixer(x2, b, s, positions, rope, rel_bias, norm_g, w_in_bf, layer, w_br_l, w_o_l, gn_w, gn_b, conv_w,
           q_norm, k_norm):
    t, d_model = x2.shape
    mix_w = d_model // 2
    iq_w = IDX_HEADS * IDX_DH
    gate_col0 = 10 * mix_w
    iq_col0 = gate_col0 + N_BRANCH * d_model
    assert iq_col0 % iq_w == 0

    h = _rmsnorm(x2, norm_g)
    proj2 = _matmul(h, _pack_w_in(w_in_bf, layer, d_model, mix_w), BF16, *TILE_IN_PROJ, name="in_proj")
    proj3 = proj2.reshape(b, s, proj2.shape[1])
    pos_q = positions.reshape(b, s, 1)
    pos_k = positions.reshape(b, 1, s)

    y_ret = _retention_branch(proj3, rope, gn_w, gn_b, mix_w)
    y_conv = _conv_branch(proj3, conv_w, mix_w)

    ik0 = iq_col0 + iq_w
    kit = jnp.swapaxes(proj3[:, :, ik0:ik0 + IDX_DH], 1, 2)
    mask = _dsa_select(proj3, kit, s, min(TOPK_MAX, s // 4), iq_col0 // iq_w)
    qn, kn = _qk_norm(proj3, q_norm, k_norm, mix_w)
    y_att = _dsa_attention(qn, kn, proj3, mask, pos_q, pos_k, rel_bias, mix_w)

    ys = [y.reshape(t, mix_w) for y in (y_ret, y_conv, y_att)]
    merged = _branch_merge(ys, w_br_l.astype(BF16), proj2, d_model, mix_w, gate_col0)
    return _matmul(merged, w_o_l.astype(BF16), F32, *TILE_OUT_PROJ, res=x2, name="out_proj")


def kernel(x, positions, rel_bias, norm_mix, norm_ffn, w_in, w_br, w_o, ret_gn_w, ret_gn_b, conv_w,
           q_norm, k_norm, ffn_w1, ffn_w3, ffn_w2, moe_router, moe_w1, moe_w3, moe_w2):
    b, s, d_model = x.shape
    depth = w_in.shape[0]
    x2 = x.reshape(b * s, d_model)
    w_in_bf = w_in.astype(BF16)
    rope = _rope_tables(positions.reshape(b, s, 1), (d_model // 2) // RET_HEADS)
    for l in range(depth):
        x2 = _mixer(x2, b, s, positions, rope, rel_bias, norm_mix[l], w_in_bf, l, w_br[l], w_o[l],
                    ret_gn_w[l], ret_gn_b[l], conv_w[l], q_norm[l], k_norm[l])
        if l % 2 == 0:
            i = l // 2
            u = _glu(x2, norm_ffn[l], ffn_w1[i].astype(BF16), ffn_w3[i].astype(BF16))
            x2 = _matmul(u, ffn_w2[i].astype(BF16), F32, *TILE_FFN_DOWN, res=x2, name="ffn_down")
        else:
            i = l // 2
            x2 = _moe(x2, norm_ffn[l], moe_router[i], moe_w1[i], moe_w3[i], moe_w2[i].astype(BF16))
    return x2.reshape(b, s, d_model)
```

```python
import functools
import math

import numpy as np
import jax
import jax.numpy as jnp
from jax import lax
from jax.experimental import pallas as pl
from jax.experimental.pallas import tpu as pltpu

F32 = jnp.float32
BF16 = jnp.bfloat16
I32 = jnp.int32

EPS = 1e-6
N_BRANCH = 3
RET_HEADS = 8
RET_CHUNK = 128
RET_ROPE_BASE = 10000.0
CONV_K = 3
ATT_HEADS = 8
IDX_HEADS = 8
IDX_DH = 64
TOPK_MAX = 256
REL_BUCKETS = 32
REL_MAX_DIST = 128
TOP_K = 2

LANES = 128
SUBLANES = 8
VMEM_LIMIT_BYTES = 56 * 1024 * 1024
ROW_DMA_UNROLL = 8

TM_ROWWISE = 512
TILE_IN_PROJ = (2048, 1024)
TILE_OUT_PROJ = (2048, 512)
TILE_FFN_DOWN = (1024, 512)
TILE_FFN_UP = (1024, 512)
TILE_MERGE = (1024, 512)
TQ_SELECT, CK_SELECT = 256, 512
T_ATTN = 512
TM_MOE, TN_MOE = 512, 512
ROWS_GATHER = 256

NEG_MASK = -1e30
LOG2E = math.log2(math.e)
INT_MIN = -2 ** 31
KEY_BITS = 32


def _cparams(sem):
    return pltpu.CompilerParams(dimension_semantics=sem, vmem_limit_bytes=VMEM_LIMIT_BYTES)


def _sigmoid(x):
    return 1.0 / (1.0 + jnp.exp(-x))


def _tile(n, pref):
    t = min(n, pref)
    assert n % t == 0, (n, pref)
    return t


def _rmsnorm_kernel(x_ref, g_ref, o_ref):
    x = x_ref[...]
    ms = jnp.mean(x * x, axis=-1, keepdims=True)
    o_ref[...] = (x * lax.rsqrt(ms + EPS) * g_ref[...]).astype(o_ref.dtype)


def _rmsnorm(x2, g):
    t, d = x2.shape
    tm = _tile(t, TM_ROWWISE)
    return pl.pallas_call(
        _rmsnorm_kernel,
        out_shape=jax.ShapeDtypeStruct((t, d), BF16),
        grid=(t // tm,),
        in_specs=[pl.BlockSpec((tm, d), lambda i: (i, 0)),
                  pl.BlockSpec((1, d), lambda i: (0, 0))],
        out_specs=pl.BlockSpec((tm, d), lambda i: (i, 0)),
        compiler_params=_cparams(("parallel",)),
        name="rmsnorm",
    )(x2, g.reshape(1, d))


def _mm_kernel(a_ref, b_ref, o_ref):
    o_ref[...] = jnp.dot(a_ref[...], b_ref[...], preferred_element_type=F32).astype(o_ref.dtype)


def _mm_res_kernel(a_ref, b_ref, r_ref, o_ref):
    o_ref[...] = (r_ref[...] + jnp.dot(a_ref[...], b_ref[...], preferred_element_type=F32)).astype(o_ref.dtype)


def _matmul(a, b, out_dtype, tm, tn, res=None, name="matmul"):
    m, k = a.shape
    k2, n = b.shape
    assert k == k2
    tm = _tile(m, tm)
    tn = _tile(n, tn)
    in_specs = [pl.BlockSpec((tm, k), lambda i, j: (i, 0)),
                pl.BlockSpec((k, tn), lambda i, j: (0, j))]
    args = [a, b]
    kern = _mm_kernel
    if res is not None:
        in_specs.append(pl.BlockSpec((tm, tn), lambda i, j: (i, j)))
        args.append(res)
        kern = _mm_res_kernel
    return pl.pallas_call(
        kern,
        out_shape=jax.ShapeDtypeStruct((m, n), out_dtype),
        grid=(m // tm, n // tn),
        in_specs=in_specs,
        out_specs=pl.BlockSpec((tm, tn), lambda i, j: (i, j)),
        compiler_params=_cparams(("parallel", "parallel")),
        name=name,
    )(*args)


def _rope_kernel(pos_ref, invf_ref, cos_ref, sin_ref):
    ang = pos_ref[0].astype(F32) * invf_ref[...]
    sin = jnp.sin(ang)
    dh = ang.shape[1]
    lane = lax.broadcasted_iota(I32, ang.shape, 1)
    cos_ref[0] = jnp.cos(ang)
    sin_ref[0] = jnp.where(lane < dh // 2, -sin, sin)


def _rope_tables(pos3, dh):
    b, s, _ = pos3.shape
    ts = _tile(s, TM_ROWWISE)
    inv_freq = RET_ROPE_BASE ** (-jnp.arange(0, dh, 2, dtype=F32) / dh)
    invf = jnp.concatenate([inv_freq, inv_freq]).reshape(1, dh)
    out = jax.ShapeDtypeStruct((b, s, dh), F32)
    blk = pl.BlockSpec((1, ts, dh), lambda bi, n: (bi, n, 0))
    return pl.pallas_call(
        _rope_kernel,
        out_shape=(out, out),
        grid=(b, s // ts),
        in_specs=[pl.BlockSpec((1, ts, 1), lambda bi, n: (bi, n, 0)),
                  pl.BlockSpec((1, dh), lambda bi, n: (0, 0))],
        out_specs=(blk, blk),
        compiler_params=_cparams(("parallel", "parallel")),
        name="rope_tables",
    )(pos3, invf)


def _ret_kernel(cos_ref, sin_ref, q_ref, k_ref, v_ref, g_ref, dec_ref, kdec_ref, qdec_ref,
                gnw_ref, gnb_ref, o_ref, state_ref, *, heads, dh, chunk_decay):
    @pl.when(pl.program_id(1) == 0)
    def _():
        state_ref[...] = jnp.zeros_like(state_ref)

    cos = cos_ref[0]
    sin_signed = sin_ref[0]
    scale = dh ** -0.5
    for h in range(heads):
        sl = slice(h * dh, (h + 1) * dh)
        q = q_ref[0, :, sl].astype(F32)
        k = k_ref[0, :, sl].astype(F32)
        v = v_ref[0, :, sl]
        qr = (q * cos + pltpu.roll(q, dh // 2, 1) * sin_signed).astype(BF16)
        kr = (k * cos + pltpu.roll(k, dh // 2, 1) * sin_signed) * scale
        scores = lax.dot_general(qr, kr.astype(BF16), (((1,), (1,)), ((), ())),
                                 preferred_element_type=F32) * dec_ref[h]
        intra = jnp.dot(scores.astype(BF16), v, preferred_element_type=F32)
        prev = state_ref[h]
        cross = jnp.dot(qr, prev.astype(BF16), preferred_element_type=F32) * qdec_ref[h]
        kd = (kr * kdec_ref[h]).astype(BF16)
        kv = lax.dot_general(kd, v, (((0,), (0,)), ((), ())), preferred_element_type=F32)
        state_ref[h] = prev * chunk_decay[h] + kv
        ret = intra + cross
        mu = jnp.mean(ret, axis=-1, keepdims=True)
        cen = ret - mu
        var = jnp.mean(cen * cen, axis=-1, keepdims=True)
        gn = cen * lax.rsqrt(var + EPS) * gnw_ref[:, sl] + gnb_ref[:, sl]
        g = g_ref[0, :, sl].astype(F32)
        o_ref[0, :, sl] = (g * _sigmoid(g) * gn).astype(o_ref.dtype)


def _retention_branch(proj3, rope, gn_w, gn_b, mix_w):
    b, s, _ = proj3.shape
    rope_cos, rope_sin = rope
    heads, c = RET_HEADS, RET_CHUNK
    dh = mix_w // heads
    assert dh == LANES and s % c == 0
    log_g = np.log1p(-np.exp2(-5.0 - np.arange(heads, dtype=np.float64)))
    pos = np.arange(c, dtype=np.float64)
    diff = pos[:, None] - pos[None, :]
    intra_decay = np.where(diff >= 0, np.exp(np.maximum(diff, 0.0)[None] * log_g[:, None, None]), 0.0)
    k_decay = np.exp((c - 1 - pos)[None, :] * log_g[:, None])
    q_decay = np.exp((pos + 1)[None, :] * log_g[:, None])
    chunk_decay = tuple(float(x) for x in np.exp(c * log_g))
    dec = jnp.asarray(intra_decay, F32)
    kdec = jnp.asarray(np.broadcast_to(k_decay[:, :, None], (heads, c, dh)), F32)
    qdec = jnp.asarray(np.broadcast_to(q_decay[:, :, None], (heads, c, dh)), F32)

    def col(cb):
        return pl.BlockSpec((1, c, mix_w), lambda bi, n, cb=cb: (bi, n, cb))

    const3 = pl.BlockSpec((heads, c, dh), lambda bi, n: (0, 0, 0))
    return pl.pallas_call(
        functools.partial(_ret_kernel, heads=heads, dh=dh, chunk_decay=chunk_decay),
        out_shape=jax.ShapeDtypeStruct((b, s, mix_w), BF16),
        grid=(b, s // c),
        in_specs=[pl.BlockSpec((1, c, dh), lambda bi, n: (bi, n, 0)),
                  pl.BlockSpec((1, c, dh), lambda bi, n: (bi, n, 0)),
                  col(0), col(1), col(2), col(3),
                  const3, const3, const3,
                  pl.BlockSpec((1, mix_w), lambda bi, n: (0, 0)),
                  pl.BlockSpec((1, mix_w), lambda bi, n: (0, 0))],
        out_specs=pl.BlockSpec((1, c, mix_w), lambda bi, n: (bi, n, 0)),
        scratch_shapes=[pltpu.VMEM((heads, dh, dh), F32)],
        compiler_params=_cparams(("arbitrary", "arbitrary")),
        name="retention",
    )(rope_cos, rope_sin, proj3, proj3, proj3, proj3, dec, kdec, qdec,
      gn_w.reshape(1, mix_w), gn_b.reshape(1, mix_w))


def _conv_kernel(b_ref, c_ref, u_ref, w_ref, o_ref, ubuf_ref):
    ts = o_ref.shape[1]
    halo = SUBLANES

    @pl.when(pl.program_id(1) == 0)
    def _():
        ubuf_ref[0:halo, :] = jnp.zeros((halo, ubuf_ref.shape[1]), F32)

    u = c_ref[0].astype(F32) * u_ref[0].astype(F32)
    ubuf_ref[halo:halo + ts, :] = u
    u1 = ubuf_ref[halo - 1:halo - 1 + ts, :]
    u2 = ubuf_ref[halo - 2:halo - 2 + ts, :]
    conv = w_ref[0:1, :] * u2 + w_ref[1:2, :] * u1 + w_ref[2:3, :] * u
    o_ref[0] = (b_ref[0].astype(F32) * conv).astype(o_ref.dtype)
    ubuf_ref[0:halo, :] = ubuf_ref[ts:ts + halo, :]


def _conv_branch(proj3, conv_w, mix_w):
    b, s, _ = proj3.shape
    ts = _tile(s, TM_ROWWISE)

    def col(cb):
        return pl.BlockSpec((1, ts, mix_w), lambda bi, n, cb=cb: (bi, n, cb))

    return pl.pallas_call(
        _conv_kernel,
        out_shape=jax.ShapeDtypeStruct((b, s, mix_w), BF16),
        grid=(b, s // ts),
        in_specs=[col(4), col(5), col(6),
                  pl.BlockSpec((CONV_K, mix_w), lambda bi, n: (0, 0))],
        out_specs=pl.BlockSpec((1, ts, mix_w), lambda bi, n: (bi, n, 0)),
        scratch_shapes=[pltpu.VMEM((ts + SUBLANES, mix_w), F32)],
        compiler_params=_cparams(("arbitrary", "arbitrary")),
        name="short_conv",
    )(proj3, proj3, proj3, conv_w)


def _qknorm_kernel(q_ref, k_ref, qn_ref, kn_ref, qo_ref, ko_ref, *, heads, dh, q_scale):
    for h in range(heads):
        sl = slice(h * dh, (h + 1) * dh)
        q = q_ref[0, :, sl].astype(F32)
        k = k_ref[0, :, sl].astype(F32)
        qy = q * lax.rsqrt(jnp.mean(q * q, axis=-1, keepdims=True) + EPS) * qn_ref[...]
        ky = k * lax.rsqrt(jnp.mean(k * k, axis=-1, keepdims=True) + EPS) * kn_ref[...]
        qo_ref[0, :, sl] = (qy * q_scale).astype(qo_ref.dtype)
        ko_ref[0, :, sl] = ky.astype(ko_ref.dtype)


def _qk_norm(proj3, q_norm, k_norm, mix_w):
    b, s, _ = proj3.shape
    heads = ATT_HEADS
    dh = mix_w // heads
    ts = _tile(s, TM_ROWWISE)
    out = jax.ShapeDtypeStruct((b, s, mix_w), BF16)
    blk = pl.BlockSpec((1, ts, mix_w), lambda bi, n: (bi, n, 0))
    return pl.pallas_call(
        functools.partial(_qknorm_kernel, heads=heads, dh=dh, q_scale=dh ** -0.5 * LOG2E),
        out_shape=(out, out),
        grid=(b, s // ts),
        in_specs=[pl.BlockSpec((1, ts, mix_w), lambda bi, n: (bi, n, 7)),
                  pl.BlockSpec((1, ts, mix_w), lambda bi, n: (bi, n, 8)),
                  pl.BlockSpec((1, dh), lambda bi, n: (0, 0)),
                  pl.BlockSpec((1, dh), lambda bi, n: (0, 0))],
        out_specs=(blk, blk),
        compiler_params=_cparams(("parallel", "parallel")),
        name="qk_norm",
    )(proj3, proj3, q_norm.reshape(1, dh), k_norm.reshape(1, dh))


def _select_kernel(iq_ref, sm_ref, kit_ref, tri_ref, o_ref, qall_ref, keys_ref, planes_ref, eq_ref, *,
                   tq, ck, ksel, heads, d_idx):
    qb = pl.program_id(1)
    row0 = qb * tq
    nch = (row0 + tq + ck - 1) // ck
    grp = KEY_BITS * LANES
    ngrp = keys_ref.shape[1] // grp
    nlive = (row0 + tq + grp - 1) // grp
    for h in range(heads):
        qall_ref[h * tq:(h + 1) * tq, :] = iq_ref[0, :, h * d_idx:(h + 1) * d_idx]
    w = sm_ref[0, :, d_idx:d_idx + heads].astype(F32) * (heads ** -0.5 * d_idx ** -0.5)
    t_col = row0 + lax.broadcasted_iota(I32, (tq, 1), 0)
    lane_ck = lax.broadcasted_iota(I32, (1, ck), 1)

    def score_chunk(c, carry):
        off = pl.multiple_of(c * ck, ck)
        kt = kit_ref[0, :, pl.ds(off, ck)]
        sc = jnp.zeros((tq, ck), F32)
        for h in range(heads):
            d = jnp.dot(qall_ref[h * tq:(h + 1) * tq, :], kt, preferred_element_type=F32)
            sc = sc + w[:, h:h + 1] * jnp.maximum(d, 0.0)
        bits = pltpu.bitcast(sc, I32)
        key = bits ^ ((bits >> 31) & 0x7FFFFFFF)
        keys_ref[:, pl.ds(off, ck)] = jnp.where(off + lane_ck <= t_col, key, INT_MIN)
        return carry

    def clear_chunk(c, carry):
        keys_ref[:, pl.ds(pl.multiple_of(c * ck, ck), ck)] = jnp.full((tq, ck), INT_MIN, I32)
        return carry

    lax.fori_loop(0, nch, score_chunk, 0)
    lax.fori_loop(nch, nlive * (grp // ck), clear_chunk, 0)

    def transpose_group(g):
        def rows(r, carry):
            rs = pl.ds(pl.multiple_of(r * SUBLANES, SUBLANES), SUBLANES)
            a = [keys_ref[rs, (g * KEY_BITS + j) * LANES:(g * KEY_BITS + j + 1) * LANES] ^ INT_MIN
                 for j in range(KEY_BITS)]
            j, m = KEY_BITS // 2, (1 << (KEY_BITS // 2)) - 1
            while j:
                k = 0
                while k < KEY_BITS:
                    t = (a[k] ^ (a[k + j] >> j)) & m
                    a[k] = a[k] ^ t
                    a[k + j] = a[k + j] ^ (t << j)
                    k = (k + j + 1) & ~j
                j >>= 1
                m = m ^ (m << j)
            for p in range(KEY_BITS):
                planes_ref[p, g, rs, :] = a[p]
            return carry

        lax.fori_loop(0, tq // SUBLANES, rows, 0)

    @pl.when(qb == 0)
    def _():
        planes_ref[...] = jnp.zeros_like(planes_ref)

    for g in range(ngrp):
        @pl.when(g < nlive)
        def _(g=g):
            transpose_group(g)
            eq_ref[g] = jnp.full(eq_ref.shape[1:], -1, I32)

        @pl.when(g >= nlive)
        def _(g=g):
            eq_ref[g] = jnp.zeros(eq_ref.shape[1:], I32)

    kf = float(ksel)

    def bit_pass(p, carry):
        above, thr_u = carry
        ones = jnp.zeros((tq, LANES), I32)
        for g in range(ngrp):
            ones = ones + lax.population_count(eq_ref[g] & planes_ref[p, g])
        c1 = jnp.sum(ones.astype(F32), axis=1, keepdims=True)
        take = (above + c1) >= kf
        for g in range(ngrp):
            e = eq_ref[g]
            x = e & planes_ref[p, g]
            eq_ref[g] = jnp.where(take, x, e ^ x)
        bit = jnp.left_shift(jnp.int32(1), KEY_BITS - 1 - p)
        return jnp.where(take, above, above + c1), jnp.where(take, thr_u | bit, thr_u)

    cnt_gt, thr_u = lax.fori_loop(0, KEY_BITS, bit_pass,
                                  (jnp.zeros((tq, 1), F32), jnp.zeros((tq, 1), I32)))
    thr = thr_u ^ INT_MIN
    eq_cnt = jnp.zeros((tq, LANES), I32)
    for g in range(ngrp):
        eq_cnt = eq_cnt + lax.population_count(eq_ref[g])
    cnt_eq = jnp.sum(eq_cnt.astype(F32), axis=1, keepdims=True)

    need = kf - cnt_gt
    excess = jnp.where(thr == INT_MIN, 0.0, cnt_eq - need)
    o_ref[...] = jnp.full(o_ref.shape, NEG_MASK, o_ref.dtype)

    def write_chunk(c, eq_before, ties):
        off = pl.multiple_of(c * ck, ck)
        sl = pl.ds(off, ck)
        k = keys_ref[:, sl]
        if ties:
            eq = jnp.where(k == thr, 1.0, 0.0).astype(BF16)
            rank = eq_before + jnp.dot(eq, tri_ref[...], preferred_element_type=F32)
            on_eq = jnp.where(rank <= need, 0.0, NEG_MASK)
            eq_before = rank[:, ck - 1:ck]
        else:
            on_eq = 0.0
        val = jnp.where(k > thr, 0.0, jnp.where(k == thr, on_eq, NEG_MASK))
        o_ref[0, :, sl] = jnp.where(off + lane_ck <= t_col, val, NEG_MASK).astype(o_ref.dtype)
        return eq_before

    any_excess = jnp.max(excess) > 0.0

    @pl.when(any_excess)
    def _():
        lax.fori_loop(0, nch, lambda c, e: write_chunk(c, e, True), jnp.zeros((tq, 1), F32))

    @pl.when(jnp.logical_not(any_excess))
    def _():
        lax.fori_loop(0, nch, lambda c, e: write_chunk(c, e, False), 0)


def _dsa_select(proj3, kit, s, ksel, iq_block):
    b = proj3.shape[0]
    tq = _tile(s, TQ_SELECT)
    ck = _tile(s, CK_SELECT)
    heads, d_idx = IDX_HEADS, IDX_DH
    iq_w = heads * d_idx
    assert (s % (KEY_BITS * LANES) == 0 or s < KEY_BITS * LANES) and (KEY_BITS * LANES) % ck == 0
    sp = max(s, KEY_BITS * LANES)
    tri = (jnp.arange(ck)[:, None] <= jnp.arange(ck)[None, :]).astype(BF16)
    return pl.pallas_call(
        functools.partial(_select_kernel, tq=tq, ck=ck, ksel=ksel, heads=heads, d_idx=d_idx),
        out_shape=jax.ShapeDtypeStruct((b, s, s), BF16),
        grid=(b, s // tq),
        in_specs=[pl.BlockSpec((1, tq, iq_w), lambda bi, n: (bi, n, iq_block)),
                  pl.BlockSpec((1, tq, iq_w), lambda bi, n: (bi, n, iq_block + 1)),
                  pl.BlockSpec((1, d_idx, s), lambda bi, n: (bi, 0, 0)),
                  pl.BlockSpec((ck, ck), lambda bi, n: (0, 0))],
        out_specs=pl.BlockSpec((1, tq, s), lambda bi, n: (bi, n, 0)),
        scratch_shapes=[pltpu.VMEM((heads * tq, d_idx), BF16),
                        pltpu.VMEM((tq, sp), I32),
                        pltpu.VMEM((KEY_BITS, sp // (KEY_BITS * LANES), tq, LANES), I32),
                        pltpu.VMEM((sp // (KEY_BITS * LANES), tq, LANES), I32)],
        compiler_params=_cparams(("arbitrary", "arbitrary")),
        name="dsa_select",
    )(proj3, proj3, kit, tri)


def _t5_bucket(n):
    max_exact = REL_BUCKETS // 2
    nf = jnp.maximum(n, 1).astype(F32)
    large = max_exact + (jnp.log(nf / max_exact) / math.log(REL_MAX_DIST / max_exact)
                         * (REL_BUCKETS - max_exact)).astype(I32)
    large = jnp.minimum(large, REL_BUCKETS - 1)
    return jnp.where(n < max_exact, n, large)


def _attn_kernel(qb_ref, kb_ref, rb_ref, plo_ref, phi_ref, q_ref, k_ref, v_ref, mask_ref, pq_ref, pk_ref,
                 o_ref, acc_ref, m_ref, l_ref, bias_ref, s_ref, p_ref, *, tq, tk, heads, dh):
    qb = qb_ref[pl.program_id(1)]
    kb = kb_ref[pl.program_id(1)]
    kmax = ((qb + 1) * tq - 1) // tk
    nlt = tk // LANES
    gran = LANES

    @pl.when(kb == 0)
    def _():
        acc_ref[...] = jnp.zeros_like(acc_ref)
        m_ref[...] = jnp.full(m_ref.shape, -jnp.inf, F32)
        l_ref[...] = jnp.zeros_like(l_ref)

    def attend(bias_of):
        ones = jnp.ones((tk, LANES), BF16)
        for h in range(heads):
            sl = slice(h * dh, (h + 1) * dh)
            kh = k_ref[0, :, sl]
            v_aug = jnp.concatenate([v_ref[0, :, sl], ones], axis=1)
            add, const = bias_of(h)
            buf = h % 2
            s = lax.dot_general(q_ref[0, :, sl], kh, (((1,), (1,)), ((), ())),
                                preferred_element_type=F32) + add
            s_ref[buf] = s
            tile_max = s[:, 0:LANES]
            for c in range(1, nlt):
                tile_max = jnp.maximum(tile_max, s[:, c * LANES:(c + 1) * LANES])
            m_cur = jnp.broadcast_to(jnp.max(tile_max, axis=1, keepdims=True), (tq, LANES)) + const
            m_old = m_ref[h]
            m_new = jnp.maximum(m_old, m_cur)
            alpha = jnp.exp2(m_old - m_new)
            shift = m_new - const
            for c in range(nlt):
                cs = slice(c * LANES, (c + 1) * LANES)
                p_ref[buf, :, cs] = jnp.exp2(s_ref[buf, :, cs] - shift).astype(BF16)
            pv = jnp.dot(p_ref[buf], v_aug, preferred_element_type=F32)
            m_ref[h] = m_new
            l_ref[h] = alpha * l_ref[h] + pv[:, dh:]
            acc_ref[:, sl] = alpha * acc_ref[:, sl] + pv[:, :dh]

    @pl.when(kb <= kmax)
    def _():
        pq = pq_ref[0]
        pk = pk_ref[0]
        bi = pl.program_id(0)
        q_lo = [plo_ref[bi, qb * (tq // gran) + ri] for ri in range(tq // gran)]
        q_hi = [phi_ref[bi, qb * (tq // gran) + ri] for ri in range(tq // gran)]
        k_lo = [plo_ref[bi, kb * (tk // gran) + cj] for cj in range(tk // gran)]
        k_hi = [phi_ref[bi, kb * (tk // gran) + cj] for cj in range(tk // gran)]
        all_far = (functools.reduce(jnp.minimum, q_lo) - functools.reduce(jnp.maximum, k_hi)) >= REL_MAX_DIST

        @pl.when(all_far)
        def _():
            attend(lambda h: (mask_ref[0].astype(F32), rb_ref[REL_BUCKETS - 1, h] * LOG2E))

        @pl.when(jnp.logical_not(all_far))
        def _():
            n_lane = lax.broadcasted_iota(I32, (1, REL_MAX_DIST), 1)
            bucket = _t5_bucket(n_lane)
            tabs = []
            for h in range(heads):
                tab = jnp.zeros((1, REL_MAX_DIST), F32)
                for j in range(REL_BUCKETS):
                    tab = jnp.where(bucket == j, rb_ref[j, h] * LOG2E, tab)
                tabs.append(jnp.broadcast_to(tab, (gran, REL_MAX_DIST)))
            for ri in range(tq // gran):
                rs = slice(ri * gran, (ri + 1) * gran)
                pq_g = pq[rs]
                for cj in range(tk // gran):
                    cs = slice(cj * gran, (cj + 1) * gran)
                    pk_g = pk[:, cs]
                    lo = q_lo[ri] - k_hi[cj]
                    hi = q_hi[ri] - k_lo[cj]
                    is_far = lo >= REL_MAX_DIST
                    is_zero = hi <= 0
                    maskf = mask_ref[0, rs, cs].astype(F32)

                    @pl.when(is_far | is_zero)
                    def _():
                        for h in range(heads):
                            c_h = jnp.where(is_far, rb_ref[REL_BUCKETS - 1, h], rb_ref[0, h]) * LOG2E
                            bias_ref[h, rs, cs] = maskf + c_h

                    @pl.when(jnp.logical_not(is_far | is_zero))
                    def _():
                        dist = jnp.clip(pq_g - pk_g, 0, REL_MAX_DIST - 1)
                        for h in range(heads):
                            bias_ref[h, rs, cs] = maskf + jnp.take_along_axis(tabs[h], dist, axis=1)
            attend(lambda h: (bias_ref[h], 0.0))

    @pl.when(kb == kmax)
    def _():
        for h in range(heads):
            sl = slice(h * dh, (h + 1) * dh)
            o_ref[0, :, sl] = (acc_ref[:, sl] / l_ref[h]).astype(o_ref.dtype)


def _dsa_attention(qn, kn, proj3, mask, pos_q, pos_k, rel_bias, mix_w):
    b, s, _ = qn.shape
    heads = ATT_HEADS
    dh = mix_w // heads
    tq = _tile(s, T_ATTN)
    tk = tq
    assert REL_MAX_DIST == LANES

    tiles = [(qb, kb) for qb in range(s // tq) for kb in range(((qb + 1) * tq - 1) // tk + 1)]
    qb_of = jnp.asarray([t[0] for t in tiles], I32)
    kb_of = jnp.asarray([t[1] for t in tiles], I32)
    pos_g = pos_k.reshape(b, s // LANES, LANES)
    pos_lo = jnp.min(pos_g, axis=-1)
    pos_hi = jnp.max(pos_g, axis=-1)

    return pl.pallas_call(
        functools.partial(_attn_kernel, tq=tq, tk=tk, heads=heads, dh=dh),
        out_shape=jax.ShapeDtypeStruct((b, s, mix_w), BF16),
        grid_spec=pltpu.PrefetchScalarGridSpec(
            num_scalar_prefetch=2,
            grid=(b, len(tiles)),
            in_specs=[pl.BlockSpec(memory_space=pltpu.SMEM),
                      pl.BlockSpec(memory_space=pltpu.SMEM),
                      pl.BlockSpec(memory_space=pltpu.SMEM),
                      pl.BlockSpec((1, tq, mix_w), lambda bi, st, qo, ko: (bi, qo[st], 0)),
                      pl.BlockSpec((1, tk, mix_w), lambda bi, st, qo, ko: (bi, ko[st], 0)),
                      pl.BlockSpec((1, tk, mix_w), lambda bi, st, qo, ko: (bi, ko[st], 9)),
                      pl.BlockSpec((1, tq, tk), lambda bi, st, qo, ko: (bi, qo[st], ko[st])),
                      pl.BlockSpec((1, tq, 1), lambda bi, st, qo, ko: (bi, qo[st], 0)),
                      pl.BlockSpec((1, 1, tk), lambda bi, st, qo, ko: (bi, 0, ko[st]))],
            out_specs=pl.BlockSpec((1, tq, mix_w), lambda bi, st, qo, ko: (bi, qo[st], 0)),
            scratch_shapes=[pltpu.VMEM((tq, mix_w), F32),
                            pltpu.VMEM((heads, tq, LANES), F32),
                            pltpu.VMEM((heads, tq, LANES), F32),
                            pltpu.VMEM((heads, tq, tk), F32),
                            pltpu.VMEM((2, tq, tk), F32),
                            pltpu.VMEM((2, tq, tk), BF16)]),
        compiler_params=_cparams(("parallel", "arbitrary")),
        name="dsa_attention",
    )(qb_of, kb_of, rel_bias, pos_lo, pos_hi, qn, kn, proj3, mask, pos_q, pos_k)


def _merge_kernel(y0_ref, y1_ref, y2_ref, w_ref, g0_ref, g1_ref, g2_ref, o_ref):
    acc = None
    for n, (y_ref, g_ref) in enumerate(((y0_ref, g0_ref), (y1_ref, g1_ref), (y2_ref, g2_ref))):
        br = jnp.dot(y_ref[...], w_ref[n], preferred_element_type=F32)
        term = _sigmoid(g_ref[...].astype(F32)) * br
        acc = term if acc is None else acc + term
    o_ref[...] = acc.astype(o_ref.dtype)


def _branch_merge(ys, w_br, proj2, d_model, mix_w, gate_col0):
    t = proj2.shape[0]
    tm = _tile(t, TILE_MERGE[0])
    tn = _tile(d_model, TILE_MERGE[1])
    yspec = pl.BlockSpec((tm, mix_w), lambda i, j: (i, 0))

    def gspec(n):
        base = (gate_col0 + n * d_model) // tn
        return pl.BlockSpec((tm, tn), lambda i, j, base=base: (i, base + j))

    return pl.pallas_call(
        _merge_kernel,
        out_shape=jax.ShapeDtypeStruct((t, d_model), BF16),
        grid=(t // tm, d_model // tn),
        in_specs=[yspec, yspec, yspec,
                  pl.BlockSpec((N_BRANCH, mix_w, tn), lambda i, j: (0, 0, j)),
                  gspec(0), gspec(1), gspec(2)],
        out_specs=pl.BlockSpec((tm, tn), lambda i, j: (i, j)),
        compiler_params=_cparams(("parallel", "parallel")),
        name="branch_merge",
    )(ys[0], ys[1], ys[2], w_br, proj2, proj2, proj2)


def _glu_kernel(x_ref, g_ref, w1_ref, w3_ref, o_ref, a_ref):
    @pl.when(pl.program_id(1) == 0)
    def _():
        x = x_ref[...]
        ms = jnp.mean(x * x, axis=-1, keepdims=True)
        a_ref[...] = (x * lax.rsqrt(ms + EPS) * g_ref[...]).astype(a_ref.dtype)

    a = a_ref[...]
    h1 = jnp.dot(a, w1_ref[...], preferred_element_type=F32)
    h3 = jnp.dot(a, w3_ref[...], preferred_element_type=F32)
    o_ref[...] = (h1 * _sigmoid(h1) * h3).astype(o_ref.dtype)


def _glu(x2, g, w1, w3):
    t, d = x2.shape
    ff = w1.shape[1]
    tm = _tile(t, TILE_FFN_UP[0])
    tn = _tile(ff, TILE_FFN_UP[1])
    return pl.pallas_call(
        _glu_kernel,
        out_shape=jax.ShapeDtypeStruct((t, ff), BF16),
        grid=(t // tm, ff // tn),
        in_specs=[pl.BlockSpec((tm, d), lambda i, j: (i, 0)),
                  pl.BlockSpec((1, d), lambda i, j: (0, 0)),
                  pl.BlockSpec((d, tn), lambda i, j: (0, j)),
                  pl.BlockSpec((d, tn), lambda i, j: (0, j))],
        out_specs=pl.BlockSpec((tm, tn), lambda i, j: (i, j)),
        scratch_shapes=[pltpu.VMEM((tm, d), BF16)],
        compiler_params=_cparams(("parallel", "arbitrary")),
        name="swiglu_up",
    )(x2, g.reshape(1, d), w1, w3)


def _router_kernel(x_ref, g_ref, r_ref, comb_ref, sel_ref, *, n_exp):
    x = x_ref[...]
    h = (x * lax.rsqrt(jnp.mean(x * x, axis=-1, keepdims=True) + EPS) * g_ref[...]).astype(BF16)
    logits = jnp.dot(h, r_ref[...], preferred_element_type=F32)
    lane = lax.broadcasted_iota(I32, logits.shape, 1)
    lg = jnp.where(lane < n_exp, logits, -jnp.inf)
    v1 = jnp.max(lg, axis=1, keepdims=True)
    i1 = jnp.min(jnp.where(lg == v1, lane, LANES), axis=1, keepdims=True)
    lg2 = jnp.where(lane == i1, -jnp.inf, lg)
    v2 = jnp.max(lg2, axis=1, keepdims=True)
    i2 = jnp.min(jnp.where(lg2 == v2, lane, LANES), axis=1, keepdims=True)
    e = jnp.exp(v2 - v1)
    w1 = 1.0 / (1.0 + e)
    w2 = e / (1.0 + e)
    comb_ref[...] = jnp.where(lane == i1, w1, jnp.where(lane == i2, w2, 0.0))
    sel_ref[...] = jnp.where((lane == i1) | (lane == i2), 1, 0)


def _router(x2, g, router_w):
    t, d = x2.shape
    n_exp = router_w.shape[1]
    tm = _tile(t, TM_ROWWISE)
    rpad = jnp.zeros((d, LANES), BF16).at[:, :n_exp].set(router_w.astype(BF16))
    comb, sel = pl.pallas_call(
        functools.partial(_router_kernel, n_exp=n_exp),
        out_shape=(jax.ShapeDtypeStruct((t, LANES), F32), jax.ShapeDtypeStruct((t, LANES), I32)),
        grid=(t // tm,),
        in_specs=[pl.BlockSpec((tm, d), lambda i: (i, 0)),
                  pl.BlockSpec((1, d), lambda i: (0, 0)),
                  pl.BlockSpec((d, LANES), lambda i: (0, 0))],
        out_specs=(pl.BlockSpec((tm, LANES), lambda i: (i, 0)),
                   pl.BlockSpec((tm, LANES), lambda i: (i, 0))),
        compiler_params=_cparams(("parallel",)),
        name="moe_router",
    )(x2, g.reshape(1, d), rpad)
    return comb[:, :n_exp], sel[:, :n_exp]


def _row_gather(idx_ref, src_hbm, buf_ref, sem, slot, rows, start):
    if not start:
        pltpu.make_async_copy(src_hbm.at[pl.ds(0, rows)], buf_ref.at[slot], sem.at[slot]).wait()
        return

    def body(r8, carry):
        for u in range(ROW_DMA_UNROLL):
            r = r8 * ROW_DMA_UNROLL + u
            cp = pltpu.make_async_copy(src_hbm.at[pl.ds(idx_ref[0, 0, r], 1)],
                                       buf_ref.at[slot, pl.ds(r, 1)], sem.at[slot])
            if start:
                cp.start(priority=u % 2)
            else:
                cp.wait()
        return carry

    lax.fori_loop(0, rows // ROW_DMA_UNROLL, body, 0)


def _dispatch_kernel(cur_ref, nxt_ref, x_hbm, g_ref, o_ref, buf_ref, sem, *, rows, nsteps):
    i = pl.program_id(0)
    slot = i % 2

    @pl.when(i == 0)
    def _():
        _row_gather(cur_ref, x_hbm, buf_ref, sem, 0, rows, start=True)

    @pl.when(i + 1 < nsteps)
    def _():
        _row_gather(nxt_ref, x_hbm, buf_ref, sem, 1 - slot, rows, start=True)

    _row_gather(cur_ref, x_hbm, buf_ref, sem, slot, rows, start=False)
    x = buf_ref[slot]
    o_ref[...] = (x * lax.rsqrt(jnp.mean(x * x, axis=-1, keepdims=True) + EPS) * g_ref[...]).astype(o_ref.dtype)


def _dispatch(x2, g, src_tok, rows):
    t, d = x2.shape
    p = src_tok.shape[0]
    assert p % rows == 0 and rows % ROW_DMA_UNROLL == 0
    nsteps = p // rows
    src3 = src_tok.reshape(nsteps, 1, rows)
    return pl.pallas_call(
        functools.partial(_dispatch_kernel, rows=rows, nsteps=nsteps),
        out_shape=jax.ShapeDtypeStruct((p, d), BF16),
        grid=(nsteps,),
        in_specs=[pl.BlockSpec((1, 1, rows), lambda i: (i, 0, 0), memory_space=pltpu.SMEM),
                  pl.BlockSpec((1, 1, rows), lambda i: (jnp.minimum(i + 1, nsteps - 1), 0, 0),
                               memory_space=pltpu.SMEM),
                  pl.BlockSpec(memory_space=pl.ANY),
                  pl.BlockSpec((1, d), lambda i: (0, 0))],
        out_specs=pl.BlockSpec((rows, d), lambda i: (i, 0)),
        scratch_shapes=[pltpu.VMEM((2, rows, d), F32), pltpu.SemaphoreType.DMA((2,))],
        compiler_params=_cparams(("arbitrary",)),
        name="moe_dispatch",
    )(src3, src3, x2, g.reshape(1, d))


def _moe_up_kernel(ti_ref, tj_ref, wj_ref, te_ref, first_ref, valid_ref, a_ref, w1_ref, w3_ref, o_ref,
                   w1b_ref, w3b_ref):
    s = pl.program_id(0)

    @pl.when(first_ref[s] == 1)
    def _():
        w1b_ref[...] = w1_ref[0].astype(BF16)
        w3b_ref[...] = w3_ref[0].astype(BF16)

    @pl.when(valid_ref[s] == 1)
    def _():
        a = a_ref[...]
        h1 = jnp.dot(a, w1b_ref[...], preferred_element_type=F32)
        h3 = jnp.dot(a, w3b_ref[...], preferred_element_type=F32)
        o_ref[...] = (h1 * _sigmoid(h1) * h3).astype(o_ref.dtype)

    @pl.when(valid_ref[s] == 0)
    def _():
        o_ref[...] = jnp.zeros_like(o_ref)


def _moe_up(hs, w1, w3, tile_expert, tile_valid, gfirst, gtiles, tm):
    p, d = hs.shape
    ff = w1.shape[2]
    tn = _tile(ff, TN_MOE)
    nj = ff // tn
    n_tiles = p // tm
    step = jnp.arange(n_tiles * nj, dtype=I32)
    t0 = step // nj
    e_s = tile_expert[t0]
    valid = tile_valid[t0]
    n_run = jnp.maximum(gtiles[e_s], 1)
    local = step - gfirst[e_s] * nj
    ti = jnp.where(valid == 1, gfirst[e_s] + local % n_run, t0).astype(I32)
    tj = jnp.where(valid == 1, local // n_run, step % nj).astype(I32)
    wj = jnp.where(valid == 1, local // n_run, nj - 1).astype(I32)
    first = jnp.where((valid == 1) & (local % n_run == 0), 1, 0).astype(I32)

    def amap(s, ti_r, tj_r, wj_r, te_r, f_r, v_r):
        return (ti_r[s], 0)

    def wmap(s, ti_r, tj_r, wj_r, te_r, f_r, v_r):
        return (te_r[s], 0, wj_r[s])

    def omap(s, ti_r, tj_r, wj_r, te_r, f_r, v_r):
        return (ti_r[s], tj_r[s])

    return pl.pallas_call(
        _moe_up_kernel,
        out_shape=jax.ShapeDtypeStruct((p, ff), BF16),
        grid_spec=pltpu.PrefetchScalarGridSpec(
            num_scalar_prefetch=6,
            grid=(n_tiles * nj,),
            in_specs=[pl.BlockSpec((tm, d), amap),
                      pl.BlockSpec((1, d, tn), wmap),
                      pl.BlockSpec((1, d, tn), wmap)],
            out_specs=pl.BlockSpec((tm, tn), omap),
            scratch_shapes=[pltpu.VMEM((d, tn), BF16), pltpu.VMEM((d, tn), BF16)]),
        compiler_params=_cparams(("arbitrary",)),
        name="moe_up",
    )(ti, tj, wj, e_s.astype(I32), first, valid.astype(I32), hs, w1, w3)


def _moe_down_kernel(te_ref, tv_ref, a_ref, w_ref, o_ref):
    i = pl.program_id(0)

    @pl.when(tv_ref[i] == 1)
    def _():
        o_ref[...] = jnp.dot(a_ref[...], w_ref[0], preferred_element_type=F32)

    @pl.when(tv_ref[i] == 0)
    def _():
        o_ref[...] = jnp.zeros_like(o_ref)


def _moe_down(us, w2, tile_expert, tile_valid, tm):
    p, ff = us.shape
    d = w2.shape[2]
    tn = _tile(d, TN_MOE)
    nj = d // tn

    def wmap(i, j, te, tv):
        return (te[i], 0, jnp.where(tv[i] == 1, j, nj - 1))

    return pl.pallas_call(
        _moe_down_kernel,
        out_shape=jax.ShapeDtypeStruct((p, d), F32),
        grid_spec=pltpu.PrefetchScalarGridSpec(
            num_scalar_prefetch=2,
            grid=(p // tm, nj),
            in_specs=[pl.BlockSpec((tm, ff), lambda i, j, te, tv: (i, 0)),
                      pl.BlockSpec((1, ff, tn), wmap)],
            out_specs=pl.BlockSpec((tm, tn), lambda i, j, te, tv: (i, j))),
        compiler_params=_cparams(("arbitrary", "arbitrary")),
        name="moe_down",
    )(tile_expert, tile_valid, us, w2)


def _combine_kernel(p0_ref, p1_ref, n0_ref, n1_ref, x_ref, w0_ref, w1_ref, ys_hbm, o_ref,
                    a_ref, b_ref, sem_a, sem_b, *, rows, nsteps):
    i = pl.program_id(0)
    slot = i % 2

    @pl.when(i == 0)
    def _():
        _row_gather(p0_ref, ys_hbm, a_ref, sem_a, 0, rows, start=True)
        _row_gather(p1_ref, ys_hbm, b_ref, sem_b, 0, rows, start=True)

    @pl.when(i + 1 < nsteps)
    def _():
        _row_gather(n0_ref, ys_hbm, a_ref, sem_a, 1 - slot, rows, start=True)
        _row_gather(n1_ref, ys_hbm, b_ref, sem_b, 1 - slot, rows, start=True)

    _row_gather(p0_ref, ys_hbm, a_ref, sem_a, slot, rows, start=False)
    _row_gather(p1_ref, ys_hbm, b_ref, sem_b, slot, rows, start=False)
    o_ref[...] = x_ref[...] + (w0_ref[...] * a_ref[slot] + w1_ref[...] * b_ref[slot])


def _combine(x2, ys, pos0, pos1, w0, w1, rows):
    t, d = x2.shape
    assert t % rows == 0 and rows % ROW_DMA_UNROLL == 0
    nsteps = t // rows
    cur = pl.BlockSpec((1, 1, rows), lambda i: (i, 0, 0), memory_space=pltpu.SMEM)
    nxt = pl.BlockSpec((1, 1, rows), lambda i: (jnp.minimum(i + 1, nsteps - 1), 0, 0),
                       memory_space=pltpu.SMEM)
    p0 = pos0.reshape(nsteps, 1, rows)
    p1 = pos1.reshape(nsteps, 1, rows)
    return pl.pallas_call(
        functools.partial(_combine_kernel, rows=rows, nsteps=nsteps),
        out_shape=jax.ShapeDtypeStruct((t, d), F32),
        grid=(nsteps,),
        in_specs=[cur, cur, nxt, nxt,
                  pl.BlockSpec((rows, d), lambda i: (i, 0)),
                  pl.BlockSpec((rows, 1), lambda i: (i, 0)),
                  pl.BlockSpec((rows, 1), lambda i: (i, 0)),
                  pl.BlockSpec(memory_space=pl.ANY)],
        out_specs=pl.BlockSpec((rows, d), lambda i: (i, 0)),
        scratch_shapes=[pltpu.VMEM((2, rows, d), F32), pltpu.VMEM((2, rows, d), F32),
                        pltpu.SemaphoreType.DMA((2,)), pltpu.SemaphoreType.DMA((2,))],
        compiler_params=_cparams(("arbitrary",)),
        name="moe_combine",
    )(p0, p1, p0, p1, x2, w0, w1, ys)


def _moe(x2, g, router_w, w1, w3, w2):
    t, d = x2.shape
    n_exp = router_w.shape[1]
    tm = _tile(t, TM_MOE)
    comb, sel = _router(x2, g, router_w)
    rank = jnp.cumsum(sel, axis=0) - sel
    counts = jnp.sum(sel, axis=0)
    gsize = ((counts + tm - 1) // tm) * tm
    gend = jnp.cumsum(gsize)
    goff = gend - gsize
    dest = goff[None, :] + rank
    e0 = jnp.argmax(sel, axis=1)
    e1 = jnp.max(sel * jnp.arange(n_exp, dtype=I32)[None, :], axis=1)
    pos0 = jnp.take_along_axis(dest, e0[:, None], axis=1)[:, 0].astype(I32)
    pos1 = jnp.take_along_axis(dest, e1[:, None], axis=1)[:, 0].astype(I32)
    wt0 = jnp.take_along_axis(comb, e0[:, None], axis=1)
    wt1 = jnp.take_along_axis(comb, e1[:, None], axis=1)
    p = t * TOP_K + n_exp * tm
    tok = jnp.arange(t, dtype=I32)
    src_tok = jnp.zeros((p,), I32).at[jnp.concatenate([pos0, pos1])].set(jnp.concatenate([tok, tok]))
    n_tiles = p // tm
    tstart = jnp.arange(n_tiles, dtype=I32) * tm
    tile_valid = (tstart < gend[-1]).astype(I32)
    te_raw = jnp.minimum(jnp.sum((tstart[:, None] >= gend[None, :]).astype(I32), axis=1), n_exp - 1)
    te_last = te_raw[jnp.maximum(gend[-1] // tm - 1, 0)]
    tile_expert = jnp.where(tile_valid == 1, te_raw, te_last).astype(I32)

    hs = _dispatch(x2, g, src_tok, rows=_tile(p, ROWS_GATHER))
    us = _moe_up(hs, w1, w3, tile_expert, tile_valid, (goff // tm).astype(I32), (gsize // tm).astype(I32), tm)
    ys = _moe_down(us, w2, tile_expert, tile_valid, tm)
    return _combine(x2, ys, pos0, pos1, wt0, wt1, rows=_tile(t, ROWS_GATHER))


def _pack_kernel(ia_ref, ib_ref, sh_ref, nv_ref, a_ref, b_ref, o_ref, *, shift):
    j = pl.program_id(0)
    a = a_ref[0]
    tn = a.shape[1]
    shifted = jnp.concatenate([a[:, shift:], b_ref[0][:, :shift]], axis=1)
    out = jnp.where(sh_ref[j] == 1, shifted, a)
    lane = lax.broadcasted_iota(I32, (1, tn), 1)
    o_ref[...] = jnp.where(lane < nv_ref[j], out, jnp.zeros_like(out))


def _pack_w_in(w_in_bf, layer, d_model, mix_w):
    iq_w = IDX_HEADS * IDX_DH
    tn = iq_w
    n_main = 10 * mix_w
    small = IDX_DH + IDX_HEADS
    gates0 = n_main + iq_w + small
    n_gates = N_BRANCH * d_model
    assert n_main % tn == 0 and n_gates % tn == 0 and small < tn
    shift = gates0 % tn
    ia = ([j for j in range(n_main // tn)]
          + [gates0 // tn + q for q in range(n_gates // tn)]
          + [n_main // tn]
          + [(n_main + iq_w) // tn])
    nblk = len(ia)
    sh = [0] * (n_main // tn) + [1] * (n_gates // tn) + [0, 0]
    nv = [tn] * (nblk - 1) + [small]
    ib = [a + 1 if s_ == 1 else a for a, s_ in zip(ia, sh)]
    assert (max(ib) + 1) * tn >= gates0 + n_gates and max(ib) * tn < w_in_bf.shape[2]

    def amap(j, ia_r, ib_r, sh_r, nv_r):
        return (layer, 0, ia_r[j])

    def bmap(j, ia_r, ib_r, sh_r, nv_r):
        return (layer, 0, ib_r[j])

    return pl.pallas_call(
        functools.partial(_pack_kernel, shift=shift),
        out_shape=jax.ShapeDtypeStruct((d_model, nblk * tn), BF16),
        grid_spec=pltpu.PrefetchScalarGridSpec(
            num_scalar_prefetch=4,
            grid=(nblk,),
            in_specs=[pl.BlockSpec((1, d_model, tn), amap),
                      pl.BlockSpec((1, d_model, tn), bmap)],
            out_specs=pl.BlockSpec((d_model, tn), lambda j, ia_r, ib_r, sh_r, nv_r: (0, j))),
        compiler_params=_cparams(("arbitrary",)),
        name="pack_w_in",
    )(jnp.asarray(ia, I32), jnp.asarray(ib, I32), jnp.asarray(sh, I32), jnp.asarray(nv, I32),
      w_in_bf, w_in_bf)


def _mixer(x2, b, s, positions, rope, rel_bias, norm_g, w_in_bf, layer, w_br_l, w_o_l, gn_w, gn_b, conv_w,
           q_norm, k_norm):
    t, d_model = x2.shape
    mix_w = d_model // 2
    iq_w = IDX_HEADS * IDX_DH
    gate_col0 = 10 * mix_w
    iq_col0 = gate_col0 + N_BRANCH * d_model
    assert iq_col0 % iq_w == 0

    h = _rmsnorm(x2, norm_g)
    proj2 = _matmul(h, _pack_w_in(w_in_bf, layer, d_model, mix_w), BF16, *TILE_IN_PROJ, name="in_proj")
    proj3 = proj2.reshape(b, s, proj2.shape[1])
    pos_q = positions.reshape(b, s, 1)
    pos_k = positions.reshape(b, 1, s)

    y_ret = _retention_branch(proj3, rope, gn_w, gn_b, mix_w)
    y_conv = _conv_branch(proj3, conv_w, mix_w)

    ik0 = iq_col0 + iq_w
    kit = jnp.swapaxes(proj3[:, :, ik0:ik0 + IDX_DH], 1, 2)
    mask = _dsa_select(proj3, kit, s, min(TOPK_MAX, s // 4), iq_col0 // iq_w)
    qn, kn = _qk_norm(proj3, q_norm, k_norm, mix_w)
    y_att = _dsa_attention(qn, kn, proj3, mask, pos_q, pos_k, rel_bias, mix_w)

    ys = [y.reshape(t, mix_w) for y in (y_ret, y_conv, y_att)]
    merged = _branch_merge(ys, w_br_l.astype(BF16), proj2, d_model, mix_w, gate_col0)
    return _matmul(merged, w_o_l.astype(BF16), F32, *TILE_OUT_PROJ, res=x2, name="out_proj")


def kernel(x, positions, rel_bias, norm_mix, norm_ffn, w_in, w_br, w_o, ret_gn_w, ret_gn_b, conv_w,
           q_norm, k_norm, ffn_w1, ffn_w3, ffn_w2, moe_router, moe_w1, moe_w3, moe_w2):
    b, s, d_model = x.shape
    depth = w_in.shape[0]
    x2 = x.reshape(b * s, d_model)
    w_in_bf = w_in.astype(BF16)
    rope = _rope_tables(positions.reshape(b, s, 1), (d_model // 2) // RET_HEADS)
    for l in range(depth):
        x2 = _mixer(x2, b, s, positions, rope, rel_bias, norm_mix[l], w_in_bf, l, w_br[l], w_o[l],
                    ret_gn_w[l], ret_gn_b[l], conv_w[l], q_norm[l], k_norm[l])
        if l % 2 == 0:
            i = l // 2
            u = _glu(x2, norm_ffn[l], ffn_w1[i].astype(BF16), ffn_w3[i].astype(BF16))
            x2 = _matmul(u, ffn_w2[i].astype(BF16), F32, *TILE_FFN_DOWN, res=x2, name="ffn_down")
        else:
            i = l // 2
            x2 = _moe(x2, norm_ffn[l], moe_router[i], moe_w1[i], moe_w3[i], moe_w2[i].astype(BF16))
    return x2.reshape(b, s, d_model)
```
